```python
import jax
import jax.numpy as jnp
from jax import lax
import numpy as np

D_MODEL = 2048
BATCH = 4
SEQ = 8192
DEPTH = 1

CTX_LEN = 256
GRID_W = 64
HEAD_DIM = 128
ATTN_Q_HEADS = 8
ATTN_KV_HEADS = 2
ATTN_GROUPS = ATTN_Q_HEADS // ATTN_KV_HEADS
ATTN_WIDTH = ATTN_Q_HEADS * HEAD_DIM
KV_WIDTH = ATTN_KV_HEADS * HEAD_DIM
RET_HEADS = 8
RET_QK_DIM = 128
RET_V_DIM = 128
RET_QK_WIDTH = RET_HEADS * RET_QK_DIM
RET_V_WIDTH = RET_HEADS * RET_V_DIM
D_FF = 5632
Q_BLOCK = 128
RET_CHUNK = 128
ROPE_THETA = 10000.0
NORM_EPS = 1e-6
N_MOD = 9
PROJ_SPLITS = (ATTN_WIDTH, KV_WIDTH, KV_WIDTH, RET_QK_WIDTH, RET_QK_WIDTH, RET_V_WIDTH, RET_V_WIDTH, D_MODEL, D_MODEL)
PROJ_WIDTH = ATTN_WIDTH + 2 * KV_WIDTH + 2 * RET_QK_WIDTH + 2 * RET_V_WIDTH + 2 * D_MODEL

kernel_name = "hybrid_gqa_retention_macaron_dit_block"


def _rms(x):
    xf = x.astype(jnp.float32)
    return (xf * lax.rsqrt(jnp.mean(xf * xf, axis=-1, keepdims=True) + NORM_EPS)).astype(x.dtype)


def _modulate(n, shift, scale):
    return n * (1 + scale) + shift


def _adaln(cond, w_ada, b_ada):
    return jnp.split(jax.nn.silu(cond) @ w_ada + b_ada, N_MOD, axis=-1)


def _swiglu(x, w_in, w_out):
    a, b = jnp.split(x @ w_in, 2, axis=-1)
    return (jax.nn.silu(a) * b) @ w_out


def _split_proj(p):
    cuts = []
    acc = 0
    for w in PROJ_SPLITS[:-1]:
        acc += w
        cuts.append(acc)
    return jnp.split(p, cuts, axis=-1)


def _heads(t, n_heads, d):
    return t.reshape(t.shape[:2] + (n_heads, d))


def _grid_rope(seq_len):
    rows = seq_len // GRID_W
    row = jnp.repeat(jnp.arange(rows, dtype=jnp.float32), GRID_W)
    col = jnp.tile(jnp.arange(GRID_W, dtype=jnp.float32), rows)
    half = HEAD_DIM // 2
    inv_freq = ROPE_THETA ** (-jnp.arange(0, half, 2, dtype=jnp.float32) / half)
    ang = jnp.concatenate([row[:, None] * inv_freq, col[:, None] * inv_freq], axis=-1)
    return jnp.cos(ang), jnp.sin(ang)


def _apply_rope(x, cos, sin):
    xf = x.astype(jnp.float32).reshape(x.shape[:-1] + (HEAD_DIM // 2, 2))
    x0, x1 = xf[..., 0], xf[..., 1]
    c, s = cos[:, None, :], sin[:, None, :]
    out = jnp.stack([x0 * c - x1 * s, x0 * s + x1 * c], axis=-1)
    return out.reshape(x.shape).astype(x.dtype)


def _attend_blocks(q, k, v):
    b, l = q.shape[:2]
    nb = l // Q_BLOCK
    qb = jnp.moveaxis(q.reshape(b, nb, Q_BLOCK, ATTN_KV_HEADS, ATTN_GROUPS, HEAD_DIM), 1, 0)
    scale = HEAD_DIM ** -0.5

    def one_block(qblk):
        s = jnp.einsum("bqkgd,bskd->bkgqs", qblk, k).astype(jnp.float32) * scale
        p = jax.nn.softmax(s, axis=-1)
        return jnp.einsum("bkgqs,bskd->bqkgd", p.astype(v.dtype), v)

    out = lax.map(one_block, qb)
    return jnp.moveaxis(out, 0, 1).reshape(b, l, ATTN_WIDTH)


def _retention_chunkwise(q, k, v, log_gamma, state0):
    b, h, l, _ = q.shape
    n = l // RET_CHUNK
    idx = jnp.arange(RET_CHUNK, dtype=jnp.float32)
    diff = idx[:, None] - idx[None, :]
    lower = diff >= 0
    intra = jnp.where(lower[None], jnp.exp(jnp.where(lower, diff, 0.0)[None] * log_gamma[:, None, None]), 0.0)
    q_dec = jnp.exp((idx + 1.0)[None, :] * log_gamma[:, None])
    k_dec = jnp.exp((RET_CHUNK - 1.0 - idx)[None, :] * log_gamma[:, None])
    chunk_dec = jnp.exp(RET_CHUNK * log_gamma)

    def chunks(t):
        return jnp.moveaxis(t.reshape(b, h, n, RET_CHUNK, t.shape[-1]), 2, 0)

    def step(state, qkv):
        qc, kc, vc = qkv
        inner = jnp.einsum("bhid,bhjd->bhij", qc, kc) * intra
        y = jnp.einsum("bhij,bhje->bhie", inner, vc) + jnp.einsum("bhid,bhde->bhie", qc, state) * q_dec[..., None]
        state = state * chunk_dec[:, None, None] + jnp.einsum("bhjd,bhje->bhde", kc * k_dec[..., None], vc)
        return state, y

    _, ys = lax.scan(step, state0, (chunks(q), chunks(k), chunks(v)))
    return jnp.moveaxis(ys, 0, 2).reshape(b, h, l, v.shape[-1])


def _retention_state(k, v, log_gamma, reverse):
    l = k.shape[2]
    m = jnp.arange(l, dtype=jnp.float32)
    expo = m if reverse else (l - 1.0 - m)
    w = jnp.exp(expo[None, :] * log_gamma[:, None])
    return jnp.einsum("bhld,bhle,hl->bhde", k, v, w)


def _retention_bidir(q, k, v, lg_fwd, lg_bwd, state_fwd, state_bwd):
    flip = lambda t: jnp.flip(t, axis=2)
    y_f = _retention_chunkwise(q, k, v, lg_fwd, state_fwd)
    y_b = flip(_retention_chunkwise(flip(q), flip(k), flip(v), lg_bwd, state_bwd))
    return y_f + y_b


def _ret_heads(t, d):
    return jnp.transpose(_heads(t, RET_HEADS, d), (0, 2, 1, 3)).astype(jnp.float32)


def _retention_out(y, gate):
    b, h, l, d = y.shape
    y = jnp.transpose(_rms(y), (0, 2, 1, 3)).reshape(b, l, h * d).astype(gate.dtype)
    return jax.nn.silu(gate) * y


def _merge(y_attn, y_ret, g_attn, g_ret, w_proj_attn, w_proj_ret, w_out):
    return (jax.nn.sigmoid(g_attn) * (y_attn @ w_proj_attn) + jax.nn.sigmoid(g_ret) * (y_ret @ w_proj_ret)) @ w_out


def _mixer(n_x, n_c, w_in, q_gain, k_gain, decay_logit, w_proj_attn, w_proj_ret, w_out, with_ctx_out):
    seq_len = n_x.shape[1]
    qa_x, ka_x, va_x, qr_x, kr_x, vr_x, gr_x, ga_x, gb_x = _split_proj(n_x @ w_in)
    qa_c, ka_c, va_c, qr_c, kr_c, vr_c, gr_c, ga_c, gb_c = _split_proj(n_c @ w_in)

    cos, sin = _grid_rope(seq_len)
    q_x = _apply_rope(_rms(_heads(qa_x, ATTN_Q_HEADS, HEAD_DIM)) * q_gain, cos, sin)
    k_x = _apply_rope(_rms(_heads(ka_x, ATTN_KV_HEADS, HEAD_DIM)) * k_gain, cos, sin)
    k_c = _rms(_heads(ka_c, ATTN_KV_HEADS, HEAD_DIM)) * k_gain
    v_x = _heads(va_x, ATTN_KV_HEADS, HEAD_DIM)
    v_c = _heads(va_c, ATTN_KV_HEADS, HEAD_DIM)
    k_all = jnp.concatenate([k_c, k_x], axis=1)
    v_all = jnp.concatenate([v_c, v_x], axis=1)
    ya_x = _attend_blocks(q_x, k_all, v_all)

    log_gamma = jax.nn.log_sigmoid(decay_logit.astype(jnp.float32))
    k_scale = RET_QK_DIM ** -0.5
    qr_xh, kr_xh, vr_xh = _ret_heads(qr_x, RET_QK_DIM), _ret_heads(kr_x, RET_QK_DIM) * k_scale, _ret_heads(vr_x, RET_V_DIM)
    kr_ch, vr_ch = _ret_heads(kr_c, RET_QK_DIM) * k_scale, _ret_heads(vr_c, RET_V_DIM)
    state_f = _retention_state(kr_ch, vr_ch, log_gamma[0], False)
    state_b = _retention_state(kr_ch, vr_ch, log_gamma[1], True)
    yr_x = _retention_out(_retention_bidir(qr_xh, kr_xh, vr_xh, log_gamma[0], log_gamma[1], state_f, state_b), gr_x)

    out_x = _merge(ya_x, yr_x, ga_x, gb_x, w_proj_attn, w_proj_ret, w_out)
    if not with_ctx_out:
        return out_x, None

    q_c = _rms(_heads(qa_c, ATTN_Q_HEADS, HEAD_DIM)) * q_gain
    ya_c = _attend_blocks(q_c, k_c, v_c)
    zeros = jnp.zeros_like(state_f)
    qr_ch = _ret_heads(qr_c, RET_QK_DIM)
    yr_c = _retention_out(_retention_bidir(qr_ch, kr_ch, vr_ch, log_gamma[0], log_gamma[1], zeros, zeros), gr_c)
    out_c = _merge(ya_c, yr_c, ga_c, gb_c, w_proj_attn, w_proj_ret, w_out)
    return out_x, out_c


def setup_inputs(seed: int = 0) -> dict:
    key = jax.random.key(seed)
    ks = jax.random.split(key, 18)

    def nrm(k, shape, std):
        return jax.random.normal(k, shape, jnp.float32) * std

    heads_idx = jnp.arange(RET_HEADS, dtype=jnp.float32)
    decay_base = jnp.log1p(-(2.0 ** (-(5.0 + heads_idx)))) + (5.0 + heads_idx) * jnp.log(2.0)
    return {
        "x": nrm(ks[0], (BATCH, SEQ, D_MODEL), 1.0),
        "c": nrm(ks[1], (BATCH, D_MODEL), 1.0),
        "ctx": nrm(ks[2], (BATCH, CTX_LEN, D_MODEL), 1.0),
        "c_ctx": nrm(ks[3], (D_MODEL,), 1.0),
        "w_ada": nrm(ks[4], (DEPTH, D_MODEL, N_MOD * D_MODEL), 0.5 * D_MODEL ** -0.5),
        "b_ada": nrm(ks[5], (DEPTH, N_MOD * D_MODEL), 0.01),
        "ffn1_w_in": nrm(ks[6], (DEPTH, D_MODEL, 2 * D_FF), D_MODEL ** -0.5),
        "ffn1_w_out": nrm(ks[7], (DEPTH, D_FF, D_MODEL), D_FF ** -0.5),
        "mix_w_in": nrm(ks[8], (DEPTH, D_MODEL, PROJ_WIDTH), D_MODEL ** -0.5),
        "attn_q_gain": 1.0 + nrm(ks[9], (DEPTH, HEAD_DIM), 0.02),
        "attn_k_gain": 1.0 + nrm(ks[10], (DEPTH, HEAD_DIM), 0.02),
        "ret_decay_logit": decay_base[None, None, :] + nrm(ks[11], (DEPTH, 2, RET_HEADS), 0.1),
        "w_proj_attn": nrm(ks[12], (DEPTH, ATTN_WIDTH, D_MODEL), ATTN_WIDTH ** -0.5),
        "w_proj_ret": nrm(ks[13], (DEPTH, RET_V_WIDTH, D_MODEL), RET_V_WIDTH ** -0.5),
        "mix_w_out": nrm(ks[14], (DEPTH, D_MODEL, D_MODEL), D_MODEL ** -0.5),
        "ffn2_w_in": nrm(ks[15], (DEPTH, D_MODEL, 2 * D_FF), D_MODEL ** -0.5),
        "ffn2_w_out": nrm(ks[16], (DEPTH, D_FF, D_MODEL), D_FF ** -0.5),
        "final_norm": 1.0 + nrm(ks[17], (D_MODEL,), 0.02),
    }


def reference(x, c, ctx, c_ctx, w_ada, b_ada, ffn1_w_in, ffn1_w_out, mix_w_in, attn_q_gain, attn_k_gain,
              ret_decay_logit, w_proj_attn, w_proj_ret, mix_w_out, ffn2_w_in, ffn2_w_out, final_norm):
    h_x = x
    h_c = ctx
    for layer in range(DEPTH):
        last = layer == DEPTH - 1
        sh1, sc1, g1, sh2, sc2, g2, sh3, sc3, g3 = [t[:, None, :] for t in _adaln(c, w_ada[layer], b_ada[layer])]
        csh1, csc1, cg1, csh2, csc2, cg2, csh3, csc3, cg3 = _adaln(c_ctx, w_ada[layer], b_ada[layer])

        h_x = h_x + 0.5 * g1 * _swiglu(_modulate(_rms(h_x), sh1, sc1), ffn1_w_in[layer], ffn1_w_out[layer])
        h_c = h_c + 0.5 * cg1 * _swiglu(_modulate(_rms(h_c), csh1, csc1), ffn1_w_in[layer], ffn1_w_out[layer])

        y_x, y_c = _mixer(_modulate(_rms(h_x), sh2, sc2), _modulate(_rms(h_c), csh2, csc2),
                          mix_w_in[layer], attn_q_gain[layer], attn_k_gain[layer], ret_decay_logit[layer],
                          w_proj_attn[layer], w_proj_ret[layer], mix_w_out[layer], not last)
        h_x = h_x + g2 * y_x

        h_x = h_x + 0.5 * g3 * _swiglu(_modulate(_rms(h_x), sh3, sc3), ffn2_w_in[layer], ffn2_w_out[layer])
        if not last:
            h_c = h_c + cg2 * y_c
            h_c = h_c + 0.5 * cg3 * _swiglu(_modulate(_rms(h_c), csh3, csc3), ffn2_w_in[layer], ffn2_w_out[layer])
    return _rms(h_x) * final_norm
```

```python
import functools

import jax
import jax.numpy as jnp
from jax import lax
from jax.experimental import pallas as pl
from jax.experimental.pallas import tpu as pltpu

D_MODEL = 2048
CTX_LEN = 256
GRID_W = 64
HEAD_DIM = 128
ATTN_Q_HEADS = 8
ATTN_KV_HEADS = 2
ATTN_GROUPS = ATTN_Q_HEADS // ATTN_KV_HEADS
ATTN_WIDTH = ATTN_Q_HEADS * HEAD_DIM
KV_WIDTH = ATTN_KV_HEADS * HEAD_DIM
RET_HEADS = 8
RET_DIM = 128
RET_WIDTH = RET_HEADS * RET_DIM
D_FF = 5632
ROPE_THETA = 10000.0
NORM_EPS = 1e-6
N_MOD = 9

QKV_WIDTH = ATTN_WIDTH + 2 * KV_WIDTH
RET_OFF = QKV_WIDTH
GR_OFF = RET_OFF + 3 * RET_WIDTH
GA_OFF = GR_OFF + RET_WIDTH
GB_OFF = GA_OFF + D_MODEL

F32 = jnp.float32
BF16 = jnp.bfloat16

VMEM_LIMIT_BYTES = 56 * 1024 * 1024

TOKEN_TILE = 512
FF_TILE = 512
MERGE_TILE = 512
ATTN_Q_TILE = 256
ATTN_KV_TILE = 768
RET_CHUNK = 256


def _params(*sem):
    return pltpu.CompilerParams(dimension_semantics=sem, vmem_limit_bytes=VMEM_LIMIT_BYTES)


def _rms(x):
    return x * lax.rsqrt(jnp.mean(x * x, axis=-1, keepdims=True) + NORM_EPS)


def _sigmoid(x):
    return 1.0 / (1.0 + jnp.exp(-x))


def _adaln_kernel(c_ref, w_ref, b_ref, o_ref):
    c = c_ref[...]
    s = (c * _sigmoid(c)).astype(BF16)
    o_ref[...] = jnp.dot(s, w_ref[...].astype(BF16), preferred_element_type=F32) + b_ref[...]


def _adaln(cond, w, b):
    n = w.shape[1]
    tn = 1024
    return pl.pallas_call(
        _adaln_kernel,
        grid=(n // tn,),
        in_specs=[
            pl.BlockSpec((8, D_MODEL), lambda j: (0, 0)),
            pl.BlockSpec((D_MODEL, tn), lambda j: (0, j)),
            pl.BlockSpec((1, tn), lambda j: (0, j)),
        ],
        out_specs=pl.BlockSpec((8, tn), lambda j: (0, j)),
        out_shape=jax.ShapeDtypeStruct((8, n), F32),
        compiler_params=_params("arbitrary"),
        name="adaln",
    )(cond, w, b)


def _ffn_kernel(*refs, mod_base, emit_next, final):
    h_ref, mod_ref, wa_ref, wb_ref, wo_ref = refs[:5]
    refs = refs[5:]
    if final:
        fn_ref, refs = refs[0], refs[1:]
    out_ref, refs = refs[0], refs[1:]
    if emit_next:
        nxt_ref, refs = refs[0], refs[1:]
    xn_sc, acc_sc = refs
    j = pl.program_id(1)

    @pl.when(j == 0)
    def _():
        n = _rms(h_ref[...])
        n = n * (1.0 + mod_ref[0, mod_base + 1:mod_base + 2, :]) + mod_ref[0, mod_base:mod_base + 1, :]
        xn_sc[...] = n.astype(BF16)
        acc_sc[...] = jnp.zeros_like(acc_sc)

    xn = xn_sc[...]
    a = jnp.dot(xn, wa_ref[...], preferred_element_type=F32)
    b = jnp.dot(xn, wb_ref[...], preferred_element_type=F32)
    act = (a * _sigmoid(a) * b).astype(BF16)
    acc_sc[...] += jnp.dot(act, wo_ref[...], preferred_element_type=F32)

    @pl.when(j == pl.num_programs(1) - 1)
    def _():
        gate = mod_ref[0, mod_base + 2:mod_base + 3, :]
        h = h_ref[...] + (0.5 * gate) * acc_sc[...]
        if final:
            out_ref[...] = _rms(h) * fn_ref[...]
        else:
            out_ref[...] = h
        if emit_next:
            n = _rms(h)
            n = n * (1.0 + mod_ref[0, mod_base + 4:mod_base + 5, :]) + mod_ref[0, mod_base + 3:mod_base + 4, :]
            nxt_ref[...] = n.astype(BF16)


def _ffn(h, mod, w_in, w_out, *, mod_base, tiles_per_row, row_offset, emit_next=False, final_norm=None):
    t = h.shape[0]
    tm, tf = TOKEN_TILE, FF_TILE
    nf = D_FF // tf
    final = final_norm is not None
    mod_map = lambda i, j: (i // tiles_per_row + row_offset, 0, 0)
    in_specs = [
        pl.BlockSpec((tm, D_MODEL), lambda i, j: (i, 0)),
        pl.BlockSpec((1, N_MOD, D_MODEL), mod_map),
        pl.BlockSpec((D_MODEL, tf), lambda i, j: (0, j)),
        pl.BlockSpec((D_MODEL, tf), lambda i, j: (0, j + nf)),
        pl.BlockSpec((tf, D_MODEL), lambda i, j: (j, 0)),
    ]
    args = [h, mod, w_in, w_in, w_out]
    if final:
        in_specs.append(pl.BlockSpec((1, D_MODEL), lambda i, j: (0, 0)))
        args.append(final_norm)
    out_specs = [pl.BlockSpec((tm, D_MODEL), lambda i, j: (i, 0))]
    out_shape = [jax.ShapeDtypeStruct((t, D_MODEL), F32)]
    if emit_next:
        out_specs.append(pl.BlockSpec((tm, D_MODEL), lambda i, j: (i, 0)))
        out_shape.append(jax.ShapeDtypeStruct((t, D_MODEL), BF16))
    return pl.pallas_call(
        functools.partial(_ffn_kernel, mod_base=mod_base, emit_next=emit_next, final=final),
        grid=(t // tm, nf),
        in_specs=in_specs,
        out_specs=out_specs,
        out_shape=out_shape,
        scratch_shapes=[pltpu.VMEM((tm, D_MODEL), BF16), pltpu.VMEM((tm, D_MODEL), F32)],
        compiler_params=_params("parallel", "arbitrary"),
        name="ffn_final" if final else "ffn",
    )(*args)


def _qkv_kernel(n_ref, w_ref, cos_ref, sin_ref, gq_ref, gk_ref, o_ref):
    j = pl.program_id(1)
    y = jnp.dot(n_ref[...], w_ref[...], preferred_element_type=F32)
    cos = cos_ref[...]
    sin = sin_ref[...]
    even = lax.broadcasted_iota(jnp.int32, cos.shape, 1) % 2 == 0

    def rope_head(yh, gain):
        t = _rms(yh) * gain
        partner = jnp.where(even, pltpu.roll(t, HEAD_DIM - 1, 1), pltpu.roll(t, 1, 1))
        return t * cos + partner * sin

    @pl.when(j < ATTN_WIDTH // 512)
    def _():
        gq = gq_ref[...]
        for hh in range(4):
            sl = slice(hh * HEAD_DIM, (hh + 1) * HEAD_DIM)
            o_ref[:, sl] = (rope_head(y[:, sl], gq) * (HEAD_DIM ** -0.5)).astype(BF16)

    @pl.when(j == ATTN_WIDTH // 512)
    def _():
        gk = gk_ref[...]
        for hh in range(ATTN_KV_HEADS):
            sl = slice(hh * HEAD_DIM, (hh + 1) * HEAD_DIM)
            o_ref[:, sl] = rope_head(y[:, sl], gk).astype(BF16)
        o_ref[:, KV_WIDTH:] = y[:, KV_WIDTH:].astype(BF16)


def _qkv_proj(n, w, cos, sin, gq, gk, *, pos_tiles):
    t = n.shape[0]
    tm, tn = TOKEN_TILE, 512
    return pl.pallas_call(
        _qkv_kernel,
        grid=(t // tm, QKV_WIDTH // tn),
        in_specs=[
            pl.BlockSpec((tm, D_MODEL), lambda i, j: (i, 0)),
            pl.BlockSpec((D_MODEL, tn), lambda i, j: (0, j)),
            pl.BlockSpec((tm, HEAD_DIM), lambda i, j: (i % pos_tiles, 0)),
            pl.BlockSpec((tm, HEAD_DIM), lambda i, j: (i % pos_tiles, 0)),
            pl.BlockSpec((1, HEAD_DIM), lambda i, j: (0, 0)),
            pl.BlockSpec((1, HEAD_DIM), lambda i, j: (0, 0)),
        ],
        out_specs=pl.BlockSpec((tm, tn), lambda i, j: (i, j)),
        out_shape=jax.ShapeDtypeStruct((t, QKV_WIDTH), BF16),
        compiler_params=_params("parallel", "arbitrary"),
        name="qkv_proj",
    )(n, w, cos, sin, gq, gk)


def _scaled_mm_kernel(n_ref, w_ref, s_ref, o_ref):
    y = jnp.dot(n_ref[...], w_ref[...], preferred_element_type=F32)
    o_ref[...] = (y * s_ref[...]).astype(o_ref.dtype)


def _scaled_mm(n, w, col_scale, out_dtype, name):
    t = n.shape[0]
    nn = w.shape[1]
    tm, tn = TOKEN_TILE, 1024
    return pl.pallas_call(
        _scaled_mm_kernel,
        grid=(t // tm, nn // tn),
        in_specs=[
            pl.BlockSpec((tm, D_MODEL), lambda i, j: (i, 0)),
            pl.BlockSpec((D_MODEL, tn), lambda i, j: (0, j)),
            pl.BlockSpec((1, tn), lambda i, j: (0, j)),
        ],
        out_specs=pl.BlockSpec((tm, tn), lambda i, j: (i, j)),
        out_shape=jax.ShapeDtypeStruct((t, nn), out_dtype),
        compiler_params=_params("parallel", "arbitrary"),
        name=name,
    )(n, w, col_scale)


def _attn_kernel(q_ref, k_ref, v_ref, o_ref, qs_sc, m_sc, l_sc, acc_sc):
    tq = q_ref.shape[1]
    tk = ATTN_KV_TILE
    nk = k_ref.shape[1] // tk
    for g in range(ATTN_GROUPS):
        qs_sc[g * tq:(g + 1) * tq, :] = q_ref[0, :, g * HEAD_DIM:(g + 1) * HEAD_DIM]
    m_sc[...] = jnp.full_like(m_sc, -jnp.inf)
    l_sc[...] = jnp.zeros_like(l_sc)
    acc_sc[...] = jnp.zeros_like(acc_sc)

    def body(c, carry):
        start = pl.multiple_of(c * tk, tk)
        k = k_ref[0, pl.ds(start, tk), :]
        v = v_ref[0, pl.ds(start, tk), :]
        s = lax.dot_general(qs_sc[...], k, (((1,), (1,)), ((), ())), preferred_element_type=F32)
        m_prev = m_sc[...]
        m_new = jnp.maximum(m_prev, jnp.max(s, axis=-1, keepdims=True))
        alpha = jnp.exp(m_prev - m_new)
        p = jnp.exp(s - m_new)
        l_sc[...] = alpha * l_sc[...] + jnp.sum(p, axis=-1, keepdims=True)
        acc_sc[...] = alpha * acc_sc[...] + jnp.dot(p.astype(BF16), v, preferred_element_type=F32)
        m_sc[...] = m_new
        return carry

    lax.fori_loop(0, nk, body, 0)
    out = acc_sc[...] / l_sc[...]
    for g in range(ATTN_GROUPS):
        o_ref[0, :, g * HEAD_DIM:(g + 1) * HEAD_DIM] = out[g * tq:(g + 1) * tq, :].astype(BF16)


def _attention(qkv, k_all, v_all):
    b, l, _ = qkv.shape
    lk = k_all.shape[1]
    tq = ATTN_Q_TILE
    gw = ATTN_GROUPS * HEAD_DIM
    rows = ATTN_GROUPS * tq
    return pl.pallas_call(
        _attn_kernel,
        grid=(b, ATTN_KV_HEADS, l // tq),
        in_specs=[
            pl.BlockSpec((1, tq, gw), lambda bi, hi, qi: (bi, qi, hi)),
            pl.BlockSpec((1, lk, HEAD_DIM), lambda bi, hi, qi: (bi, 0, hi)),
            pl.BlockSpec((1, lk, HEAD_DIM), lambda bi, hi, qi: (bi, 0, hi)),
        ],
        out_specs=pl.BlockSpec((1, tq, gw), lambda bi, hi, qi: (bi, qi, hi)),
        out_shape=jax.ShapeDtypeStruct((b, l, ATTN_WIDTH), BF16),
        scratch_shapes=[
            pltpu.VMEM((rows, HEAD_DIM), BF16),
            pltpu.VMEM((rows, 1), F32),
            pltpu.VMEM((rows, 1), F32),
            pltpu.VMEM((rows, HEAD_DIM), F32),
        ],
        compiler_params=_params("parallel", "parallel", "arbitrary"),
        name="attention",
    )(qkv, k_all, v_all)


def _ret_kernel(lg_ref, q_ref, k_ref, v_ref, g_ref, kc_ref, vc_ref, o_ref, uf_sc, sb_sc):
    c_len = RET_CHUNK
    n_chunks = q_ref.shape[1] // c_len
    n_ctx = kc_ref.shape[1]
    head = pl.program_id(1)
    lgf = lg_ref[0, head]
    lgb = lg_ref[1, head]

    row = lax.broadcasted_iota(jnp.int32, (c_len, 1), 0).astype(F32)
    vdec_f = jnp.exp((c_len - 1.0 - row) * lgf)
    vdec_b = jnp.exp(row * lgb)
    qdec_f = jnp.exp((row + 1.0) * lgf)
    qdec_b = jnp.exp((c_len - row) * lgb)
    chunk_f = jnp.exp(jnp.full((1, RET_DIM), c_len, F32) * lgf)
    chunk_b = jnp.exp(jnp.full((1, RET_DIM), c_len, F32) * lgb)
    diff = (lax.broadcasted_iota(jnp.int32, (c_len, c_len), 0)
            - lax.broadcasted_iota(jnp.int32, (c_len, c_len), 1)).astype(F32)
    decay = (jnp.where(diff >= 0, jnp.exp(jnp.maximum(diff, 0.0) * lgf), 0.0)
             + jnp.where(diff <= 0, jnp.exp(jnp.maximum(-diff, 0.0) * lgb), 0.0))

    def kv_outer(k, v, dec_f, dec_b):
        vf = v.astype(F32)
        v2 = jnp.concatenate([(vf * dec_f).astype(BF16), (vf * dec_b).astype(BF16)], axis=1)
        return lax.dot_general(k, v2, (((0,), (0,)), ((), ())), preferred_element_type=F32)

    crow = lax.broadcasted_iota(jnp.int32, (n_ctx, 1), 0).astype(F32)
    s0 = kv_outer(kc_ref[0], vc_ref[0], jnp.exp((n_ctx - 1.0 - crow) * lgf), jnp.exp(crow * lgb))

    def back_body(t, sb):
        c = n_chunks - 1 - t
        start = pl.multiple_of(c * c_len, c_len)
        sb_sc[c] = sb
        u = kv_outer(k_ref[0, pl.ds(start, c_len), :], v_ref[0, pl.ds(start, c_len), :], vdec_f, vdec_b)
        uf_sc[c] = u[:, :RET_DIM]
        return sb * chunk_b + u[:, RET_DIM:]

    lax.fori_loop(0, n_chunks, back_body, s0[:, RET_DIM:])

    def fwd_body(c, sf):
        start = pl.multiple_of(c * c_len, c_len)
        q = q_ref[0, pl.ds(start, c_len), :]
        k = k_ref[0, pl.ds(start, c_len), :]
        v = v_ref[0, pl.ds(start, c_len), :]
        inner = lax.dot_general(q, k, (((1,), (1,)), ((), ())), preferred_element_type=F32)
        y = jnp.dot((inner * decay).astype(BF16), v, preferred_element_type=F32)
        states = jnp.concatenate([sf.astype(BF16), sb_sc[c].astype(BF16)], axis=1)
        cross = jnp.dot(q, states, preferred_element_type=F32)
        y = y + cross[:, :RET_DIM] * qdec_f + cross[:, RET_DIM:] * qdec_b
        gate = g_ref[0, pl.ds(start, c_len), :]
        o_ref[0, pl.ds(start, c_len), :] = (gate * _sigmoid(gate) * _rms(y)).astype(BF16)
        return sf * chunk_f + uf_sc[c]

    lax.fori_loop(0, n_chunks, fwd_body, s0[:, :RET_DIM])


def _retention(log_gamma, ret_x, gate_x, ret_c):
    b, l, _ = ret_x.shape
    lc = ret_c.shape[1]
    n_chunks = l // RET_CHUNK
    seq = lambda off: pl.BlockSpec((1, l, RET_DIM), lambda bi, hi: (bi, 0, hi + off))
    ctx = lambda off: pl.BlockSpec((1, lc, RET_DIM), lambda bi, hi: (bi, 0, hi + off))
    return pl.pallas_call(
        _ret_kernel,
        grid=(b, RET_HEADS),
        in_specs=[
            pl.BlockSpec(memory_space=pltpu.SMEM),
            seq(0), seq(RET_HEADS), seq(2 * RET_HEADS), seq(0),
            ctx(RET_HEADS), ctx(2 * RET_HEADS),
        ],
        out_specs=seq(0),
        out_shape=jax.ShapeDtypeStruct((b, l, RET_WIDTH), BF16),
        scratch_shapes=[
            pltpu.VMEM((n_chunks, RET_DIM, RET_DIM), F32),
            pltpu.VMEM((n_chunks, RET_DIM, RET_DIM), F32),
        ],
        compiler_params=_params("parallel", "arbitrary"),
        name="retention",
    )(log_gamma, ret_x, ret_x, ret_x, gate_x, ret_c, ret_c)


def _merge_kernel(h_ref, n_ref, ya_ref, yr_ref, mod_ref, wga_ref, wgb_ref, wpa_ref, wpr_ref, wo_ref,
                  out_ref, acc_sc):
    j = pl.program_id(1)

    @pl.when(j == 0)
    def _():
        acc_sc[...] = jnp.zeros_like(acc_sc)

    n = n_ref[...]
    ga = jnp.dot(n, wga_ref[...], preferred_element_type=F32)
    gb = jnp.dot(n, wgb_ref[...], preferred_element_type=F32)
    pa = jnp.dot(ya_ref[...], wpa_ref[...], preferred_element_type=F32)
    pr = jnp.dot(yr_ref[...], wpr_ref[...], preferred_element_type=F32)
    z = (_sigmoid(ga) * pa + _sigmoid(gb) * pr).astype(BF16)
    acc_sc[...] += jnp.dot(z, wo_ref[...], preferred_element_type=F32)

    @pl.when(j == pl.num_programs(1) - 1)
    def _():
        out_ref[...] = h_ref[...] + mod_ref[0, 5:6, :] * acc_sc[...]


def _merge(h, n, ya, yr, mod, w_ga, w_gb, w_pa, w_pr, w_out, *, tiles_per_row):
    t = h.shape[0]
    tm, tc = TOKEN_TILE, MERGE_TILE
    return pl.pallas_call(
        _merge_kernel,
        grid=(t // tm, D_MODEL // tc),
        in_specs=[
            pl.BlockSpec((tm, D_MODEL), lambda i, j: (i, 0)),
            pl.BlockSpec((tm, D_MODEL), lambda i, j: (i, 0)),
            pl.BlockSpec((tm, ATTN_WIDTH), lambda i, j: (i, 0)),
            pl.BlockSpec((tm, RET_WIDTH), lambda i, j: (i, 0)),
            pl.BlockSpec((1, N_MOD, D_MODEL), lambda i, j: (i // tiles_per_row, 0, 0)),
            pl.BlockSpec((D_MODEL, tc), lambda i, j: (0, j)),
            pl.BlockSpec((D_MODEL, tc), lambda i, j: (0, j)),
            pl.BlockSpec((ATTN_WIDTH, tc), lambda i, j: (0, j)),
            pl.BlockSpec((RET_WIDTH, tc), lambda i, j: (0, j)),
            pl.BlockSpec((tc, D_MODEL), lambda i, j: (j, 0)),
        ],
        out_specs=pl.BlockSpec((tm, D_MODEL), lambda i, j: (i, 0)),
        out_shape=jax.ShapeDtypeStruct((t, D_MODEL), F32),
        scratch_shapes=[pltpu.VMEM((tm, D_MODEL), F32)],
        compiler_params=_params("parallel", "arbitrary"),
        name="merge",
    )(h, n, ya, yr, mod, w_ga, w_gb, w_pa, w_pr, w_out)


def _rope_tables(seq_len):
    rows = seq_len // GRID_W
    row = jnp.repeat(jnp.arange(rows, dtype=F32), GRID_W)
    col = jnp.tile(jnp.arange(GRID_W, dtype=F32), rows)
    half = HEAD_DIM // 2
    inv_freq = ROPE_THETA ** (-jnp.arange(0, half, 2, dtype=F32) / half)
    ang = jnp.concatenate([row[:, None] * inv_freq, col[:, None] * inv_freq], axis=-1)
    cos = jnp.repeat(jnp.cos(ang), 2, axis=-1)
    sin = jnp.repeat(jnp.sin(ang), 2, axis=-1)
    sign = jnp.tile(jnp.array([-1.0, 1.0], F32), half)
    return cos, sin * sign


def kernel(x, c, ctx, c_ctx, w_ada, b_ada, ffn1_w_in, ffn1_w_out, mix_w_in, attn_q_gain, attn_k_gain,
           ret_decay_logit, w_proj_attn, w_proj_ret, mix_w_out, ffn2_w_in, ffn2_w_out, final_norm):
    batch, seq_len, _ = x.shape
    assert w_ada.shape[0] == 1, "single-layer block"
    assert seq_len % TOKEN_TILE == 0 and (batch * CTX_LEN) % TOKEN_TILE == 0
    tiles_per_row = seq_len // TOKEN_TILE

    w1_in, w1_out = ffn1_w_in[0].astype(BF16), ffn1_w_out[0].astype(BF16)
    w2_in, w2_out = ffn2_w_in[0].astype(BF16), ffn2_w_out[0].astype(BF16)
    w_mix = mix_w_in[0]
    w_qkv = w_mix[:, :QKV_WIDTH].astype(BF16)
    w_ret = w_mix[:, RET_OFF:GR_OFF].astype(BF16)
    w_gr = w_mix[:, GR_OFF:GA_OFF].astype(BF16)
    w_ga = w_mix[:, GA_OFF:GB_OFF].astype(BF16)
    w_gb = w_mix[:, GB_OFF:].astype(BF16)
    w_pa, w_pr = w_proj_attn[0].astype(BF16), w_proj_ret[0].astype(BF16)
    w_mo = mix_w_out[0].astype(BF16)

    cond = jnp.zeros((8, D_MODEL), F32).at[:batch].set(c).at[batch].set(c_ctx)
    mod = _adaln(cond, w_ada[0], b_ada).reshape(8, N_MOD, D_MODEL)

    x2 = x.reshape(batch * seq_len, D_MODEL)
    c2 = ctx.reshape(batch * CTX_LEN, D_MODEL)
    h1, n2 = _ffn(x2, mod, w1_in, w1_out, mod_base=0, tiles_per_row=tiles_per_row, row_offset=0,
                  emit_next=True)
    _, n2c = _ffn(c2, mod, w1_in, w1_out, mod_base=0, tiles_per_row=batch * CTX_LEN, row_offset=batch,
                  emit_next=True)

    cos, sin = _rope_tables(seq_len)
    gq = attn_q_gain[0].reshape(1, HEAD_DIM)
    gk = attn_k_gain[0].reshape(1, HEAD_DIM)
    qkv_x = _qkv_proj(n2, w_qkv, cos, sin, gq, gk, pos_tiles=tiles_per_row)
    ones = jnp.ones((TOKEN_TILE, HEAD_DIM), F32)
    qkv_c = _qkv_proj(n2c, w_qkv, ones, jnp.zeros_like(ones), gq, gk, pos_tiles=1)

    ret_scale = jnp.concatenate([jnp.ones((1, RET_WIDTH), F32),
                                 jnp.full((1, RET_WIDTH), RET_DIM ** -0.5, F32),
                                 jnp.ones((1, RET_WIDTH), F32)], axis=1)
    ret_x = _scaled_mm(n2, w_ret, ret_scale, BF16, "ret_proj")
    ret_c = _scaled_mm(n2c, w_ret, ret_scale, BF16, "ret_proj")
    gate_x = _scaled_mm(n2, w_gr, jnp.ones((1, RET_WIDTH), F32), F32, "ret_gate_proj")

    qkv_x = qkv_x.reshape(batch, seq_len, QKV_WIDTH)
    qkv_c = qkv_c.reshape(batch, CTX_LEN, QKV_WIDTH)
    k_all = jnp.concatenate([qkv_c[:, :, ATTN_WIDTH:ATTN_WIDTH + KV_WIDTH],
                             qkv_x[:, :, ATTN_WIDTH:ATTN_WIDTH + KV_WIDTH]], axis=1)
    v_all = jnp.concatenate([qkv_c[:, :, ATTN_WIDTH + KV_WIDTH:], qkv_x[:, :, ATTN_WIDTH + KV_WIDTH:]], axis=1)
    ya = _attention(qkv_x, k_all, v_all)

    log_gamma = jax.nn.log_sigmoid(ret_decay_logit[0].astype(F32))
    yr = _retention(log_gamma, ret_x.reshape(batch, seq_len, 3 * RET_WIDTH),
                    gate_x.reshape(batch, seq_len, RET_WIDTH),
                    ret_c.reshape(batch, CTX_LEN, 3 * RET_WIDTH))

    h2 = _merge(h1, n2, ya.reshape(batch * seq_len, ATTN_WIDTH), yr.reshape(batch * seq_len, RET_WIDTH), mod,
                w_ga, w_gb, w_pa, w_pr, w_mo, tiles_per_row=tiles_per_row)
    out = _ffn(h2, mod, w2_in, w2_out, mod_base=6, tiles_per_row=tiles_per_row, row_offset=0,
               final_norm=final_norm.reshape(1, D_MODEL))
    return out[0].reshape(batch, seq_len, D_MODEL)
```

```python
import functools

import jax
import jax.numpy as jnp
from jax import lax
from jax.experimental import pallas as pl
from jax.experimental.pallas import tpu as pltpu

D_MODEL = 2048
CTX_LEN = 256
GRID_W = 64
HEAD_DIM = 128
ATTN_Q_HEADS = 8
ATTN_KV_HEADS = 2
ATTN_GROUPS = ATTN_Q_HEADS // ATTN_KV_HEADS
ATTN_WIDTH = ATTN_Q_HEADS * HEAD_DIM
KV_WIDTH = ATTN_KV_HEADS * HEAD_DIM
RET_HEADS = 8
RET_DIM = 128
RET_WIDTH = RET_HEADS * RET_DIM
D_FF = 5632
ROPE_THETA = 10000.0
NORM_EPS = 1e-6
N_MOD = 9

QKV_WIDTH = ATTN_WIDTH + 2 * KV_WIDTH
RET_OFF = QKV_WIDTH
GR_OFF = RET_OFF + 3 * RET_WIDTH
GA_OFF = GR_OFF + RET_WIDTH
GB_OFF = GA_OFF + D_MODEL

F32 = jnp.float32
BF16 = jnp.bfloat16

VMEM_LIMIT_BYTES = 56 * 1024 * 1024

TOKEN_TILE = 512
FF_TILE = 512
MERGE_TILE = 512
ATTN_Q_TILE = 256
ATTN_KV_TILE = 768
RET_CHUNK = 256

Q_SCALE = HEAD_DIM ** -0.5 * 1.4426950408889634
ATTN_SAFE_SCORE = 32.0


def _params(*sem):
    return pltpu.CompilerParams(dimension_semantics=sem, vmem_limit_bytes=VMEM_LIMIT_BYTES)


def _rms(x):
    return x * lax.rsqrt(jnp.mean(x * x, axis=-1, keepdims=True) + NORM_EPS)


def _sigmoid(x):
    return 1.0 / (1.0 + jnp.exp(-x))


def _adaln_kernel(c_ref, w_ref, b_ref, o_ref):
    c = c_ref[...]
    s = (c * _sigmoid(c)).astype(BF16)
    o_ref[...] = jnp.dot(s, w_ref[...].astype(BF16), preferred_element_type=F32) + b_ref[...]


def _adaln(cond, w, b):
    n = w.shape[1]
    tn = 1024
    return pl.pallas_call(
        _adaln_kernel,
        grid=(n // tn,),
        in_specs=[
            pl.BlockSpec((8, D_MODEL), lambda j: (0, 0)),
            pl.BlockSpec((D_MODEL, tn), lambda j: (0, j)),
            pl.BlockSpec((1, tn), lambda j: (0, j)),
        ],
        out_specs=pl.BlockSpec((8, tn), lambda j: (0, j)),
        out_shape=jax.ShapeDtypeStruct((8, n), F32),
        compiler_params=_params("arbitrary"),
        name="adaln",
    )(cond, w, b)


def _ffn_kernel(*refs, mod_base, emit_next, final):
    h_ref, mod_ref, wa_ref, wb_ref, wo_ref = refs[:5]
    refs = refs[5:]
    if final:
        fn_ref, refs = refs[0], refs[1:]
    out_ref, refs = refs[0], refs[1:]
    if emit_next:
        nxt_ref, refs = refs[0], refs[1:]
    xn_sc, acc_sc = refs
    j = pl.program_id(1)

    @pl.when(j == 0)
    def _():
        n = _rms(h_ref[...])
        n = n * (1.0 + mod_ref[0, mod_base + 1:mod_base + 2, :]) + mod_ref[0, mod_base:mod_base + 1, :]
        xn_sc[...] = n.astype(BF16)
        acc_sc[...] = jnp.zeros_like(acc_sc)

    xn = xn_sc[...]
    a = jnp.dot(xn, wa_ref[...], preferred_element_type=F32)
    b = jnp.dot(xn, wb_ref[...], preferred_element_type=F32)
    act = (a * _sigmoid(a) * b).astype(BF16)
    acc_sc[...] += jnp.dot(act, wo_ref[...], preferred_element_type=F32)

    @pl.when(j == pl.num_programs(1) - 1)
    def _():
        gate = mod_ref[0, mod_base + 2:mod_base + 3, :]
        h = h_ref[...] + (0.5 * gate) * acc_sc[...]
        if final:
            out_ref[...] = _rms(h) * fn_ref[...]
        else:
            out_ref[...] = h
        if emit_next:
            n = _rms(h)
            n = n * (1.0 + mod_ref[0, mod_base + 4:mod_base + 5, :]) + mod_ref[0, mod_base + 3:mod_base + 4, :]
            nxt_ref[...] = n.astype(BF16)


def _ffn(h, mod, w_in, w_out, *, mod_base, tiles_per_row, row_offset, emit_next=False, final_norm=None):
    t = h.shape[0]
    tm, tf = TOKEN_TILE, FF_TILE
    nf = D_FF // tf
    final = final_norm is not None
    mod_map = lambda i, j: (i // tiles_per_row + row_offset, 0, 0)
    in_specs = [
        pl.BlockSpec((tm, D_MODEL), lambda i, j: (i, 0)),
        pl.BlockSpec((1, N_MOD, D_MODEL), mod_map),
        pl.BlockSpec((D_MODEL, tf), lambda i, j: (0, j)),
        pl.BlockSpec((D_MODEL, tf), lambda i, j: (0, j + nf)),
        pl.BlockSpec((tf, D_MODEL), lambda i, j: (j, 0)),
    ]
    args = [h, mod, w_in, w_in, w_out]
    if final:
        in_specs.append(pl.BlockSpec((1, D_MODEL), lambda i, j: (0, 0)))
        args.append(final_norm)
    out_specs = [pl.BlockSpec((tm, D_MODEL), lambda i, j: (i, 0))]
    out_shape = [jax.ShapeDtypeStruct((t, D_MODEL), F32)]
    if emit_next:
        out_specs.append(pl.BlockSpec((tm, D_MODEL), lambda i, j: (i, 0)))
        out_shape.append(jax.ShapeDtypeStruct((t, D_MODEL), BF16))
    return pl.pallas_call(
        functools.partial(_ffn_kernel, mod_base=mod_base, emit_next=emit_next, final=final),
        grid=(t // tm, nf),
        in_specs=in_specs,
        out_specs=out_specs,
        out_shape=out_shape,
        scratch_shapes=[pltpu.VMEM((tm, D_MODEL), BF16), pltpu.VMEM((tm, D_MODEL), F32)],
        compiler_params=_params("parallel", "arbitrary"),
        name="ffn_final" if final else "ffn",
    )(*args)


def _qkv_kernel(n_ref, w_ref, cos_ref, sin_ref, gq_ref, gk_ref, o_ref):
    j = pl.program_id(1)
    y = jnp.dot(n_ref[...], w_ref[...], preferred_element_type=F32)
    cos = cos_ref[...]
    sin = sin_ref[...]
    even = lax.broadcasted_iota(jnp.int32, cos.shape, 1) % 2 == 0

    def rope_head(yh, gain):
        t = _rms(yh) * gain
        partner = jnp.where(even, pltpu.roll(t, HEAD_DIM - 1, 1), pltpu.roll(t, 1, 1))
        return t * cos + partner * sin

    @pl.when(j < ATTN_WIDTH // 512)
    def _():
        gq = gq_ref[...]
        for hh in range(4):
            sl = slice(hh * HEAD_DIM, (hh + 1) * HEAD_DIM)
            o_ref[:, sl] = (rope_head(y[:, sl], gq) * Q_SCALE).astype(BF16)

    @pl.when(j == ATTN_WIDTH // 512)
    def _():
        gk = gk_ref[...]
        for hh in range(ATTN_KV_HEADS):
            sl = slice(hh * HEAD_DIM, (hh + 1) * HEAD_DIM)
            o_ref[:, sl] = rope_head(y[:, sl], gk).astype(BF16)
        o_ref[:, KV_WIDTH:] = y[:, KV_WIDTH:].astype(BF16)


def _qkv_proj(n, w, cos, sin, gq, gk, *, pos_tiles):
    t = n.shape[0]
    tm, tn = TOKEN_TILE, 512
    return pl.pallas_call(
        _qkv_kernel,
        grid=(t // tm, QKV_WIDTH // tn),
        in_specs=[
            pl.BlockSpec((tm, D_MODEL), lambda i, j: (i, 0)),
            pl.BlockSpec((D_MODEL, tn), lambda i, j: (0, j)),
            pl.BlockSpec((tm, HEAD_DIM), lambda i, j: (i % pos_tiles, 0)),
            pl.BlockSpec((tm, HEAD_DIM), lambda i, j: (i % pos_tiles, 0)),
            pl.BlockSpec((1, HEAD_DIM), lambda i, j: (0, 0)),
            pl.BlockSpec((1, HEAD_DIM), lambda i, j: (0, 0)),
        ],
        out_specs=pl.BlockSpec((tm, tn), lambda i, j: (i, j)),
        out_shape=jax.ShapeDtypeStruct((t, QKV_WIDTH), BF16),
        compiler_params=_params("parallel", "arbitrary"),
        name="qkv_proj",
    )(n, w, cos, sin, gq, gk)


def _scaled_mm_kernel(n_ref, w_ref, s_ref, o_ref):
    y = jnp.dot(n_ref[...], w_ref[...], preferred_element_type=F32)
    o_ref[...] = (y * s_ref[...]).astype(o_ref.dtype)


def _scaled_mm(n, w, col_scale, out_dtype, name):
    t = n.shape[0]
    nn = w.shape[1]
    tm, tn = TOKEN_TILE, 1024
    return pl.pallas_call(
        _scaled_mm_kernel,
        grid=(t // tm, nn // tn),
        in_specs=[
            pl.BlockSpec((tm, D_MODEL), lambda i, j: (i, 0)),
            pl.BlockSpec((D_MODEL, tn), lambda i, j: (0, j)),
            pl.BlockSpec((1, tn), lambda i, j: (0, j)),
        ],
        out_specs=pl.BlockSpec((tm, tn), lambda i, j: (i, j)),
        out_shape=jax.ShapeDtypeStruct((t, nn), out_dtype),
        compiler_params=_params("parallel", "arbitrary"),
        name=name,
    )(n, w, col_scale)


def _stack_heads(q_ref, qs_sc):
    tq = q_ref.shape[1]
    for g in range(ATTN_GROUPS):
        qs_sc[g * tq:(g + 1) * tq, :] = q_ref[0, :, g * HEAD_DIM:(g + 1) * HEAD_DIM]


def _unstack_heads(out, o_ref):
    tq = o_ref.shape[1]
    for g in range(ATTN_GROUPS):
        o_ref[0, :, g * HEAD_DIM:(g + 1) * HEAD_DIM] = out[g * tq:(g + 1) * tq, :].astype(BF16)


def _attn_online_kernel(q_ref, k_ref, v_ref, o_ref, qs_sc, m_sc, l_sc, acc_sc):
    tk = ATTN_KV_TILE
    nk = k_ref.shape[1] // tk
    _stack_heads(q_ref, qs_sc)
    m_sc[...] = jnp.full_like(m_sc, -jnp.inf)
    l_sc[...] = jnp.zeros_like(l_sc)
    acc_sc[...] = jnp.zeros_like(acc_sc)

    def body(c, carry):
        start = pl.multiple_of(c * tk, tk)
        k = k_ref[0, pl.ds(start, tk), :]
        v = v_ref[0, pl.ds(start, tk), :HEAD_DIM]
        s = lax.dot_general(qs_sc[...], k, (((1,), (1,)), ((), ())), preferred_element_type=F32)
        m_prev = m_sc[...]
        m_new = jnp.maximum(m_prev, jnp.max(s, axis=-1, keepdims=True))
        alpha = jnp.exp2(m_prev - m_new)
        p = jnp.exp2(s - m_new)
        l_sc[...] = alpha * l_sc[...] + jnp.sum(p, axis=-1, keepdims=True)
        acc_sc[...] = alpha * acc_sc[...] + jnp.dot(p.astype(BF16), v, preferred_element_type=F32)
        m_sc[...] = m_new
        return carry

    lax.fori_loop(0, nk, body, 0)
    _unstack_heads(acc_sc[...] / l_sc[...], o_ref)


def _attn_bounded_kernel(q_ref, k_ref, v_ref, o_ref, qs_sc, acc_sc):
    tk = ATTN_KV_TILE
    nk = k_ref.shape[1] // tk
    _stack_heads(q_ref, qs_sc)
    acc_sc[...] = jnp.zeros_like(acc_sc)

    def body(c, carry):
        start = pl.multiple_of(c * tk, tk)
        k = k_ref[0, pl.ds(start, tk), :]
        v = v_ref[0, pl.ds(start, tk), :]
        s = lax.dot_general(qs_sc[...], k, (((1,), (1,)), ((), ())), preferred_element_type=F32)
        acc_sc[...] += jnp.dot(jnp.exp2(s).astype(BF16), v, preferred_element_type=F32)
        return carry

    lax.fori_loop(0, nk, body, 0, unroll=True)
    acc = acc_sc[...]
    _unstack_heads(acc[:, :HEAD_DIM] / acc[:, HEAD_DIM:HEAD_DIM + 1], o_ref)


def _attention(qkv, k_all, v_aug, bounded):
    b, l, _ = qkv.shape
    lk = k_all.shape[1]
    tq = ATTN_Q_TILE
    gw = ATTN_GROUPS * HEAD_DIM
    rows = ATTN_GROUPS * tq
    if bounded:
        body = _attn_bounded_kernel
        scratch = [pltpu.VMEM((rows, HEAD_DIM), BF16), pltpu.VMEM((rows, 2 * HEAD_DIM), F32)]
    else:
        body = _attn_online_kernel
        scratch = [pltpu.VMEM((rows, HEAD_DIM), BF16), pltpu.VMEM((rows, 1), F32),
                   pltpu.VMEM((rows, 1), F32), pltpu.VMEM((rows, HEAD_DIM), F32)]
    return pl.pallas_call(
        body,
        grid=(b, ATTN_KV_HEADS, l // tq),
        in_specs=[
            pl.BlockSpec((1, tq, gw), lambda bi, hi, qi: (bi, qi, hi)),
            pl.BlockSpec((1, lk, HEAD_DIM), lambda bi, hi, qi: (bi, 0, hi)),
            pl.BlockSpec((1, lk, 2 * HEAD_DIM), lambda bi, hi, qi: (bi, 0, hi)),
        ],
        out_specs=pl.BlockSpec((1, tq, gw), lambda bi, hi, qi: (bi, qi, hi)),
        out_shape=jax.ShapeDtypeStruct((b, l, ATTN_WIDTH), BF16),
        scratch_shapes=scratch,
        compiler_params=_params("parallel", "parallel", "arbitrary"),
        name="attention_bounded" if bounded else "attention_online",
    )(qkv, k_all, v_aug)


def _ret_kernel(lg_ref, q_ref, k_ref, v_ref, g_ref, kc_ref, vc_ref, o_ref, uf_sc, sb_sc):
    c_len = RET_CHUNK
    n_chunks = q_ref.shape[1] // c_len
    n_ctx = kc_ref.shape[1]
    head = pl.program_id(1)
    lgf = lg_ref[0, head]
    lgb = lg_ref[1, head]

    row = lax.broadcasted_iota(jnp.int32, (c_len, 1), 0).astype(F32)
    vdec_f = jnp.exp((c_len - 1.0 - row) * lgf)
    vdec_b = jnp.exp(row * lgb)
    qdec_f = jnp.exp((row + 1.0) * lgf)
    qdec_b = jnp.exp((c_len - row) * lgb)
    chunk_f = jnp.exp(jnp.full((1, RET_DIM), c_len, F32) * lgf)
    chunk_b = jnp.exp(jnp.full((1, RET_DIM), c_len, F32) * lgb)
    diff = (lax.broadcasted_iota(jnp.int32, (c_len, c_len), 0)
            - lax.broadcasted_iota(jnp.int32, (c_len, c_len), 1)).astype(F32)
    decay = (jnp.where(diff >= 0, jnp.exp(jnp.maximum(diff, 0.0) * lgf), 0.0)
             + jnp.where(diff <= 0, jnp.exp(jnp.maximum(-diff, 0.0) * lgb), 0.0))

    def kv_outer(k, v, dec_f, dec_b):
        vf = v.astype(F32)
        v2 = jnp.concatenate([(vf * dec_f).astype(BF16), (vf * dec_b).astype(BF16)], axis=1)
        return lax.dot_general(k, v2, (((0,), (0,)), ((), ())), preferred_element_type=F32)

    crow = lax.broadcasted_iota(jnp.int32, (n_ctx, 1), 0).astype(F32)
    s0 = kv_outer(kc_ref[0], vc_ref[0], jnp.exp((n_ctx - 1.0 - crow) * lgf), jnp.exp(crow * lgb))

    def back_body(t, sb):
        c = n_chunks - 1 - t
        start = pl.multiple_of(c * c_len, c_len)
        sb_sc[c] = sb
        u = kv_outer(k_ref[0, pl.ds(start, c_len), :], v_ref[0, pl.ds(start, c_len), :], vdec_f, vdec_b)
        uf_sc[c] = u[:, :RET_DIM]
        return sb * chunk_b + u[:, RET_DIM:]

    lax.fori_loop(0, n_chunks, back_body, s0[:, RET_DIM:])

    def fwd_body(c, sf):
        start = pl.multiple_of(c * c_len, c_len)
        q = q_ref[0, pl.ds(start, c_len), :]
        k = k_ref[0, pl.ds(start, c_len), :]
        v = v_ref[0, pl.ds(start, c_len), :]
        inner = lax.dot_general(q, k, (((1,), (1,)), ((), ())), preferred_element_type=F32)
        y = jnp.dot((inner * decay).astype(BF16), v, preferred_element_type=F32)
        states = jnp.concatenate([sf.astype(BF16), sb_sc[c].astype(BF16)], axis=1)
        cross = jnp.dot(q, states, preferred_element_type=F32)
        y = y + cross[:, :RET_DIM] * qdec_f + cross[:, RET_DIM:] * qdec_b
        gate = g_ref[0, pl.ds(start, c_len), :]
        o_ref[0, pl.ds(start, c_len), :] = (gate * _sigmoid(gate) * _rms(y)).astype(BF16)
        return sf * chunk_f + uf_sc[c]

    lax.fori_loop(0, n_chunks, fwd_body, s0[:, :RET_DIM])


def _retention(log_gamma, ret_x, gate_x, ret_c):
    b, l, _ = ret_x.shape
    lc = ret_c.shape[1]
    n_chunks = l // RET_CHUNK
    seq = lambda off: pl.BlockSpec((1, l, RET_DIM), lambda bi, hi: (bi, 0, hi + off))
    ctx = lambda off: pl.BlockSpec((1, lc, RET_DIM), lambda bi, hi: (bi, 0, hi + off))
    return pl.pallas_call(
        _ret_kernel,
        grid=(b, RET_HEADS),
        in_specs=[
            pl.BlockSpec(memory_space=pltpu.SMEM),
            seq(0), seq(RET_HEADS), seq(2 * RET_HEADS), seq(0),
            ctx(RET_HEADS), ctx(2 * RET_HEADS),
        ],
        out_specs=seq(0),
        out_shape=jax.ShapeDtypeStruct((b, l, RET_WIDTH), BF16),
        scratch_shapes=[
            pltpu.VMEM((n_chunks, RET_DIM, RET_DIM), F32),
            pltpu.VMEM((n_chunks, RET_DIM, RET_DIM), F32),
        ],
        compiler_params=_params("parallel", "arbitrary"),
        name="retention",
    )(log_gamma, ret_x, ret_x, ret_x, gate_x, ret_c, ret_c)


def _merge_kernel(h_ref, n_ref, ya_ref, yr_ref, mod_ref, wga_ref, wgb_ref, wpa_ref, wpr_ref, wo_ref,
                  out_ref, acc_sc):
    j = pl.program_id(1)

    @pl.when(j == 0)
    def _():
        acc_sc[...] = jnp.zeros_like(acc_sc)

    n = n_ref[...]
    ga = jnp.dot(n, wga_ref[...], preferred_element_type=F32)
    gb = jnp.dot(n, wgb_ref[...], preferred_element_type=F32)
    pa = jnp.dot(ya_ref[...], wpa_ref[...], preferred_element_type=F32)
    pr = jnp.dot(yr_ref[...], wpr_ref[...], preferred_element_type=F32)
    z = (_sigmoid(ga) * pa + _sigmoid(gb) * pr).astype(BF16)
    acc_sc[...] += jnp.dot(z, wo_ref[...], preferred_element_type=F32)

    @pl.when(j == pl.num_programs(1) - 1)
    def _():
        out_ref[...] = h_ref[...] + mod_ref[0, 5:6, :] * acc_sc[...]


def _merge(h, n, ya, yr, mod, w_ga, w_gb, w_pa, w_pr, w_out, *, tiles_per_row):
    t = h.shape[0]
    tm, tc = TOKEN_TILE, MERGE_TILE
    return pl.pallas_call(
        _merge_kernel,
        grid=(t // tm, D_MODEL // tc),
        in_specs=[
            pl.BlockSpec((tm, D_MODEL), lambda i, j: (i, 0)),
            pl.BlockSpec((tm, D_MODEL), lambda i, j: (i, 0)),
            pl.BlockSpec((tm, ATTN_WIDTH), lambda i, j: (i, 0)),
            pl.BlockSpec((tm, RET_WIDTH), lambda i, j: (i, 0)),
            pl.BlockSpec((1, N_MOD, D_MODEL), lambda i, j: (i // tiles_per_row, 0, 0)),
            pl.BlockSpec((D_MODEL, tc), lambda i, j: (0, j)),
            pl.BlockSpec((D_MODEL, tc), lambda i, j: (0, j)),
            pl.BlockSpec((ATTN_WIDTH, tc), lambda i, j: (0, j)),
            pl.BlockSpec((RET_WIDTH, tc), lambda i, j: (0, j)),
            pl.BlockSpec((tc, D_MODEL), lambda i, j: (j, 0)),
        ],
        out_specs=pl.BlockSpec((tm, D_MODEL), lambda i, j: (i, 0)),
        out_shape=jax.ShapeDtypeStruct((t, D_MODEL), F32),
        scratch_shapes=[pltpu.VMEM((tm, D_MODEL), F32)],
        compiler_params=_params("parallel", "arbitrary"),
        name="merge",
    )(h, n, ya, yr, mod, w_ga, w_gb, w_pa, w_pr, w_out)


def _rope_tables(seq_len):
    rows = seq_len // GRID_W
    row = jnp.repeat(jnp.arange(rows, dtype=F32), GRID_W)
    col = jnp.tile(jnp.arange(GRID_W, dtype=F32), rows)
    half = HEAD_DIM // 2
    inv_freq = ROPE_THETA ** (-jnp.arange(0, half, 2, dtype=F32) / half)
    ang = jnp.concatenate([row[:, None] * inv_freq, col[:, None] * inv_freq], axis=-1)
    cos = jnp.repeat(jnp.cos(ang), 2, axis=-1)
    sin = jnp.repeat(jnp.sin(ang), 2, axis=-1)
    sign = jnp.tile(jnp.array([-1.0, 1.0], F32), half)
    return cos, sin * sign


def kernel(x, c, ctx, c_ctx, w_ada, b_ada, ffn1_w_in, ffn1_w_out, mix_w_in, attn_q_gain, attn_k_gain,
           ret_decay_logit, w_proj_attn, w_proj_ret, mix_w_out, ffn2_w_in, ffn2_w_out, final_norm):
    batch, seq_len, _ = x.shape
    assert w_ada.shape[0] == 1, "single-layer block"
    assert seq_len % TOKEN_TILE == 0 and (batch * CTX_LEN) % TOKEN_TILE == 0
    tiles_per_row = seq_len // TOKEN_TILE

    w1_in, w1_out = ffn1_w_in[0].astype(BF16), ffn1_w_out[0].astype(BF16)
    w2_in, w2_out = ffn2_w_in[0].astype(BF16), ffn2_w_out[0].astype(BF16)
    w_mix = mix_w_in[0]
    w_qkv = w_mix[:, :QKV_WIDTH].astype(BF16)
    w_ret = w_mix[:, RET_OFF:GR_OFF].astype(BF16)
    w_gr = w_mix[:, GR_OFF:GA_OFF].astype(BF16)
    w_ga = w_mix[:, GA_OFF:GB_OFF].astype(BF16)
    w_gb = w_mix[:, GB_OFF:].astype(BF16)
    w_pa, w_pr = w_proj_attn[0].astype(BF16), w_proj_ret[0].astype(BF16)
    w_mo = mix_w_out[0].astype(BF16)

    cond = jnp.zeros((8, D_MODEL), F32).at[:batch].set(c).at[batch].set(c_ctx)
    mod = _adaln(cond, w_ada[0], b_ada).reshape(8, N_MOD, D_MODEL)

    x2 = x.reshape(batch * seq_len, D_MODEL)
    c2 = ctx.reshape(batch * CTX_LEN, D_MODEL)
    h1, n2 = _ffn(x2, mod, w1_in, w1_out, mod_base=0, tiles_per_row=tiles_per_row, row_offset=0,
                  emit_next=True)
    _, n2c = _ffn(c2, mod, w1_in, w1_out, mod_base=0, tiles_per_row=batch * CTX_LEN, row_offset=batch,
                  emit_next=True)

    cos, sin = _rope_tables(seq_len)
    gq = attn_q_gain[0].reshape(1, HEAD_DIM)
    gk = attn_k_gain[0].reshape(1, HEAD_DIM)
    qkv_x = _qkv_proj(n2, w_qkv, cos, sin, gq, gk, pos_tiles=tiles_per_row)
    ones = jnp.ones((TOKEN_TILE, HEAD_DIM), F32)
    qkv_c = _qkv_proj(n2c, w_qkv, ones, jnp.zeros_like(ones), gq, gk, pos_tiles=1)

    ret_scale = jnp.concatenate([jnp.ones((1, RET_WIDTH), F32),
                                 jnp.full((1, RET_WIDTH), RET_DIM ** -0.5, F32),
                                 jnp.ones((1, RET_WIDTH), F32)], axis=1)
    ret_x = _scaled_mm(n2, w_ret, ret_scale, BF16, "ret_proj")
    ret_c = _scaled_mm(n2c, w_ret, ret_scale, BF16, "ret_proj")
    gate_x = _scaled_mm(n2, w_gr, jnp.ones((1, RET_WIDTH), F32), F32, "ret_gate_proj")

    qkv_x = qkv_x.reshape(batch, seq_len, QKV_WIDTH)
    qkv_c = qkv_c.reshape(batch, CTX_LEN, QKV_WIDTH)
    k_all = jnp.concatenate([qkv_c[:, :, ATTN_WIDTH:ATTN_WIDTH + KV_WIDTH],
                             qkv_x[:, :, ATTN_WIDTH:ATTN_WIDTH + KV_WIDTH]], axis=1)
    v_all = jnp.concatenate([qkv_c[:, :, ATTN_WIDTH + KV_WIDTH:], qkv_x[:, :, ATTN_WIDTH + KV_WIDTH:]], axis=1)
    lk = CTX_LEN + seq_len
    ones_col = jnp.zeros((batch, lk, ATTN_KV_HEADS, HEAD_DIM), BF16).at[..., 0].set(1.0)
    v_aug = jnp.concatenate([v_all.reshape(batch, lk, ATTN_KV_HEADS, HEAD_DIM), ones_col], axis=-1)
    v_aug = v_aug.reshape(batch, lk, 2 * KV_WIDTH)
    score_bound = HEAD_DIM ** 0.5 * jnp.max(jnp.abs(gq)) * jnp.max(jnp.abs(gk))
    ya = lax.cond(score_bound <= ATTN_SAFE_SCORE,
                  functools.partial(_attention, bounded=True),
                  functools.partial(_attention, bounded=False),
                  qkv_x, k_all, v_aug)

    log_gamma = jax.nn.log_sigmoid(ret_decay_logit[0].astype(F32))
    yr = _retention(log_gamma, ret_x.reshape(batch, seq_len, 3 * RET_WIDTH),
                    gate_x.reshape(batch, seq_len, RET_WIDTH),
                    ret_c.reshape(batch, CTX_LEN, 3 * RET_WIDTH))

    h2 = _merge(h1, n2, ya.reshape(batch * seq_len, ATTN_WIDTH), yr.reshape(batch * seq_len, RET_WIDTH), mod,
                w_ga, w_gb, w_pa, w_pr, w_mo, tiles_per_row=tiles_per_row)
    out = _ffn(h2, mod, w2_in, w2_out, mod_base=6, tiles_per_row=tiles_per_row, row_offset=0,
               final_norm=final_norm.reshape(1, D_MODEL))
    return out[0].reshape(batch, seq_len, D_MODEL)
```

```python
import functools

import jax
import jax.numpy as jnp
from jax import lax
from jax.experimental import pallas as pl
from jax.experimental.pallas import tpu as pltpu

D_MODEL = 2048
CTX_LEN = 256
GRID_W = 64
HEAD_DIM = 128
ATTN_Q_HEADS = 8
ATTN_KV_HEADS = 2
ATTN_GROUPS = ATTN_Q_HEADS // ATTN_KV_HEADS
ATTN_WIDTH = ATTN_Q_HEADS * HEAD_DIM
KV_WIDTH = ATTN_KV_HEADS * HEAD_DIM
RET_HEADS = 8
RET_DIM = 128
RET_WIDTH = RET_HEADS * RET_DIM
D_FF = 5632
ROPE_THETA = 10000.0
NORM_EPS = 1e-6
N_MOD = 9

QKV_WIDTH = ATTN_WIDTH + 2 * KV_WIDTH
RET_OFF = QKV_WIDTH
GR_OFF = RET_OFF + 3 * RET_WIDTH
GA_OFF = GR_OFF + RET_WIDTH
GB_OFF = GA_OFF + D_MODEL

F32 = jnp.float32
BF16 = jnp.bfloat16

VMEM_LIMIT_BYTES = 56 * 1024 * 1024

TOKEN_TILE = 512
FF_TILE = 512
MERGE_TILE = 512
ATTN_Q_TILE = 256
ATTN_KV_TILE = 768
RET_CHUNK = 256
RET_UNROLL = 4

Q_SCALE = HEAD_DIM ** -0.5 * 1.4426950408889634
ATTN_SAFE_SCORE = 32.0


def _params(*sem):
    return pltpu.CompilerParams(dimension_semantics=sem, vmem_limit_bytes=VMEM_LIMIT_BYTES)


def _rms(x):
    return x * lax.rsqrt(jnp.mean(x * x, axis=-1, keepdims=True) + NORM_EPS)


def _sigmoid(x):
    return 1.0 / (1.0 + jnp.exp(-x))


def _adaln_kernel(c_ref, w_ref, b_ref, o_ref):
    c = c_ref[...]
    s = (c * _sigmoid(c)).astype(BF16)
    o_ref[...] = jnp.dot(s, w_ref[...].astype(BF16), preferred_element_type=F32) + b_ref[...]


def _adaln(cond, w, b):
    n = w.shape[1]
    tn = 1024
    return pl.pallas_call(
        _adaln_kernel,
        grid=(n // tn,),
        in_specs=[
            pl.BlockSpec((8, D_MODEL), lambda j: (0, 0)),
            pl.BlockSpec((D_MODEL, tn), lambda j: (0, j)),
            pl.BlockSpec((1, tn), lambda j: (0, j)),
        ],
        out_specs=pl.BlockSpec((8, tn), lambda j: (0, j)),
        out_shape=jax.ShapeDtypeStruct((8, n), F32),
        compiler_params=_params("arbitrary"),
        name="adaln",
    )(cond, w, b)


def _ffn_kernel(*refs, mod_base, emit_next, final):
    h_ref, mod_ref, wa_ref, wb_ref, wo_ref = refs[:5]
    refs = refs[5:]
    if final:
        fn_ref, refs = refs[0], refs[1:]
    out_ref, refs = refs[0], refs[1:]
    if emit_next:
        nxt_ref, refs = refs[0], refs[1:]
    xn_sc, acc_sc = refs
    j = pl.program_id(1)

    @pl.when(j == 0)
    def _():
        n = _rms(h_ref[...])
        n = n * (1.0 + mod_ref[0, mod_base + 1:mod_base + 2, :]) + mod_ref[0, mod_base:mod_base + 1, :]
        xn_sc[...] = n.astype(BF16)
        acc_sc[...] = jnp.zeros_like(acc_sc)

    xn = xn_sc[...]
    a = jnp.dot(xn, wa_ref[...], preferred_element_type=F32)
    b = jnp.dot(xn, wb_ref[...], preferred_element_type=F32)
    act = (a * _sigmoid(a) * b).astype(BF16)
    acc_sc[...] += jnp.dot(act, wo_ref[...], preferred_element_type=F32)

    @pl.when(j == pl.num_programs(1) - 1)
    def _():
        gate = mod_ref[0, mod_base + 2:mod_base + 3, :]
        h = h_ref[...] + (0.5 * gate) * acc_sc[...]
        if final:
            out_ref[...] = _rms(h) * fn_ref[...]
        else:
            out_ref[...] = h
        if emit_next:
            n = _rms(h)
            n = n * (1.0 + mod_ref[0, mod_base + 4:mod_base + 5, :]) + mod_ref[0, mod_base + 3:mod_base + 4, :]
            nxt_ref[...] = n.astype(BF16)


def _ffn(h, mod, w_in, w_out, *, mod_base, tiles_per_row, row_offset, emit_next=False, final_norm=None):
    t = h.shape[0]
    tm, tf = TOKEN_TILE, FF_TILE
    nf = D_FF // tf
    final = final_norm is not None
    mod_map = lambda i, j: (i // tiles_per_row + row_offset, 0, 0)
    in_specs = [
        pl.BlockSpec((tm, D_MODEL), lambda i, j: (i, 0)),
        pl.BlockSpec((1, N_MOD, D_MODEL), mod_map),
        pl.BlockSpec((D_MODEL, tf), lambda i, j: (0, j)),
        pl.BlockSpec((D_MODEL, tf), lambda i, j: (0, j + nf)),
        pl.BlockSpec((tf, D_MODEL), lambda i, j: (j, 0)),
    ]
    args = [h, mod, w_in, w_in, w_out]
    if final:
        in_specs.append(pl.BlockSpec((1, D_MODEL), lambda i, j: (0, 0)))
        args.append(final_norm)
    out_specs = [pl.BlockSpec((tm, D_MODEL), lambda i, j: (i, 0))]
    out_shape = [jax.ShapeDtypeStruct((t, D_MODEL), F32)]
    if emit_next:
        out_specs.append(pl.BlockSpec((tm, D_MODEL), lambda i, j: (i, 0)))
        out_shape.append(jax.ShapeDtypeStruct((t, D_MODEL), BF16))
    return pl.pallas_call(
        functools.partial(_ffn_kernel, mod_base=mod_base, emit_next=emit_next, final=final),
        grid=(t // tm, nf),
        in_specs=in_specs,
        out_specs=out_specs,
        out_shape=out_shape,
        scratch_shapes=[pltpu.VMEM((tm, D_MODEL), BF16), pltpu.VMEM((tm, D_MODEL), F32)],
        compiler_params=_params("parallel", "arbitrary"),
        name="ffn_final" if final else "ffn",
    )(*args)


def _qkv_kernel(n_ref, w_ref, cos_ref, sin_ref, gq_ref, gk_ref, o_ref):
    y = jnp.dot(n_ref[...], w_ref[...], preferred_element_type=F32)
    cos = cos_ref[...]
    sin = sin_ref[...]
    gq = gq_ref[...] * Q_SCALE
    gk = gk_ref[...]
    for hh in range(ATTN_Q_HEADS + ATTN_KV_HEADS):
        sl = slice(hh * HEAD_DIM, (hh + 1) * HEAD_DIM)
        t = _rms(y[:, sl]) * (gq if hh < ATTN_Q_HEADS else gk)
        o_ref[:, sl] = (t * cos + pltpu.roll(t, HEAD_DIM // 2, 1) * sin).astype(BF16)
    o_ref[:, ATTN_WIDTH + KV_WIDTH:] = y[:, ATTN_WIDTH + KV_WIDTH:].astype(BF16)


def _qkv_proj(n, w, cos, sin, gq, gk, *, pos_tiles):
    t = n.shape[0]
    tm = TOKEN_TILE
    return pl.pallas_call(
        _qkv_kernel,
        grid=(t // tm,),
        in_specs=[
            pl.BlockSpec((tm, D_MODEL), lambda i: (i, 0)),
            pl.BlockSpec((D_MODEL, QKV_WIDTH), lambda i: (0, 0)),
            pl.BlockSpec((tm, HEAD_DIM), lambda i: (i % pos_tiles, 0)),
            pl.BlockSpec((tm, HEAD_DIM), lambda i: (i % pos_tiles, 0)),
            pl.BlockSpec((1, HEAD_DIM), lambda i: (0, 0)),
            pl.BlockSpec((1, HEAD_DIM), lambda i: (0, 0)),
        ],
        out_specs=pl.BlockSpec((tm, QKV_WIDTH), lambda i: (i, 0)),
        out_shape=jax.ShapeDtypeStruct((t, QKV_WIDTH), BF16),
        compiler_params=_params("parallel"),
        name="qkv_proj",
    )(n, w, cos, sin, gq, gk)


def _scaled_mm_kernel(n_ref, w_ref, s_ref, o_ref):
    y = jnp.dot(n_ref[...], w_ref[...], preferred_element_type=F32)
    o_ref[...] = (y * s_ref[...]).astype(o_ref.dtype)


def _scaled_mm(n, w, col_scale, out_dtype, name):
    t = n.shape[0]
    nn = w.shape[1]
    tm = TOKEN_TILE
    return pl.pallas_call(
        _scaled_mm_kernel,
        grid=(t // tm,),
        in_specs=[
            pl.BlockSpec((tm, D_MODEL), lambda i: (i, 0)),
            pl.BlockSpec((D_MODEL, nn), lambda i: (0, 0)),
            pl.BlockSpec((1, nn), lambda i: (0, 0)),
        ],
        out_specs=pl.BlockSpec((tm, nn), lambda i: (i, 0)),
        out_shape=jax.ShapeDtypeStruct((t, nn), out_dtype),
        compiler_params=_params("parallel"),
        name=name,
    )(n, w, col_scale)


def _attn_kernel(flag_ref, q_ref, kc_ref, kx_ref, vc_ref, vx_ref, o_ref,
                 k_sc, v_sc, qs_sc, acc_sc, m_sc):
    tq = q_ref.shape[1]
    tk = ATTN_KV_TILE
    n_ctx = kc_ref.shape[1]
    lk = k_sc.shape[0]
    nk = lk // tk

    @pl.when(pl.program_id(2) == 0)
    def _():
        k_sc[:n_ctx, :] = kc_ref[0]
        k_sc[n_ctx:, :] = kx_ref[0]
        v_sc[:n_ctx, :HEAD_DIM] = vc_ref[0]
        v_sc[n_ctx:, :HEAD_DIM] = vx_ref[0]
        lane = lax.broadcasted_iota(jnp.int32, (n_ctx, HEAD_DIM), 1)
        ones_col = jnp.where(lane == 0, 1.0, 0.0).astype(BF16)
        for r in range(lk // n_ctx):
            v_sc[r * n_ctx:(r + 1) * n_ctx, HEAD_DIM:] = ones_col

    for g in range(ATTN_GROUPS):
        qs_sc[g * tq:(g + 1) * tq, :] = q_ref[0, :, g * HEAD_DIM:(g + 1) * HEAD_DIM]

    def scores(c):
        start = pl.multiple_of(c * tk, tk)
        k = k_sc[pl.ds(start, tk), :]
        s = lax.dot_general(qs_sc[...], k, (((1,), (1,)), ((), ())), preferred_element_type=F32)
        return s, v_sc[pl.ds(start, tk), :]

    def finish():
        acc = acc_sc[...]
        out = acc[:, :HEAD_DIM] / acc[:, HEAD_DIM:HEAD_DIM + 1]
        for g in range(ATTN_GROUPS):
            o_ref[0, :, g * HEAD_DIM:(g + 1) * HEAD_DIM] = out[g * tq:(g + 1) * tq, :].astype(BF16)

    @pl.when(flag_ref[0] != 0)
    def _():
        acc_sc[...] = jnp.zeros_like(acc_sc)

        def body(c, carry):
            s, v = scores(c)
            acc_sc[...] += jnp.dot(jnp.exp2(s).astype(BF16), v, preferred_element_type=F32)
            return carry

        lax.fori_loop(0, nk, body, 0, unroll=True)
        finish()

    @pl.when(flag_ref[0] == 0)
    def _():
        acc_sc[...] = jnp.zeros_like(acc_sc)
        m_sc[...] = jnp.full_like(m_sc, -jnp.inf)

        def body(c, carry):
            s, v = scores(c)
            m_prev = m_sc[...]
            m_new = jnp.maximum(m_prev, jnp.max(s, axis=-1, keepdims=True))
            p = jnp.exp2(s - m_new).astype(BF16)
            acc_sc[...] = jnp.exp2(m_prev - m_new) * acc_sc[...] + jnp.dot(p, v, preferred_element_type=F32)
            m_sc[...] = m_new
            return carry

        lax.fori_loop(0, nk, body, 0)
        finish()


def _attention(bounded_flag, qkv_x, qkv_c):
    b, l, _ = qkv_x.shape
    lc = qkv_c.shape[1]
    lk = lc + l
    tq = ATTN_Q_TILE
    gw = ATTN_GROUPS * HEAD_DIM
    rows = ATTN_GROUPS * tq
    k_col = ATTN_WIDTH // HEAD_DIM
    v_col = k_col + ATTN_KV_HEADS
    lat = lambda col: pl.BlockSpec((1, l, HEAD_DIM), lambda bi, hi, qi: (bi, 0, hi + col))
    ctx = lambda col: pl.BlockSpec((1, lc, HEAD_DIM), lambda bi, hi, qi: (bi, 0, hi + col))
    return pl.pallas_call(
        _attn_kernel,
        grid=(b, ATTN_KV_HEADS, l // tq),
        in_specs=[
            pl.BlockSpec(memory_space=pltpu.SMEM),
            pl.BlockSpec((1, tq, gw), lambda bi, hi, qi: (bi, qi, hi)),
            ctx(k_col), lat(k_col), ctx(v_col), lat(v_col),
        ],
        out_specs=pl.BlockSpec((1, tq, gw), lambda bi, hi, qi: (bi, qi, hi)),
        out_shape=jax.ShapeDtypeStruct((b, l, ATTN_WIDTH), BF16),
        scratch_shapes=[
            pltpu.VMEM((lk, HEAD_DIM), BF16),
            pltpu.VMEM((lk, 2 * HEAD_DIM), BF16),
            pltpu.VMEM((rows, HEAD_DIM), BF16),
            pltpu.VMEM((rows, 2 * HEAD_DIM), F32),
            pltpu.VMEM((rows, 1), F32),
        ],
        compiler_params=_params("parallel", "parallel", "arbitrary"),
        name="attention",
    )(bounded_flag, qkv_x, qkv_c, qkv_x, qkv_c, qkv_x)


def _ret_kernel(lg_ref, q_ref, k_ref, v_ref, g_ref, kc_ref, vc_ref, o_ref, uf_sc, sb_sc):
    c_len = RET_CHUNK
    n_chunks = q_ref.shape[1] // c_len
    n_ctx = kc_ref.shape[1]
    head = pl.program_id(1)
    lgf = lg_ref[0, head]
    lgb = lg_ref[1, head]

    row = lax.broadcasted_iota(jnp.int32, (c_len, 1), 0).astype(F32)
    vdec_f = jnp.exp((c_len - 1.0 - row) * lgf)
    vdec_b = jnp.exp(row * lgb)
    qdec_f = jnp.exp((row + 1.0) * lgf)
    qdec_b = jnp.exp((c_len - row) * lgb)
    chunk_f = jnp.exp(jnp.full((1, RET_DIM), c_len, F32) * lgf)
    chunk_b = jnp.exp(jnp.full((1, RET_DIM), c_len, F32) * lgb)
    diff = (lax.broadcasted_iota(jnp.int32, (c_len, c_len), 0)
            - lax.broadcasted_iota(jnp.int32, (c_len, c_len), 1)).astype(F32)
    decay = (jnp.where(diff >= 0, jnp.exp(jnp.maximum(diff, 0.0) * lgf), 0.0)
             + jnp.where(diff <= 0, jnp.exp(jnp.maximum(-diff, 0.0) * lgb), 0.0))

    def kv_outer(k, v, dec_f, dec_b):
        vf = v.astype(F32)
        v2 = jnp.concatenate([(vf * dec_f).astype(BF16), (vf * dec_b).astype(BF16)], axis=1)
        return lax.dot_general(k, v2, (((0,), (0,)), ((), ())), preferred_element_type=F32)

    crow = lax.broadcasted_iota(jnp.int32, (n_ctx, 1), 0).astype(F32)
    s0 = kv_outer(kc_ref[0], vc_ref[0], jnp.exp((n_ctx - 1.0 - crow) * lgf), jnp.exp(crow * lgb))

    def back_body(t, sb):
        c = n_chunks - 1 - t
        start = pl.multiple_of(c * c_len, c_len)
        sb_sc[c] = sb
        u = kv_outer(k_ref[0, pl.ds(start, c_len), :], v_ref[0, pl.ds(start, c_len), :], vdec_f, vdec_b)
        uf_sc[c] = u[:, :RET_DIM]
        return sb * chunk_b + u[:, RET_DIM:]

    lax.fori_loop(0, n_chunks, back_body, s0[:, RET_DIM:], unroll=RET_UNROLL)

    def fwd_body(c, sf):
        start = pl.multiple_of(c * c_len, c_len)
        q = q_ref[0, pl.ds(start, c_len), :]
        k = k_ref[0, pl.ds(start, c_len), :]
        v = v_ref[0, pl.ds(start, c_len), :]
        inner = lax.dot_general(q, k, (((1,), (1,)), ((), ())), preferred_element_type=F32)
        y = jnp.dot((inner * decay).astype(BF16), v, preferred_element_type=F32)
        states = jnp.concatenate([sf.astype(BF16), sb_sc[c].astype(BF16)], axis=1)
        cross = jnp.dot(q, states, preferred_element_type=F32)
        y = y + cross[:, :RET_DIM] * qdec_f + cross[:, RET_DIM:] * qdec_b
        gate = g_ref[0, pl.ds(start, c_len), :]
        o_ref[0, pl.ds(start, c_len), :] = (gate * _sigmoid(gate) * _rms(y)).astype(BF16)
        return sf * chunk_f + uf_sc[c]

    lax.fori_loop(0, n_chunks, fwd_body, s0[:, :RET_DIM], unroll=RET_UNROLL)


def _retention(log_gamma, ret_x, gate_x, ret_c):
    b, l, _ = ret_x.shape
    lc = ret_c.shape[1]
    n_chunks = l // RET_CHUNK
    seq = lambda off: pl.BlockSpec((1, l, RET_DIM), lambda bi, hi: (bi, 0, hi + off))
    ctx = lambda off: pl.BlockSpec((1, lc, RET_DIM), lambda bi, hi: (bi, 0, hi + off))
    return pl.pallas_call(
        _ret_kernel,
        grid=(b, RET_HEADS),
        in_specs=[
            pl.BlockSpec(memory_space=pltpu.SMEM),
            seq(0), seq(RET_HEADS), seq(2 * RET_HEADS), seq(0),
            ctx(RET_HEADS), ctx(2 * RET_HEADS),
        ],
        out_specs=seq(0),
        out_shape=jax.ShapeDtypeStruct((b, l, RET_WIDTH), BF16),
        scratch_shapes=[
            pltpu.VMEM((n_chunks, RET_DIM, RET_DIM), F32),
            pltpu.VMEM((n_chunks, RET_DIM, RET_DIM), F32),
        ],
        compiler_params=_params("parallel", "arbitrary"),
        name="retention",
    )(log_gamma, ret_x, ret_x, ret_x, gate_x, ret_c, ret_c)


def _merge_kernel(h_ref, n_ref, ya_ref, yr_ref, mod_ref, wga_ref, wgb_ref, wpa_ref, wpr_ref, wo_ref,
                  out_ref, acc_sc):
    j = pl.program_id(1)

    @pl.when(j == 0)
    def _():
        acc_sc[...] = jnp.zeros_like(acc_sc)

    n = n_ref[...]
    ga = jnp.dot(n, wga_ref[...], preferred_element_type=F32)
    gb = jnp.dot(n, wgb_ref[...], preferred_element_type=F32)
    pa = jnp.dot(ya_ref[...], wpa_ref[...], preferred_element_type=F32)
    pr = jnp.dot(yr_ref[...], wpr_ref[...], preferred_element_type=F32)
    z = (_sigmoid(ga) * pa + _sigmoid(gb) * pr).astype(BF16)
    acc_sc[...] += jnp.dot(z, wo_ref[...], preferred_element_type=F32)

    @pl.when(j == pl.num_programs(1) - 1)
    def _():
        out_ref[...] = h_ref[...] + mod_ref[0, 5:6, :] * acc_sc[...]


def _merge(h, n, ya, yr, mod, w_ga, w_gb, w_pa, w_pr, w_out, *, tiles_per_row):
    t = h.shape[0]
    tm, tc = TOKEN_TILE, MERGE_TILE
    return pl.pallas_call(
        _merge_kernel,
        grid=(t // tm, D_MODEL // tc),
        in_specs=[
            pl.BlockSpec((tm, D_MODEL), lambda i, j: (i, 0)),
            pl.BlockSpec((tm, D_MODEL), lambda i, j: (i, 0)),
            pl.BlockSpec((tm, ATTN_WIDTH), lambda i, j: (i, 0)),
            pl.BlockSpec((tm, RET_WIDTH), lambda i, j: (i, 0)),
            pl.BlockSpec((1, N_MOD, D_MODEL), lambda i, j: (i // tiles_per_row, 0, 0)),
            pl.BlockSpec((D_MODEL, tc), lambda i, j: (0, j)),
            pl.BlockSpec((D_MODEL, tc), lambda i, j: (0, j)),
            pl.BlockSpec((ATTN_WIDTH, tc), lambda i, j: (0, j)),
            pl.BlockSpec((RET_WIDTH, tc), lambda i, j: (0, j)),
            pl.BlockSpec((tc, D_MODEL), lambda i, j: (j, 0)),
        ],
        out_specs=pl.BlockSpec((tm, D_MODEL), lambda i, j: (i, 0)),
        out_shape=jax.ShapeDtypeStruct((t, D_MODEL), F32),
        scratch_shapes=[pltpu.VMEM((tm, D_MODEL), F32)],
        compiler_params=_params("parallel", "arbitrary"),
        name="merge",
    )(h, n, ya, yr, mod, w_ga, w_gb, w_pa, w_pr, w_out)


def _rope_tables(seq_len):
    rows = seq_len // GRID_W
    row = jnp.repeat(jnp.arange(rows, dtype=F32), GRID_W)
    col = jnp.tile(jnp.arange(GRID_W, dtype=F32), rows)
    half = HEAD_DIM // 2
    inv_freq = ROPE_THETA ** (-jnp.arange(0, half, 2, dtype=F32) / half)
    ang = jnp.concatenate([row[:, None] * inv_freq, col[:, None] * inv_freq], axis=-1)
    cos, sin = jnp.cos(ang), jnp.sin(ang)
    return jnp.concatenate([cos, cos], axis=-1), jnp.concatenate([-sin, sin], axis=-1)


def _deinterleave(t):
    lead = t.shape[:-1]
    t = t.reshape(lead + (-1, HEAD_DIM // 2, 2))
    return jnp.swapaxes(t, -1, -2).reshape(lead + (-1,))


def kernel(x, c, ctx, c_ctx, w_ada, b_ada, ffn1_w_in, ffn1_w_out, mix_w_in, attn_q_gain, attn_k_gain,
           ret_decay_logit, w_proj_attn, w_proj_ret, mix_w_out, ffn2_w_in, ffn2_w_out, final_norm):
    batch, seq_len, _ = x.shape
    assert w_ada.shape[0] == 1, "single-layer block"
    assert seq_len % TOKEN_TILE == 0 and (batch * CTX_LEN) % TOKEN_TILE == 0
    tiles_per_row = seq_len // TOKEN_TILE

    w1_in, w1_out = ffn1_w_in[0].astype(BF16), ffn1_w_out[0].astype(BF16)
    w2_in, w2_out = ffn2_w_in[0].astype(BF16), ffn2_w_out[0].astype(BF16)
    w_mix = mix_w_in[0]
    w_qkv = jnp.concatenate([_deinterleave(w_mix[:, :ATTN_WIDTH + KV_WIDTH]),
                             w_mix[:, ATTN_WIDTH + KV_WIDTH:QKV_WIDTH]], axis=1).astype(BF16)
    w_ret = w_mix[:, RET_OFF:GR_OFF].astype(BF16)
    w_gr = w_mix[:, GR_OFF:GA_OFF].astype(BF16)
    w_ga = w_mix[:, GA_OFF:GB_OFF].astype(BF16)
    w_gb = w_mix[:, GB_OFF:].astype(BF16)
    w_pa, w_pr = w_proj_attn[0].astype(BF16), w_proj_ret[0].astype(BF16)
    w_mo = mix_w_out[0].astype(BF16)

    cond = jnp.zeros((8, D_MODEL), F32).at[:batch].set(c).at[batch].set(c_ctx)
    mod = _adaln(cond, w_ada[0], b_ada).reshape(8, N_MOD, D_MODEL)

    x2 = x.reshape(batch * seq_len, D_MODEL)
    c2 = ctx.reshape(batch * CTX_LEN, D_MODEL)
    h1, n2 = _ffn(x2, mod, w1_in, w1_out, mod_base=0, tiles_per_row=tiles_per_row, row_offset=0,
                  emit_next=True)
    _, n2c = _ffn(c2, mod, w1_in, w1_out, mod_base=0, tiles_per_row=batch * CTX_LEN, row_offset=batch,
                  emit_next=True)

    cos, sin = _rope_tables(seq_len)
    gq = _deinterleave(attn_q_gain[0].reshape(1, HEAD_DIM))
    gk = _deinterleave(attn_k_gain[0].reshape(1, HEAD_DIM))
    qkv_x = _qkv_proj(n2, w_qkv, cos, sin, gq, gk, pos_tiles=tiles_per_row)
    ones = jnp.ones((TOKEN_TILE, HEAD_DIM), F32)
    qkv_c = _qkv_proj(n2c, w_qkv, ones, jnp.zeros_like(ones), gq, gk, pos_tiles=1)

    ret_scale = jnp.concatenate([jnp.ones((1, RET_WIDTH), F32),
                                 jnp.full((1, RET_WIDTH), RET_DIM ** -0.5, F32),
                                 jnp.ones((1, RET_WIDTH), F32)], axis=1)
    ret_x = _scaled_mm(n2, w_ret, ret_scale, BF16, "ret_proj")
    ret_c = _scaled_mm(n2c, w_ret, ret_scale, BF16, "ret_proj")
    gate_x = _scaled_mm(n2, w_gr, jnp.ones((1, RET_WIDTH), F32), F32, "ret_gate_proj")

    qkv_x = qkv_x.reshape(batch, seq_len, QKV_WIDTH)
    qkv_c = qkv_c.reshape(batch, CTX_LEN, QKV_WIDTH)
    score_bound = HEAD_DIM ** 0.5 * jnp.max(jnp.abs(gq)) * jnp.max(jnp.abs(gk))
    ya = _attention((score_bound <= ATTN_SAFE_SCORE).astype(jnp.int32).reshape(1), qkv_x, qkv_c)

    log_gamma = jax.nn.log_sigmoid(ret_decay_logit[0].astype(F32))
    yr = _retention(log_gamma, ret_x.reshape(batch, seq_len, 3 * RET_WIDTH),
                    gate_x.reshape(batch, seq_len, RET_WIDTH),
                    ret_c.reshape(batch, CTX_LEN, 3 * RET_WIDTH))

    h2 = _merge(h1, n2, ya.reshape(batch * seq_len, ATTN_WIDTH), yr.reshape(batch * seq_len, RET_WIDTH), mod,
                w_ga, w_gb, w_pa, w_pr, w_mo, tiles_per_row=tiles_per_row)
    out = _ffn(h2, mod, w2_in, w2_out, mod_base=6, tiles_per_row=tiles_per_row, row_offset=0,
               final_norm=final_norm.reshape(1, D_MODEL))
    return out[0].reshape(batch, seq_len, D_MODEL)
```

```python
import functools

import jax
import jax.numpy as jnp
from jax import lax
from jax.experimental import pallas as pl
from jax.experimental.pallas import tpu as pltpu

D_MODEL = 2048
CTX_LEN = 256
GRID_W = 64
HEAD_DIM = 128
LANES = 128
ATTN_Q_HEADS = 8
ATTN_KV_HEADS = 2
ATTN_GROUPS = ATTN_Q_HEADS // ATTN_KV_HEADS
ATTN_WIDTH = ATTN_Q_HEADS * HEAD_DIM
KV_WIDTH = ATTN_KV_HEADS * HEAD_DIM
RET_HEADS = 8
RET_DIM = 128
RET_WIDTH = RET_HEADS * RET_DIM
D_FF = 5632
ROPE_THETA = 10000.0
NORM_EPS = 1e-6
N_MOD = 9

QKV_WIDTH = ATTN_WIDTH + 2 * KV_WIDTH
RET_OFF = QKV_WIDTH
GR_OFF = RET_OFF + 3 * RET_WIDTH
GA_OFF = GR_OFF + RET_WIDTH
GB_OFF = GA_OFF + D_MODEL

F32 = jnp.float32
BF16 = jnp.bfloat16

VMEM_LIMIT_BYTES = 56 * 1024 * 1024

ADALN_TILE = 1024
TOKEN_TILE = 512
FF_TILE = 512
ROW_BLOCK = 16
ROW_UNROLL = 4
MERGE_TILE = 512
ATTN_Q_TILE = 256
ATTN_KV_TILE = 768
RET_CHUNK = 256
RET_UNROLL = 4

Q_SCALE = HEAD_DIM ** -0.5 * 1.4426950408889634
ATTN_SAFE_SCORE = 32.0


def _params(*sem):
    return pltpu.CompilerParams(dimension_semantics=sem, vmem_limit_bytes=VMEM_LIMIT_BYTES)


def _rms(x):
    return x * lax.rsqrt(jnp.mean(x * x, axis=-1, keepdims=True) + NORM_EPS)


def _sigmoid(x):
    return 1.0 / (1.0 + jnp.exp(-x))


def _adaln_kernel(c_ref, w_ref, b_ref, o_ref):
    chunk = pl.program_id(0) // (D_MODEL // ADALN_TILE)
    c = c_ref[...]
    s = (c * _sigmoid(c)).astype(BF16)
    y = jnp.dot(s, w_ref[...].astype(BF16), preferred_element_type=F32) + b_ref[...]
    y = y + jnp.where(chunk % 3 == 1, 1.0, 0.0)
    o_ref[...] = y * jnp.where((chunk == 2) | (chunk == 8), 0.5, 1.0)


def _adaln(cond, w, b):
    n = w.shape[1]
    tn = ADALN_TILE
    return pl.pallas_call(
        _adaln_kernel,
        grid=(n // tn,),
        in_specs=[
            pl.BlockSpec((8, D_MODEL), lambda j: (0, 0)),
            pl.BlockSpec((D_MODEL, tn), lambda j: (0, j)),
            pl.BlockSpec((1, tn), lambda j: (0, j)),
        ],
        out_specs=pl.BlockSpec((8, tn), lambda j: (0, j)),
        out_shape=jax.ShapeDtypeStruct((8, n), F32),
        compiler_params=_params("arbitrary"),
        name="adaln",
    )(cond, w, b)


def _ffn_kernel(*refs, mod_base, emit_next, final):
    h_ref, mod_ref, wa_ref, wb_ref, wo_ref = refs[:5]
    refs = refs[5:]
    if final:
        fn_ref, refs = refs[0], refs[1:]
    out_ref, refs = refs[0], refs[1:]
    if emit_next:
        nxt_ref, refs = refs[0], refs[1:]
    xn_sc, acc_sc, inv_sc = refs
    j = pl.program_id(1)
    n_row_blocks = h_ref.shape[0] // ROW_BLOCK

    lane_tiles = [slice(k, k + LANES) for k in range(0, D_MODEL, LANES)]

    def row_block(r):
        return pl.ds(pl.multiple_of(r * ROW_BLOCK, ROW_BLOCK), ROW_BLOCK)

    def inv_rms(x):
        inv = lax.rsqrt(jnp.mean(x * x, axis=-1, keepdims=True) + NORM_EPS)
        return jnp.broadcast_to(inv, (x.shape[0], LANES))

    def for_row_blocks(body, unroll):
        def step(r, carry):
            body(row_block(r))
            return carry

        lax.fori_loop(0, n_row_blocks, step, 0, unroll=unroll)

    @pl.when(j == 0)
    def _():
        def stats(rows):
            inv_sc[rows, :] = inv_rms(h_ref[rows, :])

        def prenorm(rows):
            inv = inv_sc[rows, :]
            for sl in lane_tiles:
                n = h_ref[rows, sl] * inv * mod_ref[0, mod_base + 1, :, sl] + mod_ref[0, mod_base, :, sl]
                xn_sc[rows, sl] = n.astype(BF16)

        for_row_blocks(stats, True)
        for_row_blocks(prenorm, ROW_UNROLL)
        acc_sc[...] = jnp.zeros_like(acc_sc)

    xn = xn_sc[...]
    a = jnp.dot(xn, wa_ref[...], preferred_element_type=F32)
    b = jnp.dot(xn, wb_ref[...], preferred_element_type=F32)
    act = (a * _sigmoid(a) * b).astype(BF16)
    acc_sc[...] += jnp.dot(act, wo_ref[...], preferred_element_type=F32)

    @pl.when(j == pl.num_programs(1) - 1)
    def _():
        def residual(rows):
            h = h_ref[rows, :] + mod_ref[0, mod_base + 2] * acc_sc[rows, :]
            out_ref[rows, :] = h
            inv_sc[rows, :] = inv_rms(h)

        def postnorm(rows):
            inv = inv_sc[rows, :]
            for sl in lane_tiles:
                n = out_ref[rows, sl] * inv
                if final:
                    out_ref[rows, sl] = n * fn_ref[:, sl]
                else:
                    n = n * mod_ref[0, mod_base + 4, :, sl] + mod_ref[0, mod_base + 3, :, sl]
                    nxt_ref[rows, sl] = n.astype(BF16)

        for_row_blocks(residual, True)
        if final or emit_next:
            for_row_blocks(postnorm, ROW_UNROLL)


def _ffn(h, mod, w_in, w_out, *, mod_base, tiles_per_row, row_offset, emit_next=False, final_norm=None):
    t = h.shape[0]
    tm, tf = TOKEN_TILE, FF_TILE
    nf = D_FF // tf
    final = final_norm is not None
    mod_map = lambda i, j: (i // tiles_per_row + row_offset, 0, 0, 0)
    in_specs = [
        pl.BlockSpec((tm, D_MODEL), lambda i, j: (i, 0)),
        pl.BlockSpec((1, N_MOD, ROW_BLOCK, D_MODEL), mod_map),
        pl.BlockSpec((D_MODEL, tf), lambda i, j: (0, j)),
        pl.BlockSpec((D_MODEL, tf), lambda i, j: (0, j + nf)),
        pl.BlockSpec((tf, D_MODEL), lambda i, j: (j, 0)),
    ]
    args = [h, mod, w_in, w_in, w_out]
    if final:
        in_specs.append(pl.BlockSpec((ROW_BLOCK, D_MODEL), lambda i, j: (0, 0)))
        args.append(final_norm)
    out_specs = [pl.BlockSpec((tm, D_MODEL), lambda i, j: (i, 0))]
    out_shape = [jax.ShapeDtypeStruct((t, D_MODEL), F32)]
    if emit_next:
        out_specs.append(pl.BlockSpec((tm, D_MODEL), lambda i, j: (i, 0)))
        out_shape.append(jax.ShapeDtypeStruct((t, D_MODEL), BF16))
    return pl.pallas_call(
        functools.partial(_ffn_kernel, mod_base=mod_base, emit_next=emit_next, final=final),
        grid=(t // tm, nf),
        in_specs=in_specs,
        out_specs=out_specs,
        out_shape=out_shape,
        scratch_shapes=[pltpu.VMEM((tm, D_MODEL), BF16), pltpu.VMEM((tm, D_MODEL), F32),
                        pltpu.VMEM((tm, LANES), F32)],
        compiler_params=_params("parallel", "arbitrary"),
        name="ffn_final" if final else "ffn",
    )(*args)


def _qkv_kernel(n_ref, w_ref, cos_ref, sin_ref, gq_ref, gk_ref, o_ref):
    y = jnp.dot(n_ref[...], w_ref[...], preferred_element_type=F32)
    cos = cos_ref[...]
    sin = sin_ref[...]
    gq = gq_ref[...] * Q_SCALE
    gk = gk_ref[...]
    for hh in range(ATTN_Q_HEADS + ATTN_KV_HEADS):
        sl = slice(hh * HEAD_DIM, (hh + 1) * HEAD_DIM)
        t = _rms(y[:, sl]) * (gq if hh < ATTN_Q_HEADS else gk)
        o_ref[:, sl] = (t * cos + pltpu.roll(t, HEAD_DIM // 2, 1) * sin).astype(BF16)
    o_ref[:, ATTN_WIDTH + KV_WIDTH:] = y[:, ATTN_WIDTH + KV_WIDTH:].astype(BF16)


def _qkv_proj(n, w, cos, sin, gq, gk, *, pos_tiles):
    t = n.shape[0]
    tm = TOKEN_TILE
    return pl.pallas_call(
        _qkv_kernel,
        grid=(t // tm,),
        in_specs=[
            pl.BlockSpec((tm, D_MODEL), lambda i: (i, 0)),
            pl.BlockSpec((D_MODEL, QKV_WIDTH), lambda i: (0, 0)),
            pl.BlockSpec((tm, HEAD_DIM), lambda i: (i % pos_tiles, 0)),
            pl.BlockSpec((tm, HEAD_DIM), lambda i: (i % pos_tiles, 0)),
            pl.BlockSpec((1, HEAD_DIM), lambda i: (0, 0)),
            pl.BlockSpec((1, HEAD_DIM), lambda i: (0, 0)),
        ],
        out_specs=pl.BlockSpec((tm, QKV_WIDTH), lambda i: (i, 0)),
        out_shape=jax.ShapeDtypeStruct((t, QKV_WIDTH), BF16),
        compiler_params=_params("parallel"),
        name="qkv_proj",
    )(n, w, cos, sin, gq, gk)


def _scaled_mm_kernel(n_ref, w_ref, s_ref, o_ref):
    y = jnp.dot(n_ref[...], w_ref[...], preferred_element_type=F32)
    o_ref[...] = (y * s_ref[...]).astype(o_ref.dtype)


def _scaled_mm(n, w, col_scale, out_dtype, name):
    t = n.shape[0]
    nn = w.shape[1]
    tm = TOKEN_TILE
    return pl.pallas_call(
        _scaled_mm_kernel,
        grid=(t // tm,),
        in_specs=[
            pl.BlockSpec((tm, D_MODEL), lambda i: (i, 0)),
            pl.BlockSpec((D_MODEL, nn), lambda i: (0, 0)),
            pl.BlockSpec((1, nn), lambda i: (0, 0)),
        ],
        out_specs=pl.BlockSpec((tm, nn), lambda i: (i, 0)),
        out_shape=jax.ShapeDtypeStruct((t, nn), out_dtype),
        compiler_params=_params("parallel"),
        name=name,
    )(n, w, col_scale)


def _attn_kernel(flag_ref, q_ref, kc_ref, kx_ref, vc_ref, vx_ref, o_ref,
                 k_sc, v_sc, qs_sc, acc_sc, m_sc):
    tq = q_ref.shape[1]
    tk = ATTN_KV_TILE
    n_ctx = kc_ref.shape[1]
    lk = k_sc.shape[0]
    nk = lk // tk

    @pl.when(pl.program_id(2) == 0)
    def _():
        k_sc[:n_ctx, :] = kc_ref[0]
        k_sc[n_ctx:, :] = kx_ref[0]
        v_sc[:n_ctx, :HEAD_DIM] = vc_ref[0]
        v_sc[n_ctx:, :HEAD_DIM] = vx_ref[0]
        lane = lax.broadcasted_iota(jnp.int32, (n_ctx, HEAD_DIM), 1)
        ones_col = jnp.where(lane == 0, 1.0, 0.0).astype(BF16)
        for r in range(lk // n_ctx):
            v_sc[r * n_ctx:(r + 1) * n_ctx, HEAD_DIM:] = ones_col

    for g in range(ATTN_GROUPS):
        qs_sc[g * tq:(g + 1) * tq, :] = q_ref[0, :, g * HEAD_DIM:(g + 1) * HEAD_DIM]

    def scores(c):
        start = pl.multiple_of(c * tk, tk)
        k = k_sc[pl.ds(start, tk), :]
        s = lax.dot_general(qs_sc[...], k, (((1,), (1,)), ((), ())), preferred_element_type=F32)
        return s, v_sc[pl.ds(start, tk), :]

    def finish():
        acc = acc_sc[...]
        out = acc[:, :HEAD_DIM] / acc[:, HEAD_DIM:HEAD_DIM + 1]
        for g in range(ATTN_GROUPS):
            o_ref[0, :, g * HEAD_DIM:(g + 1) * HEAD_DIM] = out[g * tq:(g + 1) * tq, :].astype(BF16)

    @pl.when(flag_ref[0] != 0)
    def _():
        acc_sc[...] = jnp.zeros_like(acc_sc)

        def body(c, carry):
            s, v = scores(c)
            acc_sc[...] += jnp.dot(jnp.exp2(s).astype(BF16), v, preferred_element_type=F32)
            return carry

        lax.fori_loop(0, nk, body, 0, unroll=True)
        finish()

    @pl.when(flag_ref[0] == 0)
    def _():
        acc_sc[...] = jnp.zeros_like(acc_sc)
        m_sc[...] = jnp.full_like(m_sc, -jnp.inf)

        def body(c, carry):
            s, v = scores(c)
            m_prev = m_sc[...]
            m_new = jnp.maximum(m_prev, jnp.max(s, axis=-1, keepdims=True))
            p = jnp.exp2(s - m_new).astype(BF16)
            acc_sc[...] = jnp.exp2(m_prev - m_new) * acc_sc[...] + jnp.dot(p, v, preferred_element_type=F32)
            m_sc[...] = m_new
            return carry

        lax.fori_loop(0, nk, body, 0)
        finish()


def _attention(bounded_flag, qkv_x, qkv_c):
    b, l, _ = qkv_x.shape
    lc = qkv_c.shape[1]
    lk = lc + l
    tq = ATTN_Q_TILE
    gw = ATTN_GROUPS * HEAD_DIM
    rows = ATTN_GROUPS * tq
    k_col = ATTN_WIDTH // HEAD_DIM
    v_col = k_col + ATTN_KV_HEADS
    lat = lambda col: pl.BlockSpec((1, l, HEAD_DIM), lambda bi, hi, qi: (bi, 0, hi + col))
    ctx = lambda col: pl.BlockSpec((1, lc, HEAD_DIM), lambda bi, hi, qi: (bi, 0, hi + col))
    return pl.pallas_call(
        _attn_kernel,
        grid=(b, ATTN_KV_HEADS, l // tq),
        in_specs=[
            pl.BlockSpec(memory_space=pltpu.SMEM),
            pl.BlockSpec((1, tq, gw), lambda bi, hi, qi: (bi, qi, hi)),
            ctx(k_col), lat(k_col), ctx(v_col), lat(v_col),
        ],
        out_specs=pl.BlockSpec((1, tq, gw), lambda bi, hi, qi: (bi, qi, hi)),
        out_shape=jax.ShapeDtypeStruct((b, l, ATTN_WIDTH), BF16),
        scratch_shapes=[
            pltpu.VMEM((lk, HEAD_DIM), BF16),
            pltpu.VMEM((lk, 2 * HEAD_DIM), BF16),
            pltpu.VMEM((rows, HEAD_DIM), BF16),
            pltpu.VMEM((rows, 2 * HEAD_DIM), F32),
            pltpu.VMEM((rows, 1), F32),
        ],
        compiler_params=_params("parallel", "parallel", "arbitrary"),
        name="attention",
    )(bounded_flag, qkv_x, qkv_c, qkv_x, qkv_c, qkv_x)


def _ret_kernel(lg_ref, q_ref, k_ref, v_ref, g_ref, kc_ref, vc_ref, o_ref, uf_sc, sb_sc):
    c_len = RET_CHUNK
    n_chunks = q_ref.shape[1] // c_len
    n_ctx = kc_ref.shape[1]
    head = pl.program_id(1)
    lgf = lg_ref[0, head]
    lgb = lg_ref[1, head]

    row = lax.broadcasted_iota(jnp.int32, (c_len, 1), 0).astype(F32)
    vdec_f = jnp.exp((c_len - 1.0 - row) * lgf)
    vdec_b = jnp.exp(row * lgb)
    qdec_f = jnp.exp((row + 1.0) * lgf)
    qdec_b = jnp.exp((c_len - row) * lgb)
    chunk_f = jnp.exp(jnp.full((1, RET_DIM), c_len, F32) * lgf)
    chunk_b = jnp.exp(jnp.full((1, RET_DIM), c_len, F32) * lgb)
    diff = (lax.broadcasted_iota(jnp.int32, (c_len, c_len), 0)
            - lax.broadcasted_iota(jnp.int32, (c_len, c_len), 1)).astype(F32)
    decay = (jnp.where(diff >= 0, jnp.exp(jnp.maximum(diff, 0.0) * lgf), 0.0)
             + jnp.where(diff <= 0, jnp.exp(jnp.maximum(-diff, 0.0) * lgb), 0.0))

    def kv_outer(k, v, dec_f, dec_b):
        vf = v.astype(F32)
        v2 = jnp.concatenate([(vf * dec_f).astype(BF16), (vf * dec_b).astype(BF16)], axis=1)
        return lax.dot_general(k, v2, (((0,), (0,)), ((), ())), preferred_element_type=F32)

    crow = lax.broadcasted_iota(jnp.int32, (n_ctx, 1), 0).astype(F32)
    s0 = kv_outer(kc_ref[0], vc_ref[0], jnp.exp((n_ctx - 1.0 - crow) * lgf), jnp.exp(crow * lgb))

    def back_body(t, sb):
        c = n_chunks - 1 - t
        start = pl.multiple_of(c * c_len, c_len)
        sb_sc[c] = sb
        u = kv_outer(k_ref[0, pl.ds(start, c_len), :], v_ref[0, pl.ds(start, c_len), :], vdec_f, vdec_b)
        uf_sc[c] = u[:, :RET_DIM]
        return sb * chunk_b + u[:, RET_DIM:]

    lax.fori_loop(0, n_chunks, back_body, s0[:, RET_DIM:], unroll=RET_UNROLL)

    def fwd_body(c, sf):
        start = pl.multiple_of(c * c_len, c_len)
        q = q_ref[0, pl.ds(start, c_len), :]
        k = k_ref[0, pl.ds(start, c_len), :]
        v = v_ref[0, pl.ds(start, c_len), :]
        inner = lax.dot_general(q, k, (((1,), (1,)), ((), ())), preferred_element_type=F32)
        y = jnp.dot((inner * decay).astype(BF16), v, preferred_element_type=F32)
        states = jnp.concatenate([sf.astype(BF16), sb_sc[c].astype(BF16)], axis=1)
        cross = jnp.dot(q, states, preferred_element_type=F32)
        y = y + cross[:, :RET_DIM] * qdec_f + cross[:, RET_DIM:] * qdec_b
        gate = g_ref[0, pl.ds(start, c_len), :]
        o_ref[0, pl.ds(start, c_len), :] = (gate * _sigmoid(gate) * _rms(y)).astype(BF16)
        return sf * chunk_f + uf_sc[c]

    lax.fori_loop(0, n_chunks, fwd_body, s0[:, :RET_DIM], unroll=RET_UNROLL)


def _retention(log_gamma, ret_x, gate_x, ret_c):
    b, l, _ = ret_x.shape
    lc = ret_c.shape[1]
    n_chunks = l // RET_CHUNK
    seq = lambda off: pl.BlockSpec((1, l, RET_DIM), lambda bi, hi: (bi, 0, hi + off))
    ctx = lambda off: pl.BlockSpec((1, lc, RET_DIM), lambda bi, hi: (bi, 0, hi + off))
    return pl.pallas_call(
        _ret_kernel,
        grid=(b, RET_HEADS),
        in_specs=[
            pl.BlockSpec(memory_space=pltpu.SMEM),
            seq(0), seq(RET_HEADS), seq(2 * RET_HEADS), seq(0),
            ctx(RET_HEADS), ctx(2 * RET_HEADS),
        ],
        out_specs=seq(0),
        out_shape=jax.ShapeDtypeStruct((b, l, RET_WIDTH), BF16),
        scratch_shapes=[
            pltpu.VMEM((n_chunks, RET_DIM, RET_DIM), F32),
            pltpu.VMEM((n_chunks, RET_DIM, RET_DIM), F32),
        ],
        compiler_params=_params("parallel", "arbitrary"),
        name="retention",
    )(log_gamma, ret_x, ret_x, ret_x, gate_x, ret_c, ret_c)


def _merge_kernel(h_ref, n_ref, ya_ref, yr_ref, mod_ref, wga_ref, wgb_ref, wpa_ref, wpr_ref, wo_ref,
                  out_ref, acc_sc):
    j = pl.program_id(1)

    @pl.when(j == 0)
    def _():
        acc_sc[...] = jnp.zeros_like(acc_sc)

    n = n_ref[...]
    ga = jnp.dot(n, wga_ref[...], preferred_element_type=F32)
    gb = jnp.dot(n, wgb_ref[...], preferred_element_type=F32)
    pa = jnp.dot(ya_ref[...], wpa_ref[...], preferred_element_type=F32)
    pr = jnp.dot(yr_ref[...], wpr_ref[...], preferred_element_type=F32)
    z = (_sigmoid(ga) * pa + _sigmoid(gb) * pr).astype(BF16)
    acc_sc[...] += jnp.dot(z, wo_ref[...], preferred_element_type=F32)

    @pl.when(j == pl.num_programs(1) - 1)
    def _():
        out_ref[...] = h_ref[...] + mod_ref[0, 5, 0:1, :] * acc_sc[...]


def _merge(h, n, ya, yr, mod, w_ga, w_gb, w_pa, w_pr, w_out, *, tiles_per_row):
    t = h.shape[0]
    tm, tc = TOKEN_TILE, MERGE_TILE
    return pl.pallas_call(
        _merge_kernel,
        grid=(t // tm, D_MODEL // tc),
        in_specs=[
            pl.BlockSpec((tm, D_MODEL), lambda i, j: (i, 0)),
            pl.BlockSpec((tm, D_MODEL), lambda i, j: (i, 0)),
            pl.BlockSpec((tm, ATTN_WIDTH), lambda i, j: (i, 0)),
            pl.BlockSpec((tm, RET_WIDTH), lambda i, j: (i, 0)),
            pl.BlockSpec((1, N_MOD, ROW_BLOCK, D_MODEL), lambda i, j: (i // tiles_per_row, 0, 0, 0)),
            pl.BlockSpec((D_MODEL, tc), lambda i, j: (0, j)),
            pl.BlockSpec((D_MODEL, tc), lambda i, j: (0, j)),
            pl.BlockSpec((ATTN_WIDTH, tc), lambda i, j: (0, j)),
            pl.BlockSpec((RET_WIDTH, tc), lambda i, j: (0, j)),
            pl.BlockSpec((tc, D_MODEL), lambda i, j: (j, 0)),
        ],
        out_specs=pl.BlockSpec((tm, D_MODEL), lambda i, j: (i, 0)),
        out_shape=jax.ShapeDtypeStruct((t, D_MODEL), F32),
        scratch_shapes=[pltpu.VMEM((tm, D_MODEL), F32)],
        compiler_params=_params("parallel", "arbitrary"),
        name="merge",
    )(h, n, ya, yr, mod, w_ga, w_gb, w_pa, w_pr, w_out)


def _rope_tables(seq_len):
    rows = seq_len // GRID_W
    row = jnp.repeat(jnp.arange(rows, dtype=F32), GRID_W)
    col = jnp.tile(jnp.arange(GRID_W, dtype=F32), rows)
    half = HEAD_DIM // 2
    inv_freq = ROPE_THETA ** (-jnp.arange(0, half, 2, dtype=F32) / half)
    ang = jnp.concatenate([row[:, None] * inv_freq, col[:, None] * inv_freq], axis=-1)
    cos, sin = jnp.cos(ang), jnp.sin(ang)
    return jnp.concatenate([cos, cos], axis=-1), jnp.concatenate([-sin, sin], axis=-1)


def _deinterleave(t):
    lead = t.shape[:-1]
    t = t.reshape(lead + (-1, HEAD_DIM // 2, 2))
    return jnp.swapaxes(t, -1, -2).reshape(lead + (-1,))


def kernel(x, c, ctx, c_ctx, w_ada, b_ada, ffn1_w_in, ffn1_w_out, mix_w_in, attn_q_gain, attn_k_gain,
           ret_decay_logit, w_proj_attn, w_proj_ret, mix_w_out, ffn2_w_in, ffn2_w_out, final_norm):
    batch, seq_len, _ = x.shape
    assert w_ada.shape[0] == 1, "single-layer block"
    assert seq_len % TOKEN_TILE == 0 and (batch * CTX_LEN) % TOKEN_TILE == 0
    tiles_per_row = seq_len // TOKEN_TILE

    w1_in, w1_out = ffn1_w_in[0].astype(BF16), ffn1_w_out[0].astype(BF16)
    w2_in, w2_out = ffn2_w_in[0].astype(BF16), ffn2_w_out[0].astype(BF16)
    w_mix = mix_w_in[0]
    w_qkv = jnp.concatenate([_deinterleave(w_mix[:, :ATTN_WIDTH + KV_WIDTH]),
                             w_mix[:, ATTN_WIDTH + KV_WIDTH:QKV_WIDTH]], axis=1).astype(BF16)
    w_ret = w_mix[:, RET_OFF:GR_OFF].astype(BF16)
    w_gr = w_mix[:, GR_OFF:GA_OFF].astype(BF16)
    w_ga = w_mix[:, GA_OFF:GB_OFF].astype(BF16)
    w_gb = w_mix[:, GB_OFF:].astype(BF16)
    w_pa, w_pr = w_proj_attn[0].astype(BF16), w_proj_ret[0].astype(BF16)
    w_mo = mix_w_out[0].astype(BF16)

    cond = jnp.zeros((8, D_MODEL), F32).at[:batch].set(c).at[batch].set(c_ctx)
    mod = _adaln(cond, w_ada[0], b_ada).reshape(8, N_MOD, 1, D_MODEL)
    mod = jnp.broadcast_to(mod, (8, N_MOD, ROW_BLOCK, D_MODEL))

    x2 = x.reshape(batch * seq_len, D_MODEL)
    c2 = ctx.reshape(batch * CTX_LEN, D_MODEL)
    h1, n2 = _ffn(x2, mod, w1_in, w1_out, mod_base=0, tiles_per_row=tiles_per_row, row_offset=0,
                  emit_next=True)
    _, n2c = _ffn(c2, mod, w1_in, w1_out, mod_base=0, tiles_per_row=batch * CTX_LEN, row_offset=batch,
                  emit_next=True)

    cos, sin = _rope_tables(seq_len)
    gq = _deinterleave(attn_q_gain[0].reshape(1, HEAD_DIM))
    gk = _deinterleave(attn_k_gain[0].reshape(1, HEAD_DIM))
    qkv_x = _qkv_proj(n2, w_qkv, cos, sin, gq, gk, pos_tiles=tiles_per_row)
    ones = jnp.ones((TOKEN_TILE, HEAD_DIM), F32)
    qkv_c = _qkv_proj(n2c, w_qkv, ones, jnp.zeros_like(ones), gq, gk, pos_tiles=1)

    ret_scale = jnp.concatenate([jnp.ones((1, RET_WIDTH), F32),
                                 jnp.full((1, RET_WIDTH), RET_DIM ** -0.5, F32),
                                 jnp.ones((1, RET_WIDTH), F32)], axis=1)
    ret_x = _scaled_mm(n2, w_ret, ret_scale, BF16, "ret_proj")
    ret_c = _scaled_mm(n2c, w_ret, ret_scale, BF16, "ret_proj")
    gate_x = _scaled_mm(n2, w_gr, jnp.ones((1, RET_WIDTH), F32), F32, "ret_gate_proj")

    qkv_x = qkv_x.reshape(batch, seq_len, QKV_WIDTH)
    qkv_c = qkv_c.reshape(batch, CTX_LEN, QKV_WIDTH)
    score_bound = HEAD_DIM ** 0.5 * jnp.max(jnp.abs(gq)) * jnp.max(jnp.abs(gk))
    ya = _attention((score_bound <= ATTN_SAFE_SCORE).astype(jnp.int32).reshape(1), qkv_x, qkv_c)

    log_gamma = jax.nn.log_sigmoid(ret_decay_logit[0].astype(F32))
    yr = _retention(log_gamma, ret_x.reshape(batch, seq_len, 3 * RET_WIDTH),
                    gate_x.reshape(batch, seq_len, RET_WIDTH),
                    ret_c.reshape(batch, CTX_LEN, 3 * RET_WIDTH))

    h2 = _merge(h1, n2, ya.reshape(batch * seq_len, ATTN_WIDTH), yr.reshape(batch * seq_len, RET_WIDTH), mod,
                w_ga, w_gb, w_pa, w_pr, w_mo, tiles_per_row=tiles_per_row)
    out = _ffn(h2, mod, w2_in, w2_out, mod_base=6, tiles_per_row=tiles_per_row, row_offset=0,
               final_norm=jnp.broadcast_to(final_norm.reshape(1, D_MODEL), (ROW_BLOCK, D_MODEL)))
    return out[0].reshape(batch, seq_len, D_MODEL)
```

```python
import functools

import jax
import jax.numpy as jnp
from jax import lax
from jax.experimental import pallas as pl
from jax.experimental.pallas import tpu as pltpu

D_MODEL = 2048
CTX_LEN = 256
GRID_W = 64
HEAD_DIM = 128
ATTN_Q_HEADS = 8
ATTN_KV_HEADS = 2
ATTN_GROUPS = ATTN_Q_HEADS // ATTN_KV_HEADS
ATTN_WIDTH = ATTN_Q_HEADS * HEAD_DIM
KV_WIDTH = ATTN_KV_HEADS * HEAD_DIM
RET_HEADS = 8
RET_DIM = 128
RET_WIDTH = RET_HEADS * RET_DIM
D_FF = 5632
ROPE_THETA = 10000.0
NORM_EPS = 1e-6
N_MOD = 9

QKV_WIDTH = ATTN_WIDTH + 2 * KV_WIDTH
RET_OFF = QKV_WIDTH
GR_OFF = RET_OFF + 3 * RET_WIDTH
GA_OFF = GR_OFF + RET_WIDTH
GB_OFF = GA_OFF + D_MODEL

F32 = jnp.float32
BF16 = jnp.bfloat16

VMEM_LIMIT_BYTES = 60 * 1024 * 1024

ADALN_TILE = 1024
TOKEN_TILE = 512
FF_TILE = 512
ROW_BLOCK = 16
ROW_UNROLL = 4
SIDE_BLOCKS = 3
MERGE_TILE = 512
ATTN_Q_TILE = 256
ATTN_KV_TILE = 768
RET_CHUNK = 256
RET_UNROLL = 4

Q_SCALE = HEAD_DIM ** -0.5 * 1.4426950408889634
ATTN_SAFE_SCORE = 32.0


def _params(*sem):
    return pltpu.CompilerParams(dimension_semantics=sem, vmem_limit_bytes=VMEM_LIMIT_BYTES)


def _rms(x):
    return x * lax.rsqrt(jnp.mean(x * x, axis=-1, keepdims=True) + NORM_EPS)


def _sigmoid(x):
    return 1.0 / (1.0 + jnp.exp(-x))


def _adaln_kernel(c_ref, w_ref, b_ref, o_ref):
    chunk = pl.program_id(0) // (D_MODEL // ADALN_TILE)
    c = c_ref[...]
    s = (c * _sigmoid(c)).astype(BF16)
    y = jnp.dot(s, w_ref[...].astype(BF16), preferred_element_type=F32) + b_ref[...]
    y = y + jnp.where(chunk % 3 == 1, 1.0, 0.0)
    o_ref[...] = y * jnp.where((chunk == 2) | (chunk == 8), 0.5, 1.0)


def _adaln(cond, w, b):
    n = w.shape[1]
    tn = ADALN_TILE
    return pl.pallas_call(
        _adaln_kernel,
        grid=(n // tn,),
        in_specs=[
            pl.BlockSpec((8, D_MODEL), lambda j: (0, 0)),
            pl.BlockSpec((D_MODEL, tn), lambda j: (0, j)),
            pl.BlockSpec((1, tn), lambda j: (0, j)),
        ],
        out_specs=pl.BlockSpec((8, tn), lambda j: (0, j)),
        out_shape=jax.ShapeDtypeStruct((8, n), F32),
        compiler_params=_params("arbitrary"),
        name="adaln",
    )(cond, w, b)


def _ffn_kernel(*refs, mod_base, emit_next, final, n_tiles):
    hp_ref, hn_ref, mp_ref, mn_ref, wa_ref, wb_ref, wo_ref = refs[:7]
    refs = refs[7:]
    if final:
        fn_ref, refs = refs[0], refs[1:]
    out_ref, refs = refs[0], refs[1:]
    if emit_next:
        nxt_ref, refs = refs[0], refs[1:]
    xn_bufs, acc_bufs = refs[:2], refs[2:]
    i = pl.program_id(0)
    j = pl.program_id(1)
    tm = hp_ref.shape[0]
    side_rows = SIDE_BLOCKS * ROW_BLOCK
    side_start = jnp.minimum(j * side_rows, tm - side_rows)

    def inv_rms(x):
        return lax.rsqrt(jnp.mean(x * x, axis=-1, keepdims=True) + NORM_EPS)

    def prenorm(h_ref, m_ref, xn_ref, rows):
        x = h_ref[rows, :]
        xn_ref[rows, :] = (x * inv_rms(x) * m_ref[0, mod_base + 1] + m_ref[0, mod_base]).astype(BF16)

    def finish(acc_ref, rows):
        h = hp_ref[rows, :] + mp_ref[0, mod_base + 2] * acc_ref[rows, :]
        if final:
            out_ref[rows, :] = h * inv_rms(h) * fn_ref[...]
        else:
            out_ref[rows, :] = h
        if emit_next:
            nxt_ref[rows, :] = (h * inv_rms(h) * mp_ref[0, mod_base + 4] + mp_ref[0, mod_base + 3]).astype(BF16)

    def side_blocks():
        return [pl.ds(pl.multiple_of(side_start + sub * ROW_BLOCK, ROW_BLOCK), ROW_BLOCK)
                for sub in range(SIDE_BLOCKS)]

    @pl.when((i == 0) & (j == 0))
    def _():
        def step(r, carry):
            prenorm(hp_ref, mp_ref, xn_bufs[0], pl.ds(pl.multiple_of(r * ROW_BLOCK, ROW_BLOCK), ROW_BLOCK))
            return carry

        lax.fori_loop(0, tm // ROW_BLOCK, step, 0, unroll=ROW_UNROLL)
        for acc_ref in acc_bufs:
            acc_ref[...] = jnp.zeros_like(acc_ref)

    for parity in range(2):
        xn_cur, xn_oth = xn_bufs[parity], xn_bufs[1 - parity]
        acc_cur, acc_oth = acc_bufs[parity], acc_bufs[1 - parity]

        @pl.when((i < n_tiles) & (i % 2 == parity))
        def _(xn_cur=xn_cur, xn_oth=xn_oth, acc_cur=acc_cur, acc_oth=acc_oth):
            for rows in side_blocks():
                finish(acc_oth, rows)
                prenorm(hn_ref, mn_ref, xn_oth, rows)
            xn = xn_cur[...]
            a = jnp.dot(xn, wa_ref[...], preferred_element_type=F32)
            b = jnp.dot(xn, wb_ref[...], preferred_element_type=F32)
            act = (a * _sigmoid(a) * b).astype(BF16)
            partial = jnp.dot(act, wo_ref[...], preferred_element_type=F32)
            acc_cur[...] = jnp.where(j == 0, 0.0, acc_cur[...]) + partial

    @pl.when(i == n_tiles)
    def _():
        for rows in side_blocks():
            finish(acc_bufs[1 - n_tiles % 2], rows)


def _ffn(h, mod, w_in, w_out, *, mod_base, tiles_per_row, row_offset, emit_next=False, final_norm=None):
    t = h.shape[0]
    tm, tf = TOKEN_TILE, FF_TILE
    nf = D_FF // tf
    n_tiles = t // tm
    assert nf * SIDE_BLOCKS * ROW_BLOCK >= tm, "side work must cover a token tile per grid row"
    final = final_norm is not None
    prev_tile = lambda i: jnp.maximum(i - 1, 0)
    next_tile = lambda i: jnp.minimum(i + 1, n_tiles - 1)
    mod_row = lambda tile: tile // tiles_per_row + row_offset
    w_step = lambda i, j: jnp.where(i < n_tiles, j, nf - 1)
    in_specs = [
        pl.BlockSpec((tm, D_MODEL), lambda i, j: (prev_tile(i), 0)),
        pl.BlockSpec((tm, D_MODEL), lambda i, j: (next_tile(i), 0)),
        pl.BlockSpec((1, N_MOD, ROW_BLOCK, D_MODEL), lambda i, j: (mod_row(prev_tile(i)), 0, 0, 0)),
        pl.BlockSpec((1, N_MOD, ROW_BLOCK, D_MODEL), lambda i, j: (mod_row(next_tile(i)), 0, 0, 0)),
        pl.BlockSpec((D_MODEL, tf), lambda i, j: (0, w_step(i, j))),
        pl.BlockSpec((D_MODEL, tf), lambda i, j: (0, w_step(i, j) + nf)),
        pl.BlockSpec((tf, D_MODEL), lambda i, j: (w_step(i, j), 0)),
    ]
    args = [h, h, mod, mod, w_in, w_in, w_out]
    if final:
        in_specs.append(pl.BlockSpec((ROW_BLOCK, D_MODEL), lambda i, j: (0, 0)))
        args.append(final_norm)
    out_specs = [pl.BlockSpec((tm, D_MODEL), lambda i, j: (prev_tile(i), 0))]
    out_shape = [jax.ShapeDtypeStruct((t, D_MODEL), F32)]
    if emit_next:
        out_specs.append(pl.BlockSpec((tm, D_MODEL), lambda i, j: (prev_tile(i), 0)))
        out_shape.append(jax.ShapeDtypeStruct((t, D_MODEL), BF16))
    return pl.pallas_call(
        functools.partial(_ffn_kernel, mod_base=mod_base, emit_next=emit_next, final=final, n_tiles=n_tiles),
        grid=(n_tiles + 1, nf),
        in_specs=in_specs,
        out_specs=out_specs,
        out_shape=out_shape,
        scratch_shapes=[pltpu.VMEM((tm, D_MODEL), BF16), pltpu.VMEM((tm, D_MODEL), BF16),
                        pltpu.VMEM((tm, D_MODEL), F32), pltpu.VMEM((tm, D_MODEL), F32)],
        compiler_params=_params("arbitrary", "arbitrary"),
        name="ffn_final" if final else "ffn",
    )(*args)


def _qkv_kernel(n_ref, w_ref, cos_ref, sin_ref, gq_ref, gk_ref, o_ref):
    y = jnp.dot(n_ref[...], w_ref[...], preferred_element_type=F32)
    cos = cos_ref[...]
    sin = sin_ref[...]
    gq = gq_ref[...] * Q_SCALE
    gk = gk_ref[...]
    for hh in range(ATTN_Q_HEADS + ATTN_KV_HEADS):
        sl = slice(hh * HEAD_DIM, (hh + 1) * HEAD_DIM)
        t = _rms(y[:, sl]) * (gq if hh < ATTN_Q_HEADS else gk)
        o_ref[:, sl] = (t * cos + pltpu.roll(t, HEAD_DIM // 2, 1) * sin).astype(BF16)
    o_ref[:, ATTN_WIDTH + KV_WIDTH:] = y[:, ATTN_WIDTH + KV_WIDTH:].astype(BF16)


def _qkv_proj(n, w, cos, sin, gq, gk, *, pos_tiles):
    t = n.shape[0]
    tm = TOKEN_TILE
    return pl.pallas_call(
        _qkv_kernel,
        grid=(t // tm,),
        in_specs=[
            pl.BlockSpec((tm, D_MODEL), lambda i: (i, 0)),
            pl.BlockSpec((D_MODEL, QKV_WIDTH), lambda i: (0, 0)),
            pl.BlockSpec((tm, HEAD_DIM), lambda i: (i % pos_tiles, 0)),
            pl.BlockSpec((tm, HEAD_DIM), lambda i: (i % pos_tiles, 0)),
            pl.BlockSpec((1, HEAD_DIM), lambda i: (0, 0)),
            pl.BlockSpec((1, HEAD_DIM), lambda i: (0, 0)),
        ],
        out_specs=pl.BlockSpec((tm, QKV_WIDTH), lambda i: (i, 0)),
        out_shape=jax.ShapeDtypeStruct((t, QKV_WIDTH), BF16),
        compiler_params=_params("parallel"),
        name="qkv_proj",
    )(n, w, cos, sin, gq, gk)


def _scaled_mm_kernel(n_ref, w_ref, s_ref, o_ref):
    y = jnp.dot(n_ref[...], w_ref[...], preferred_element_type=F32)
    o_ref[...] = (y * s_ref[...]).astype(o_ref.dtype)


def _scaled_mm(n, w, col_scale, out_dtype, name):
    t = n.shape[0]
    nn = w.shape[1]
    tm = TOKEN_TILE
    return pl.pallas_call(
        _scaled_mm_kernel,
        grid=(t // tm,),
        in_specs=[
            pl.BlockSpec((tm, D_MODEL), lambda i: (i, 0)),
            pl.BlockSpec((D_MODEL, nn), lambda i: (0, 0)),
            pl.BlockSpec((1, nn), lambda i: (0, 0)),
        ],
        out_specs=pl.BlockSpec((tm, nn), lambda i: (i, 0)),
        out_shape=jax.ShapeDtypeStruct((t, nn), out_dtype),
        compiler_params=_params("parallel"),
        name=name,
    )(n, w, col_scale)


def _attn_kernel(flag_ref, q_ref, kc_ref, kx_ref, vc_ref, vx_ref, o_ref,
                 k_sc, v_sc, qs_sc, acc_sc, m_sc):
    tq = q_ref.shape[1]
    tk = ATTN_KV_TILE
    n_ctx = kc_ref.shape[1]
    lk = k_sc.shape[0]
    nk = lk // tk

    @pl.when(pl.program_id(2) == 0)
    def _():
        k_sc[:n_ctx, :] = kc_ref[0]
        k_sc[n_ctx:, :] = kx_ref[0]
        v_sc[:n_ctx, :HEAD_DIM] = vc_ref[0]
        v_sc[n_ctx:, :HEAD_DIM] = vx_ref[0]
        lane = lax.broadcasted_iota(jnp.int32, (n_ctx, HEAD_DIM), 1)
        ones_col = jnp.where(lane == 0, 1.0, 0.0).astype(BF16)
        for r in range(lk // n_ctx):
            v_sc[r * n_ctx:(r + 1) * n_ctx, HEAD_DIM:] = ones_col

    for g in range(ATTN_GROUPS):
        qs_sc[g * tq:(g + 1) * tq, :] = q_ref[0, :, g * HEAD_DIM:(g + 1) * HEAD_DIM]

    def scores(c):
        start = pl.multiple_of(c * tk, tk)
        k = k_sc[pl.ds(start, tk), :]
        s = lax.dot_general(qs_sc[...], k, (((1,), (1,)), ((), ())), preferred_element_type=F32)
        return s, v_sc[pl.ds(start, tk), :]

    def finish():
        acc = acc_sc[...]
        out = acc[:, :HEAD_DIM] / acc[:, HEAD_DIM:HEAD_DIM + 1]
        for g in range(ATTN_GROUPS):
            o_ref[0, :, g * HEAD_DIM:(g + 1) * HEAD_DIM] = out[g * tq:(g + 1) * tq, :].astype(BF16)

    @pl.when(flag_ref[0] != 0)
    def _():
        acc_sc[...] = jnp.zeros_like(acc_sc)

        def body(c, carry):
            s, v = scores(c)
            acc_sc[...] += jnp.dot(jnp.exp2(s).astype(BF16), v, preferred_element_type=F32)
            return carry

        lax.fori_loop(0, nk, body, 0, unroll=True)
        finish()

    @pl.when(flag_ref[0] == 0)
    def _():
        acc_sc[...] = jnp.zeros_like(acc_sc)
        m_sc[...] = jnp.full_like(m_sc, -jnp.inf)

        def body(c, carry):
            s, v = scores(c)
            m_prev = m_sc[...]
            m_new = jnp.maximum(m_prev, jnp.max(s, axis=-1, keepdims=True))
            p = jnp.exp2(s - m_new).astype(BF16)
            acc_sc[...] = jnp.exp2(m_prev - m_new) * acc_sc[...] + jnp.dot(p, v, preferred_element_type=F32)
            m_sc[...] = m_new
            return carry

        lax.fori_loop(0, nk, body, 0)
        finish()


def _attention(bounded_flag, qkv_x, qkv_c):
    b, l, _ = qkv_x.shape
    lc = qkv_c.shape[1]
    lk = lc + l
    tq = ATTN_Q_TILE
    gw = ATTN_GROUPS * HEAD_DIM
    rows = ATTN_GROUPS * tq
    k_col = ATTN_WIDTH // HEAD_DIM
    v_col = k_col + ATTN_KV_HEADS
    lat = lambda col: pl.BlockSpec((1, l, HEAD_DIM), lambda bi, hi, qi: (bi, 0, hi + col))
    ctx = lambda col: pl.BlockSpec((1, lc, HEAD_DIM), lambda bi, hi, qi: (bi, 0, hi + col))
    return pl.pallas_call(
        _attn_kernel,
        grid=(b, ATTN_KV_HEADS, l // tq),
        in_specs=[
            pl.BlockSpec(memory_space=pltpu.SMEM),
            pl.BlockSpec((1, tq, gw), lambda bi, hi, qi: (bi, qi, hi)),
            ctx(k_col), lat(k_col), ctx(v_col), lat(v_col),
        ],
        out_specs=pl.BlockSpec((1, tq, gw), lambda bi, hi, qi: (bi, qi, hi)),
        out_shape=jax.ShapeDtypeStruct((b, l, ATTN_WIDTH), BF16),
        scratch_shapes=[
            pltpu.VMEM((lk, HEAD_DIM), BF16),
            pltpu.VMEM((lk, 2 * HEAD_DIM), BF16),
            pltpu.VMEM((rows, HEAD_DIM), BF16),
            pltpu.VMEM((rows, 2 * HEAD_DIM), F32),
            pltpu.VMEM((rows, 1), F32),
        ],
        compiler_params=_params("parallel", "parallel", "arbitrary"),
        name="attention",
    )(bounded_flag, qkv_x, qkv_c, qkv_x, qkv_c, qkv_x)


def _ret_kernel(lg_ref, q_ref, k_ref, v_ref, g_ref, kc_ref, vc_ref, o_ref, uf_sc, sb_sc):
    c_len = RET_CHUNK
    n_chunks = q_ref.shape[1] // c_len
    n_ctx = kc_ref.shape[1]
    head = pl.program_id(1)
    lgf = lg_ref[0, head]
    lgb = lg_ref[1, head]

    row = lax.broadcasted_iota(jnp.int32, (c_len, 1), 0).astype(F32)
    vdec_f = jnp.exp((c_len - 1.0 - row) * lgf)
    vdec_b = jnp.exp(row * lgb)
    qdec_f = jnp.exp((row + 1.0) * lgf)
    qdec_b = jnp.exp((c_len - row) * lgb)
    chunk_f = jnp.exp(jnp.full((1, RET_DIM), c_len, F32) * lgf)
    chunk_b = jnp.exp(jnp.full((1, RET_DIM), c_len, F32) * lgb)
    diff = (lax.broadcasted_iota(jnp.int32, (c_len, c_len), 0)
            - lax.broadcasted_iota(jnp.int32, (c_len, c_len), 1)).astype(F32)
    decay = (jnp.where(diff >= 0, jnp.exp(jnp.maximum(diff, 0.0) * lgf), 0.0)
             + jnp.where(diff <= 0, jnp.exp(jnp.maximum(-diff, 0.0) * lgb), 0.0))

    def kv_outer(k, v, dec_f, dec_b):
        vf = v.astype(F32)
        v2 = jnp.concatenate([(vf * dec_f).astype(BF16), (vf * dec_b).astype(BF16)], axis=1)
        return lax.dot_general(k, v2, (((0,), (0,)), ((), ())), preferred_element_type=F32)

    crow = lax.broadcasted_iota(jnp.int32, (n_ctx, 1), 0).astype(F32)
    s0 = kv_outer(kc_ref[0], vc_ref[0], jnp.exp((n_ctx - 1.0 - crow) * lgf), jnp.exp(crow * lgb))

    def back_body(t, sb):
        c = n_chunks - 1 - t
        start = pl.multiple_of(c * c_len, c_len)
        sb_sc[c] = sb
        u = kv_outer(k_ref[0, pl.ds(start, c_len), :], v_ref[0, pl.ds(start, c_len), :], vdec_f, vdec_b)
        uf_sc[c] = u[:, :RET_DIM]
        return sb * chunk_b + u[:, RET_DIM:]

    lax.fori_loop(0, n_chunks, back_body, s0[:, RET_DIM:], unroll=RET_UNROLL)

    def fwd_body(c, sf):
        start = pl.multiple_of(c * c_len, c_len)
        q = q_ref[0, pl.ds(start, c_len), :]
        k = k_ref[0, pl.ds(start, c_len), :]
        v = v_ref[0, pl.ds(start, c_len), :]
        inner = lax.dot_general(q, k, (((1,), (1,)), ((), ())), preferred_element_type=F32)
        y = jnp.dot((inner * decay).astype(BF16), v, preferred_element_type=F32)
        states = jnp.concatenate([sf.astype(BF16), sb_sc[c].astype(BF16)], axis=1)
        cross = jnp.dot(q, states, preferred_element_type=F32)
        y = y + cross[:, :RET_DIM] * qdec_f + cross[:, RET_DIM:] * qdec_b
        gate = g_ref[0, pl.ds(start, c_len), :]
        o_ref[0, pl.ds(start, c_len), :] = (gate * _sigmoid(gate) * _rms(y)).astype(BF16)
        return sf * chunk_f + uf_sc[c]

    lax.fori_loop(0, n_chunks, fwd_body, s0[:, :RET_DIM], unroll=RET_UNROLL)


def _retention(log_gamma, ret_x, gate_x, ret_c):
    b, l, _ = ret_x.shape
    lc = ret_c.shape[1]
    n_chunks = l // RET_CHUNK
    seq = lambda off: pl.BlockSpec((1, l, RET_DIM), lambda bi, hi: (bi, 0, hi + off))
    ctx = lambda off: pl.BlockSpec((1, lc, RET_DIM), lambda bi, hi: (bi, 0, hi + off))
    return pl.pallas_call(
        _ret_kernel,
        grid=(b, RET_HEADS),
        in_specs=[
            pl.BlockSpec(memory_space=pltpu.SMEM),
            seq(0), seq(RET_HEADS), seq(2 * RET_HEADS), seq(0),
            ctx(RET_HEADS), ctx(2 * RET_HEADS),
        ],
        out_specs=seq(0),
        out_shape=jax.ShapeDtypeStruct((b, l, RET_WIDTH), BF16),
        scratch_shapes=[
            pltpu.VMEM((n_chunks, RET_DIM, RET_DIM), F32),
            pltpu.VMEM((n_chunks, RET_DIM, RET_DIM), F32),
        ],
        compiler_params=_params("parallel", "arbitrary"),
        name="retention",
    )(log_gamma, ret_x, ret_x, ret_x, gate_x, ret_c, ret_c)


def _merge_kernel(h_ref, n_ref, ya_ref, yr_ref, mod_ref, wga_ref, wgb_ref, wpa_ref, wpr_ref, wo_ref,
                  out_ref, acc_sc):
    j = pl.program_id(1)

    @pl.when(j == 0)
    def _():
        acc_sc[...] = jnp.zeros_like(acc_sc)

    n = n_ref[...]
    ga = jnp.dot(n, wga_ref[...], preferred_element_type=F32)
    gb = jnp.dot(n, wgb_ref[...], preferred_element_type=F32)
    pa = jnp.dot(ya_ref[...], wpa_ref[...], preferred_element_type=F32)
    pr = jnp.dot(yr_ref[...], wpr_ref[...], preferred_element_type=F32)
    z = (_sigmoid(ga) * pa + _sigmoid(gb) * pr).astype(BF16)
    acc_sc[...] += jnp.dot(z, wo_ref[...], preferred_element_type=F32)

    @pl.when(j == pl.num_programs(1) - 1)
    def _():
        out_ref[...] = h_ref[...] + mod_ref[0, 5, 0:1, :] * acc_sc[...]


def _merge(h, n, ya, yr, mod, w_ga, w_gb, w_pa, w_pr, w_out, *, tiles_per_row):
    t = h.shape[0]
    tm, tc = TOKEN_TILE, MERGE_TILE
    return pl.pallas_call(
        _merge_kernel,
        grid=(t // tm, D_MODEL // tc),
        in_specs=[
            pl.BlockSpec((tm, D_MODEL), lambda i, j: (i, 0)),
            pl.BlockSpec((tm, D_MODEL), lambda i, j: (i, 0)),
            pl.BlockSpec((tm, ATTN_WIDTH), lambda i, j: (i, 0)),
            pl.BlockSpec((tm, RET_WIDTH), lambda i, j: (i, 0)),
            pl.BlockSpec((1, N_MOD, ROW_BLOCK, D_MODEL), lambda i, j: (i // tiles_per_row, 0, 0, 0)),
            pl.BlockSpec((D_MODEL, tc), lambda i, j: (0, j)),
            pl.BlockSpec((D_MODEL, tc), lambda i, j: (0, j)),
            pl.BlockSpec((ATTN_WIDTH, tc), lambda i, j: (0, j)),
            pl.BlockSpec((RET_WIDTH, tc), lambda i, j: (0, j)),
            pl.BlockSpec((tc, D_MODEL), lambda i, j: (j, 0)),
        ],
        out_specs=pl.BlockSpec((tm, D_MODEL), lambda i, j: (i, 0)),
        out_shape=jax.ShapeDtypeStruct((t, D_MODEL), F32),
        scratch_shapes=[pltpu.VMEM((tm, D_MODEL), F32)],
        compiler_params=_params("parallel", "arbitrary"),
        name="merge",
    )(h, n, ya, yr, mod, w_ga, w_gb, w_pa, w_pr, w_out)


def _rope_tables(seq_len):
    rows = seq_len // GRID_W
    row = jnp.repeat(jnp.arange(rows, dtype=F32), GRID_W)
    col = jnp.tile(jnp.arange(GRID_W, dtype=F32), rows)
    half = HEAD_DIM // 2
    inv_freq = ROPE_THETA ** (-jnp.arange(0, half, 2, dtype=F32) / half)
    ang = jnp.concatenate([row[:, None] * inv_freq, col[:, None] * inv_freq], axis=-1)
    cos, sin = jnp.cos(ang), jnp.sin(ang)
    return jnp.concatenate([cos, cos], axis=-1), jnp.concatenate([-sin, sin], axis=-1)


def _deinterleave(t):
    lead = t.shape[:-1]
    t = t.reshape(lead + (-1, HEAD_DIM // 2, 2))
    return jnp.swapaxes(t, -1, -2).reshape(lead + (-1,))


def kernel(x, c, ctx, c_ctx, w_ada, b_ada, ffn1_w_in, ffn1_w_out, mix_w_in, attn_q_gain, attn_k_gain,
           ret_decay_logit, w_proj_attn, w_proj_ret, mix_w_out, ffn2_w_in, ffn2_w_out, final_norm):
    batch, seq_len, _ = x.shape
    assert w_ada.shape[0] == 1, "single-layer block"
    assert seq_len % TOKEN_TILE == 0 and (batch * CTX_LEN) % TOKEN_TILE == 0
    tiles_per_row = seq_len // TOKEN_TILE

    w1_in, w1_out = ffn1_w_in[0].astype(BF16), ffn1_w_out[0].astype(BF16)
    w2_in, w2_out = ffn2_w_in[0].astype(BF16), ffn2_w_out[0].astype(BF16)
    w_mix = mix_w_in[0]
    w_qkv = jnp.concatenate([_deinterleave(w_mix[:, :ATTN_WIDTH + KV_WIDTH]),
                             w_mix[:, ATTN_WIDTH + KV_WIDTH:QKV_WIDTH]], axis=1).astype(BF16)
    w_ret = w_mix[:, RET_OFF:GR_OFF].astype(BF16)
    w_gr = w_mix[:, GR_OFF:GA_OFF].astype(BF16)
    w_ga = w_mix[:, GA_OFF:GB_OFF].astype(BF16)
    w_gb = w_mix[:, GB_OFF:].astype(BF16)
    w_pa, w_pr = w_proj_attn[0].astype(BF16), w_proj_ret[0].astype(BF16)
    w_mo = mix_w_out[0].astype(BF16)

    cond = jnp.zeros((8, D_MODEL), F32).at[:batch].set(c).at[batch].set(c_ctx)
    mod = _adaln(cond, w_ada[0], b_ada).reshape(8, N_MOD, 1, D_MODEL)
    mod = jnp.broadcast_to(mod, (8, N_MOD, ROW_BLOCK, D_MODEL))

    x2 = x.reshape(batch * seq_len, D_MODEL)
    c2 = ctx.reshape(batch * CTX_LEN, D_MODEL)
    h1, n2 = _ffn(x2, mod, w1_in, w1_out, mod_base=0, tiles_per_row=tiles_per_row, row_offset=0,
                  emit_next=True)
    _, n2c = _ffn(c2, mod, w1_in, w1_out, mod_base=0, tiles_per_row=batch * CTX_LEN, row_offset=batch,
                  emit_next=True)

    cos, sin = _rope_tables(seq_len)
    gq = _deinterleave(attn_q_gain[0].reshape(1, HEAD_DIM))
    gk = _deinterleave(attn_k_gain[0].reshape(1, HEAD_DIM))
    qkv_x = _qkv_proj(n2, w_qkv, cos, sin, gq, gk, pos_tiles=tiles_per_row)
    ones = jnp.ones((TOKEN_TILE, HEAD_DIM), F32)
    qkv_c = _qkv_proj(n2c, w_qkv, ones, jnp.zeros_like(ones), gq, gk, pos_tiles=1)

    ret_scale = jnp.concatenate([jnp.ones((1, RET_WIDTH), F32),
                                 jnp.full((1, RET_WIDTH), RET_DIM ** -0.5, F32),
                                 jnp.ones((1, RET_WIDTH), F32)], axis=1)
    ret_x = _scaled_mm(n2, w_ret, ret_scale, BF16, "ret_proj")
    ret_c = _scaled_mm(n2c, w_ret, ret_scale, BF16, "ret_proj")
    gate_x = _scaled_mm(n2, w_gr, jnp.ones((1, RET_WIDTH), F32), F32, "ret_gate_proj")

    qkv_x = qkv_x.reshape(batch, seq_len, QKV_WIDTH)
    qkv_c = qkv_c.reshape(batch, CTX_LEN, QKV_WIDTH)
    score_bound = HEAD_DIM ** 0.5 * jnp.max(jnp.abs(gq)) * jnp.max(jnp.abs(gk))
    ya = _attention((score_bound <= ATTN_SAFE_SCORE).astype(jnp.int32).reshape(1), qkv_x, qkv_c)

    log_gamma = jax.nn.log_sigmoid(ret_decay_logit[0].astype(F32))
    yr = _retention(log_gamma, ret_x.reshape(batch, seq_len, 3 * RET_WIDTH),
                    gate_x.reshape(batch, seq_len, RET_WIDTH),
                    ret_c.reshape(batch, CTX_LEN, 3 * RET_WIDTH))

    h2 = _merge(h1, n2, ya.reshape(batch * seq_len, ATTN_WIDTH), yr.reshape(batch * seq_len, RET_WIDTH), mod,
                w_ga, w_gb, w_pa, w_pr, w_mo, tiles_per_row=tiles_per_row)
    out = _ffn(h2, mod, w2_in, w2_out, mod_base=6, tiles_per_row=tiles_per_row, row_offset=0,
               final_norm=jnp.broadcast_to(final_norm.reshape(1, D_MODEL), (ROW_BLOCK, D_MODEL)))
    return out[0].reshape(batch, seq_len, D_MODEL)
```

```python
import functools

import jax
import jax.numpy as jnp
from jax import lax
from jax.experimental import pallas as pl
from jax.experimental.pallas import tpu as pltpu

D_MODEL = 2048
CTX_LEN = 256
GRID_W = 64
HEAD_DIM = 128
ATTN_Q_HEADS = 8
ATTN_KV_HEADS = 2
ATTN_GROUPS = ATTN_Q_HEADS // ATTN_KV_HEADS
ATTN_WIDTH = ATTN_Q_HEADS * HEAD_DIM
KV_WIDTH = ATTN_KV_HEADS * HEAD_DIM
RET_HEADS = 8
RET_DIM = 128
RET_WIDTH = RET_HEADS * RET_DIM
D_FF = 5632
ROPE_THETA = 10000.0
NORM_EPS = 1e-6
N_MOD = 9

QKV_WIDTH = ATTN_WIDTH + 2 * KV_WIDTH
RET_OFF = QKV_WIDTH
GR_OFF = RET_OFF + 3 * RET_WIDTH
GA_OFF = GR_OFF + RET_WIDTH
GB_OFF = GA_OFF + D_MODEL

F32 = jnp.float32
BF16 = jnp.bfloat16

VMEM_LIMIT_BYTES = 60 * 1024 * 1024

ADALN_TILE = 1024
TOKEN_TILE = 512
FF_TILE = 512
ROW_BLOCK = 16
ROW_UNROLL = 4
SIDE_BLOCKS = 3
MERGE_TILE = 512
ATTN_Q_TILE = 256
ATTN_KV_TILE = 768
RET_CHUNK = 256
RET_UNROLL = 4

Q_SCALE = HEAD_DIM ** -0.5 * 1.4426950408889634
ATTN_SAFE_SCORE = 32.0


def _params(*sem):
    return pltpu.CompilerParams(dimension_semantics=sem, vmem_limit_bytes=VMEM_LIMIT_BYTES)


def _rms(x):
    return x * lax.rsqrt(jnp.mean(x * x, axis=-1, keepdims=True) + NORM_EPS)


def _sigmoid(x):
    return 1.0 / (1.0 + jnp.exp(-x))


def _adaln_kernel(c_ref, w_ref, b_ref, o_ref):
    chunk = pl.program_id(0) // (D_MODEL // ADALN_TILE)
    c = c_ref[...]
    s = (c * _sigmoid(c)).astype(BF16)
    y = jnp.dot(s, w_ref[...].astype(BF16), preferred_element_type=F32) + b_ref[...]
    y = y + jnp.where(chunk % 3 == 1, 1.0, 0.0)
    o_ref[...] = y * jnp.where((chunk == 2) | (chunk == 8), 0.5, 1.0)


def _adaln(cond, w, b):
    n = w.shape[1]
    tn = ADALN_TILE
    return pl.pallas_call(
        _adaln_kernel,
        grid=(n // tn,),
        in_specs=[
            pl.BlockSpec((8, D_MODEL), lambda j: (0, 0)),
            pl.BlockSpec((D_MODEL, tn), lambda j: (0, j)),
            pl.BlockSpec((1, tn), lambda j: (0, j)),
        ],
        out_specs=pl.BlockSpec((8, tn), lambda j: (0, j)),
        out_shape=jax.ShapeDtypeStruct((8, n), F32),
        compiler_params=_params("arbitrary"),
        name="adaln",
    )(cond, w, b)


def _ffn_kernel(*refs, mod_base, emit_next, final, n_tiles):
    hp_ref, hn_ref, mp_ref, mn_ref, wa_ref, wb_ref, wo_ref = refs[:7]
    refs = refs[7:]
    if final:
        fn_ref, refs = refs[0], refs[1:]
    out_ref, refs = refs[0], refs[1:]
    if emit_next:
        nxt_ref, refs = refs[0], refs[1:]
    xn_bufs, acc_bufs = refs[:2], refs[2:]
    i = pl.program_id(0)
    j = pl.program_id(1)
    tm = hp_ref.shape[0]
    side_rows = SIDE_BLOCKS * ROW_BLOCK
    side_start = jnp.minimum(j * side_rows, tm - side_rows)

    def inv_rms(x):
        return lax.rsqrt(jnp.mean(x * x, axis=-1, keepdims=True) + NORM_EPS)

    def prenorm(h_ref, m_ref, xn_ref, rows):
        x = h_ref[rows, :]
        xn_ref[rows, :] = (x * inv_rms(x) * m_ref[0, mod_base + 1] + m_ref[0, mod_base]).astype(BF16)

    def finish(acc_ref, rows):
        h = hp_ref[rows, :] + mp_ref[0, mod_base + 2] * acc_ref[rows, :]
        if final:
            out_ref[rows, :] = h * inv_rms(h) * fn_ref[...]
        else:
            out_ref[rows, :] = h
        if emit_next:
            nxt_ref[rows, :] = (h * inv_rms(h) * mp_ref[0, mod_base + 4] + mp_ref[0, mod_base + 3]).astype(BF16)

    def side_blocks():
        return [pl.ds(pl.multiple_of(side_start + sub * ROW_BLOCK, ROW_BLOCK), ROW_BLOCK)
                for sub in range(SIDE_BLOCKS)]

    @pl.when((i == 0) & (j == 0))
    def _():
        def step(r, carry):
            prenorm(hp_ref, mp_ref, xn_bufs[0], pl.ds(pl.multiple_of(r * ROW_BLOCK, ROW_BLOCK), ROW_BLOCK))
            return carry

        lax.fori_loop(0, tm // ROW_BLOCK, step, 0, unroll=ROW_UNROLL)
        for acc_ref in acc_bufs:
            acc_ref[...] = jnp.zeros_like(acc_ref)

    for parity in range(2):
        xn_cur, xn_oth = xn_bufs[parity], xn_bufs[1 - parity]
        acc_cur, acc_oth = acc_bufs[parity], acc_bufs[1 - parity]

        @pl.when((i < n_tiles) & (i % 2 == parity))
        def _(xn_cur=xn_cur, xn_oth=xn_oth, acc_cur=acc_cur, acc_oth=acc_oth):
            for rows in side_blocks():
                finish(acc_oth, rows)
                prenorm(hn_ref, mn_ref, xn_oth, rows)
            xn = xn_cur[...]
            a = jnp.dot(xn, wa_ref[...], preferred_element_type=F32)
            b = jnp.dot(xn, wb_ref[...], preferred_element_type=F32)
            act = (a * _sigmoid(a) * b).astype(BF16)
            partial = jnp.dot(act, wo_ref[...], preferred_element_type=F32)
            acc_cur[...] = jnp.where(j == 0, 0.0, acc_cur[...]) + partial

    @pl.when(i == n_tiles)
    def _():
        for rows in side_blocks():
            finish(acc_bufs[1 - n_tiles % 2], rows)


def _ffn_w_in_blocks(w_in):
    w = w_in.reshape(D_MODEL, 2, D_FF // FF_TILE, FF_TILE)
    return jnp.transpose(w, (1, 2, 0, 3)).astype(BF16)


def _ffn(h, mod, w_in, w_out, *, mod_base, tiles_per_row, row_offset, emit_next=False, final_norm=None):
    t = h.shape[0]
    tm, tf = TOKEN_TILE, FF_TILE
    nf = D_FF // tf
    n_tiles = t // tm
    assert nf * SIDE_BLOCKS * ROW_BLOCK >= tm, "side work must cover a token tile per grid row"
    final = final_norm is not None
    prev_tile = lambda i: jnp.maximum(i - 1, 0)
    next_tile = lambda i: jnp.minimum(i + 1, n_tiles - 1)
    mod_row = lambda tile: tile // tiles_per_row + row_offset
    w_step = lambda i, j: jnp.where(i < n_tiles, j, nf - 1)
    in_specs = [
        pl.BlockSpec((tm, D_MODEL), lambda i, j: (prev_tile(i), 0)),
        pl.BlockSpec((tm, D_MODEL), lambda i, j: (next_tile(i), 0)),
        pl.BlockSpec((1, N_MOD, ROW_BLOCK, D_MODEL), lambda i, j: (mod_row(prev_tile(i)), 0, 0, 0)),
        pl.BlockSpec((1, N_MOD, ROW_BLOCK, D_MODEL), lambda i, j: (mod_row(next_tile(i)), 0, 0, 0)),
        pl.BlockSpec((None, None, D_MODEL, tf), lambda i, j: (0, w_step(i, j), 0, 0)),
        pl.BlockSpec((None, None, D_MODEL, tf), lambda i, j: (1, w_step(i, j), 0, 0)),
        pl.BlockSpec((tf, D_MODEL), lambda i, j: (w_step(i, j), 0)),
    ]
    args = [h, h, mod, mod, w_in, w_in, w_out]
    if final:
        in_specs.append(pl.BlockSpec((ROW_BLOCK, D_MODEL), lambda i, j: (0, 0)))
        args.append(final_norm)
    out_specs = [pl.BlockSpec((tm, D_MODEL), lambda i, j: (prev_tile(i), 0))]
    out_shape = [jax.ShapeDtypeStruct((t, D_MODEL), F32)]
    if emit_next:
        out_specs.append(pl.BlockSpec((tm, D_MODEL), lambda i, j: (prev_tile(i), 0)))
        out_shape.append(jax.ShapeDtypeStruct((t, D_MODEL), BF16))
    return pl.pallas_call(
        functools.partial(_ffn_kernel, mod_base=mod_base, emit_next=emit_next, final=final, n_tiles=n_tiles),
        grid=(n_tiles + 1, nf),
        in_specs=in_specs,
        out_specs=out_specs,
        out_shape=out_shape,
        scratch_shapes=[pltpu.VMEM((tm, D_MODEL), BF16), pltpu.VMEM((tm, D_MODEL), BF16),
                        pltpu.VMEM((tm, D_MODEL), F32), pltpu.VMEM((tm, D_MODEL), F32)],
        compiler_params=_params("arbitrary", "arbitrary"),
        name="ffn_final" if final else "ffn",
    )(*args)


def _qkv_kernel(n_ref, w_ref, cos_ref, sin_ref, gq_ref, gk_ref, o_ref):
    y = jnp.dot(n_ref[...], w_ref[...], preferred_element_type=F32)
    cos = cos_ref[...]
    sin = sin_ref[...]
    gq = gq_ref[...] * Q_SCALE
    gk = gk_ref[...]
    for hh in range(ATTN_Q_HEADS + ATTN_KV_HEADS):
        sl = slice(hh * HEAD_DIM, (hh + 1) * HEAD_DIM)
        t = _rms(y[:, sl]) * (gq if hh < ATTN_Q_HEADS else gk)
        o_ref[:, sl] = (t * cos + pltpu.roll(t, HEAD_DIM // 2, 1) * sin).astype(BF16)
    o_ref[:, ATTN_WIDTH + KV_WIDTH:] = y[:, ATTN_WIDTH + KV_WIDTH:].astype(BF16)


def _qkv_proj(n, w, cos, sin, gq, gk, *, pos_tiles):
    t = n.shape[0]
    tm = TOKEN_TILE
    return pl.pallas_call(
        _qkv_kernel,
        grid=(t // tm,),
        in_specs=[
            pl.BlockSpec((tm, D_MODEL), lambda i: (i, 0)),
            pl.BlockSpec((D_MODEL, QKV_WIDTH), lambda i: (0, 0)),
            pl.BlockSpec((tm, HEAD_DIM), lambda i: (i % pos_tiles, 0)),
            pl.BlockSpec((tm, HEAD_DIM), lambda i: (i % pos_tiles, 0)),
            pl.BlockSpec((1, HEAD_DIM), lambda i: (0, 0)),
            pl.BlockSpec((1, HEAD_DIM), lambda i: (0, 0)),
        ],
        out_specs=pl.BlockSpec((tm, QKV_WIDTH), lambda i: (i, 0)),
        out_shape=jax.ShapeDtypeStruct((t, QKV_WIDTH), BF16),
        compiler_params=_params("parallel"),
        name="qkv_proj",
    )(n, w, cos, sin, gq, gk)


def _scaled_mm_kernel(n_ref, w_ref, s_ref, o_ref):
    y = jnp.dot(n_ref[...], w_ref[...], preferred_element_type=F32)
    o_ref[...] = (y * s_ref[...]).astype(o_ref.dtype)


def _scaled_mm(n, w, col_scale, out_dtype, name):
    t = n.shape[0]
    nn = w.shape[1]
    tm = TOKEN_TILE
    return pl.pallas_call(
        _scaled_mm_kernel,
        grid=(t // tm,),
        in_specs=[
            pl.BlockSpec((tm, D_MODEL), lambda i: (i, 0)),
            pl.BlockSpec((D_MODEL, nn), lambda i: (0, 0)),
            pl.BlockSpec((1, nn), lambda i: (0, 0)),
        ],
        out_specs=pl.BlockSpec((tm, nn), lambda i: (i, 0)),
        out_shape=jax.ShapeDtypeStruct((t, nn), out_dtype),
        compiler_params=_params("parallel"),
        name=name,
    )(n, w, col_scale)


def _attn_kernel(flag_ref, q_ref, kc_ref, kx_ref, vc_ref, vx_ref, o_ref,
                 k_sc, v_sc, qs_sc, acc_sc, m_sc):
    tq = q_ref.shape[1]
    tk = ATTN_KV_TILE
    n_ctx = kc_ref.shape[1]
    lk = k_sc.shape[0]
    nk = lk // tk

    @pl.when(pl.program_id(2) == 0)
    def _():
        k_sc[:n_ctx, :] = kc_ref[0]
        k_sc[n_ctx:, :] = kx_ref[0]
        v_sc[:n_ctx, :HEAD_DIM] = vc_ref[0]
        v_sc[n_ctx:, :HEAD_DIM] = vx_ref[0]
        lane = lax.broadcasted_iota(jnp.int32, (n_ctx, HEAD_DIM), 1)
        ones_col = jnp.where(lane == 0, 1.0, 0.0).astype(BF16)
        for r in range(lk // n_ctx):
            v_sc[r * n_ctx:(r + 1) * n_ctx, HEAD_DIM:] = ones_col

    for g in range(ATTN_GROUPS):
        qs_sc[g * tq:(g + 1) * tq, :] = q_ref[0, :, g * HEAD_DIM:(g + 1) * HEAD_DIM]

    def scores(c):
        start = pl.multiple_of(c * tk, tk)
        k = k_sc[pl.ds(start, tk), :]
        s = lax.dot_general(qs_sc[...], k, (((1,), (1,)), ((), ())), preferred_element_type=F32)
        return s, v_sc[pl.ds(start, tk), :]

    def finish():
        acc = acc_sc[...]
        out = acc[:, :HEAD_DIM] / acc[:, HEAD_DIM:HEAD_DIM + 1]
        for g in range(ATTN_GROUPS):
            o_ref[0, :, g * HEAD_DIM:(g + 1) * HEAD_DIM] = out[g * tq:(g + 1) * tq, :].astype(BF16)

    @pl.when(flag_ref[0] != 0)
    def _():
        acc_sc[...] = jnp.zeros_like(acc_sc)

        def body(c, carry):
            s, v = scores(c)
            acc_sc[...] += jnp.dot(jnp.exp2(s).astype(BF16), v, preferred_element_type=F32)
            return carry

        lax.fori_loop(0, nk, body, 0, unroll=True)
        finish()

    @pl.when(flag_ref[0] == 0)
    def _():
        acc_sc[...] = jnp.zeros_like(acc_sc)
        m_sc[...] = jnp.full_like(m_sc, -jnp.inf)

        def body(c, carry):
            s, v = scores(c)
            m_prev = m_sc[...]
            m_new = jnp.maximum(m_prev, jnp.max(s, axis=-1, keepdims=True))
            p = jnp.exp2(s - m_new).astype(BF16)
            acc_sc[...] = jnp.exp2(m_prev - m_new) * acc_sc[...] + jnp.dot(p, v, preferred_element_type=F32)
            m_sc[...] = m_new
            return carry

        lax.fori_loop(0, nk, body, 0)
        finish()


def _attention(bounded_flag, qkv_x, qkv_c):
    b, l, _ = qkv_x.shape
    lc = qkv_c.shape[1]
    lk = lc + l
    tq = ATTN_Q_TILE
    gw = ATTN_GROUPS * HEAD_DIM
    rows = ATTN_GROUPS * tq
    k_col = ATTN_WIDTH // HEAD_DIM
    v_col = k_col + ATTN_KV_HEADS
    lat = lambda col: pl.BlockSpec((1, l, HEAD_DIM), lambda bi, hi, qi: (bi, 0, hi + col))
    ctx = lambda col: pl.BlockSpec((1, lc, HEAD_DIM), lambda bi, hi, qi: (bi, 0, hi + col))
    return pl.pallas_call(
        _attn_kernel,
        grid=(b, ATTN_KV_HEADS, l // tq),
        in_specs=[
            pl.BlockSpec(memory_space=pltpu.SMEM),
            pl.BlockSpec((1, tq, gw), lambda bi, hi, qi: (bi, qi, hi)),
            ctx(k_col), lat(k_col), ctx(v_col), lat(v_col),
        ],
        out_specs=pl.BlockSpec((1, tq, gw), lambda bi, hi, qi: (bi, qi, hi)),
        out_shape=jax.ShapeDtypeStruct((b, l, ATTN_WIDTH), BF16),
        scratch_shapes=[
            pltpu.VMEM((lk, HEAD_DIM), BF16),
            pltpu.VMEM((lk, 2 * HEAD_DIM), BF16),
            pltpu.VMEM((rows, HEAD_DIM), BF16),
            pltpu.VMEM((rows, 2 * HEAD_DIM), F32),
            pltpu.VMEM((rows, 1), F32),
        ],
        compiler_params=_params("parallel", "parallel", "arbitrary"),
        name="attention",
    )(bounded_flag, qkv_x, qkv_c, qkv_x, qkv_c, qkv_x)


def _ret_kernel(lg_ref, q_ref, k_ref, v_ref, g_ref, kc_ref, vc_ref, o_ref, uf_sc, sb_sc):
    c_len = RET_CHUNK
    n_chunks = q_ref.shape[1] // c_len
    n_ctx = kc_ref.shape[1]
    head = pl.program_id(1)
    lgf = lg_ref[0, head]
    lgb = lg_ref[1, head]

    row = lax.broadcasted_iota(jnp.int32, (c_len, 1), 0).astype(F32)
    vdec_f = jnp.exp((c_len - 1.0 - row) * lgf)
    vdec_b = jnp.exp(row * lgb)
    qdec_f = jnp.exp((row + 1.0) * lgf)
    qdec_b = jnp.exp((c_len - row) * lgb)
    chunk_f = jnp.exp(jnp.full((1, RET_DIM), c_len, F32) * lgf)
    chunk_b = jnp.exp(jnp.full((1, RET_DIM), c_len, F32) * lgb)
    diff = (lax.broadcasted_iota(jnp.int32, (c_len, c_len), 0)
            - lax.broadcasted_iota(jnp.int32, (c_len, c_len), 1)).astype(F32)
    decay = (jnp.where(diff >= 0, jnp.exp(jnp.maximum(diff, 0.0) * lgf), 0.0)
             + jnp.where(diff <= 0, jnp.exp(jnp.maximum(-diff, 0.0) * lgb), 0.0))

    def kv_outer(k, v, dec_f, dec_b):
        vf = v.astype(F32)
        v2 = jnp.concatenate([(vf * dec_f).astype(BF16), (vf * dec_b).astype(BF16)], axis=1)
        return lax.dot_general(k, v2, (((0,), (0,)), ((), ())), preferred_element_type=F32)

    crow = lax.broadcasted_iota(jnp.int32, (n_ctx, 1), 0).astype(F32)
    s0 = kv_outer(kc_ref[0], vc_ref[0], jnp.exp((n_ctx - 1.0 - crow) * lgf), jnp.exp(crow * lgb))

    def back_body(t, sb):
        c = n_chunks - 1 - t
        start = pl.multiple_of(c * c_len, c_len)
        sb_sc[c] = sb
        u = kv_outer(k_ref[0, pl.ds(start, c_len), :], v_ref[0, pl.ds(start, c_len), :], vdec_f, vdec_b)
        uf_sc[c] = u[:, :RET_DIM]
        return sb * chunk_b + u[:, RET_DIM:]

    lax.fori_loop(0, n_chunks, back_body, s0[:, RET_DIM:], unroll=RET_UNROLL)

    def fwd_body(c, sf):
        start = pl.multiple_of(c * c_len, c_len)
        q = q_ref[0, pl.ds(start, c_len), :]
        k = k_ref[0, pl.ds(start, c_len), :]
        v = v_ref[0, pl.ds(start, c_len), :]
        inner = lax.dot_general(q, k, (((1,), (1,)), ((), ())), preferred_element_type=F32)
        y = jnp.dot((inner * decay).astype(BF16), v, preferred_element_type=F32)
        states = jnp.concatenate([sf.astype(BF16), sb_sc[c].astype(BF16)], axis=1)
        cross = jnp.dot(q, states, preferred_element_type=F32)
        y = y + cross[:, :RET_DIM] * qdec_f + cross[:, RET_DIM:] * qdec_b
        gate = g_ref[0, pl.ds(start, c_len), :]
        o_ref[0, pl.ds(start, c_len), :] = (gate * _sigmoid(gate) * _rms(y)).astype(BF16)
        return sf * chunk_f + uf_sc[c]

    lax.fori_loop(0, n_chunks, fwd_body, s0[:, :RET_DIM], unroll=RET_UNROLL)


def _retention(log_gamma, ret_x, gate_x, ret_c):
    b, l, _ = ret_x.shape
    lc = ret_c.shape[1]
    n_chunks = l // RET_CHUNK
    seq = lambda off: pl.BlockSpec((1, l, RET_DIM), lambda bi, hi: (bi, 0, hi + off))
    ctx = lambda off: pl.BlockSpec((1, lc, RET_DIM), lambda bi, hi: (bi, 0, hi + off))
    return pl.pallas_call(
        _ret_kernel,
        grid=(b, RET_HEADS),
        in_specs=[
            pl.BlockSpec(memory_space=pltpu.SMEM),
            seq(0), seq(RET_HEADS), seq(2 * RET_HEADS), seq(0),
            ctx(RET_HEADS), ctx(2 * RET_HEADS),
        ],
        out_specs=seq(0),
        out_shape=jax.ShapeDtypeStruct((b, l, RET_WIDTH), BF16),
        scratch_shapes=[
            pltpu.VMEM((n_chunks, RET_DIM, RET_DIM), F32),
            pltpu.VMEM((n_chunks, RET_DIM, RET_DIM), F32),
        ],
        compiler_params=_params("parallel", "arbitrary"),
        name="retention",
    )(log_gamma, ret_x, ret_x, ret_x, gate_x, ret_c, ret_c)


def _merge_kernel(h_ref, n_ref, ya_ref, yr_ref, mod_ref, wga_ref, wgb_ref, wpa_ref, wpr_ref, wo_ref,
                  out_ref, acc_sc):
    j = pl.program_id(1)

    @pl.when(j == 0)
    def _():
        acc_sc[...] = jnp.zeros_like(acc_sc)

    n = n_ref[...]
    ga = jnp.dot(n, wga_ref[...], preferred_element_type=F32)
    gb = jnp.dot(n, wgb_ref[...], preferred_element_type=F32)
    pa = jnp.dot(ya_ref[...], wpa_ref[...], preferred_element_type=F32)
    pr = jnp.dot(yr_ref[...], wpr_ref[...], preferred_element_type=F32)
    z = (_sigmoid(ga) * pa + _sigmoid(gb) * pr).astype(BF16)
    acc_sc[...] += jnp.dot(z, wo_ref[...], preferred_element_type=F32)

    @pl.when(j == pl.num_programs(1) - 1)
    def _():
        out_ref[...] = h_ref[...] + mod_ref[0, 5, 0:1, :] * acc_sc[...]


def _column_blocks(w):
    k = w.shape[0]
    return jnp.transpose(w.reshape(k, D_MODEL // MERGE_TILE, MERGE_TILE), (1, 0, 2)).astype(BF16)


def _merge(h, n, ya, yr, mod, w_ga, w_gb, w_pa, w_pr, w_out, *, tiles_per_row):
    t = h.shape[0]
    tm, tc = TOKEN_TILE, MERGE_TILE
    return pl.pallas_call(
        _merge_kernel,
        grid=(t // tm, D_MODEL // tc),
        in_specs=[
            pl.BlockSpec((tm, D_MODEL), lambda i, j: (i, 0)),
            pl.BlockSpec((tm, D_MODEL), lambda i, j: (i, 0)),
            pl.BlockSpec((tm, ATTN_WIDTH), lambda i, j: (i, 0)),
            pl.BlockSpec((tm, RET_WIDTH), lambda i, j: (i, 0)),
            pl.BlockSpec((1, N_MOD, ROW_BLOCK, D_MODEL), lambda i, j: (i // tiles_per_row, 0, 0, 0)),
            pl.BlockSpec((None, D_MODEL, tc), lambda i, j: (j, 0, 0)),
            pl.BlockSpec((None, D_MODEL, tc), lambda i, j: (j, 0, 0)),
            pl.BlockSpec((None, ATTN_WIDTH, tc), lambda i, j: (j, 0, 0)),
            pl.BlockSpec((None, RET_WIDTH, tc), lambda i, j: (j, 0, 0)),
            pl.BlockSpec((tc, D_MODEL), lambda i, j: (j, 0)),
        ],
        out_specs=pl.BlockSpec((tm, D_MODEL), lambda i, j: (i, 0)),
        out_shape=jax.ShapeDtypeStruct((t, D_MODEL), F32),
        scratch_shapes=[pltpu.VMEM((tm, D_MODEL), F32)],
        compiler_params=_params("parallel", "arbitrary"),
        name="merge",
    )(h, n, ya, yr, mod, w_ga, w_gb, w_pa, w_pr, w_out)


def _rope_tables(seq_len):
    rows = seq_len // GRID_W
    row = jnp.repeat(jnp.arange(rows, dtype=F32), GRID_W)
    col = jnp.tile(jnp.arange(GRID_W, dtype=F32), rows)
    half = HEAD_DIM // 2
    inv_freq = ROPE_THETA ** (-jnp.arange(0, half, 2, dtype=F32) / half)
    ang = jnp.concatenate([row[:, None] * inv_freq, col[:, None] * inv_freq], axis=-1)
    cos, sin = jnp.cos(ang), jnp.sin(ang)
    return jnp.concatenate([cos, cos], axis=-1), jnp.concatenate([-sin, sin], axis=-1)


def _deinterleave(t):
    lead = t.shape[:-1]
    t = t.reshape(lead + (-1, HEAD_DIM // 2, 2))
    return jnp.swapaxes(t, -1, -2).reshape(lead + (-1,))


def kernel(x, c, ctx, c_ctx, w_ada, b_ada, ffn1_w_in, ffn1_w_out, mix_w_in, attn_q_gain, attn_k_gain,
           ret_decay_logit, w_proj_attn, w_proj_ret, mix_w_out, ffn2_w_in, ffn2_w_out, final_norm):
    batch, seq_len, _ = x.shape
    assert w_ada.shape[0] == 1, "single-layer block"
    assert seq_len % TOKEN_TILE == 0 and (batch * CTX_LEN) % TOKEN_TILE == 0
    tiles_per_row = seq_len // TOKEN_TILE

    w1_in, w1_out = _ffn_w_in_blocks(ffn1_w_in[0]), ffn1_w_out[0].astype(BF16)
    w2_in, w2_out = _ffn_w_in_blocks(ffn2_w_in[0]), ffn2_w_out[0].astype(BF16)
    w_mix = mix_w_in[0]
    w_qkv = jnp.concatenate([_deinterleave(w_mix[:, :ATTN_WIDTH + KV_WIDTH]),
                             w_mix[:, ATTN_WIDTH + KV_WIDTH:QKV_WIDTH]], axis=1).astype(BF16)
    w_ret = w_mix[:, RET_OFF:GR_OFF].astype(BF16)
    w_gr = w_mix[:, GR_OFF:GA_OFF].astype(BF16)
    w_ga = _column_blocks(w_mix[:, GA_OFF:GB_OFF])
    w_gb = _column_blocks(w_mix[:, GB_OFF:])
    w_pa, w_pr = _column_blocks(w_proj_attn[0]), _column_blocks(w_proj_ret[0])
    w_mo = mix_w_out[0].astype(BF16)

    cond = jnp.zeros((8, D_MODEL), F32).at[:batch].set(c).at[batch].set(c_ctx)
    mod = _adaln(cond, w_ada[0], b_ada).reshape(8, N_MOD, 1, D_MODEL)
    mod = jnp.broadcast_to(mod, (8, N_MOD, ROW_BLOCK, D_MODEL))

    x2 = x.reshape(batch * seq_len, D_MODEL)
    c2 = ctx.reshape(batch * CTX_LEN, D_MODEL)
    h1, n2 = _ffn(x2, mod, w1_in, w1_out, mod_base=0, tiles_per_row=tiles_per_row, row_offset=0,
                  emit_next=True)
    _, n2c = _ffn(c2, mod, w1_in, w1_out, mod_base=0, tiles_per_row=batch * CTX_LEN, row_offset=batch,
                  emit_next=True)

    cos, sin = _rope_tables(seq_len)
    gq = _deinterleave(attn_q_gain[0].reshape(1, HEAD_DIM))
    gk = _deinterleave(attn_k_gain[0].reshape(1, HEAD_DIM))
    qkv_x = _qkv_proj(n2, w_qkv, cos, sin, gq, gk, pos_tiles=tiles_per_row)
    ones = jnp.ones((TOKEN_TILE, HEAD_DIM), F32)
    qkv_c = _qkv_proj(n2c, w_qkv, ones, jnp.zeros_like(ones), gq, gk, pos_tiles=1)

    ret_scale = jnp.concatenate([jnp.ones((1, RET_WIDTH), F32),
                                 jnp.full((1, RET_WIDTH), RET_DIM ** -0.5, F32),
                                 jnp.ones((1, RET_WIDTH), F32)], axis=1)
    ret_x = _scaled_mm(n2, w_ret, ret_scale, BF16, "ret_proj")
    ret_c = _scaled_mm(n2c, w_ret, ret_scale, BF16, "ret_proj")
    gate_x = _scaled_mm(n2, w_gr, jnp.ones((1, RET_WIDTH), F32), F32, "ret_gate_proj")

    qkv_x = qkv_x.reshape(batch, seq_len, QKV_WIDTH)
    qkv_c = qkv_c.reshape(batch, CTX_LEN, QKV_WIDTH)
    score_bound = HEAD_DIM ** 0.5 * jnp.max(jnp.abs(gq)) * jnp.max(jnp.abs(gk))
    ya = _attention((score_bound <= ATTN_SAFE_SCORE).astype(jnp.int32).reshape(1), qkv_x, qkv_c)

    log_gamma = jax.nn.log_sigmoid(ret_decay_logit[0].astype(F32))
    yr = _retention(log_gamma, ret_x.reshape(batch, seq_len, 3 * RET_WIDTH),
                    gate_x.reshape(batch, seq_len, RET_WIDTH),
                    ret_c.reshape(batch, CTX_LEN, 3 * RET_WIDTH))

    h2 = _merge(h1, n2, ya.reshape(batch * seq_len, ATTN_WIDTH), yr.reshape(batch * seq_len, RET_WIDTH), mod,
                w_ga, w_gb, w_pa, w_pr, w_mo, tiles_per_row=tiles_per_row)
    out = _ffn(h2, mod, w2_in, w2_out, mod_base=6, tiles_per_row=tiles_per_row, row_offset=0,
               final_norm=jnp.broadcast_to(final_norm.reshape(1, D_MODEL), (ROW_BLOCK, D_MODEL)))
    return out[0].reshape(batch, seq_len, D_MODEL)
```

```python
import functools

import jax
import jax.numpy as jnp
from jax import lax
from jax.experimental import pallas as pl
from jax.experimental.pallas import tpu as pltpu

D_MODEL = 2048
CTX_LEN = 256
GRID_W = 64
HEAD_DIM = 128
LANES = 128
ATTN_Q_HEADS = 8
ATTN_KV_HEADS = 2
ATTN_GROUPS = ATTN_Q_HEADS // ATTN_KV_HEADS
ATTN_WIDTH = ATTN_Q_HEADS * HEAD_DIM
KV_WIDTH = ATTN_KV_HEADS * HEAD_DIM
RET_HEADS = 8
RET_DIM = 128
RET_WIDTH = RET_HEADS * RET_DIM
D_FF = 5632
ROPE_THETA = 10000.0
NORM_EPS = 1e-6
N_MOD = 9

QKV_WIDTH = ATTN_WIDTH + 2 * KV_WIDTH
RET_OFF = QKV_WIDTH
GR_OFF = RET_OFF + 3 * RET_WIDTH
GA_OFF = GR_OFF + RET_WIDTH
GB_OFF = GA_OFF + D_MODEL

F32 = jnp.float32
BF16 = jnp.bfloat16

VMEM_LIMIT_BYTES = 60 * 1024 * 1024

ADALN_TILE = 1024
TOKEN_TILE = 512
FF_TILE = 512
FFN2_TOKEN_TILE = 1024
FFN2_FF_TILE = 256
ROW_BLOCK = 16
ROW_UNROLL = 4
MERGE_TILE = 512
ATTN_Q_TILE = 256
ATTN_KV_TILE = 768
RET_CHUNK = 256
RET_UNROLL = 4

Q_SCALE = HEAD_DIM ** -0.5 * 1.4426950408889634
ATTN_SAFE_SCORE = 32.0


def _params(*sem):
    return pltpu.CompilerParams(dimension_semantics=sem, vmem_limit_bytes=VMEM_LIMIT_BYTES)


def _rms(x):
    return x * lax.rsqrt(jnp.mean(x * x, axis=-1, keepdims=True) + NORM_EPS)


def _sigmoid(x):
    return 1.0 / (1.0 + jnp.exp(-x))


def _adaln_kernel(c_ref, w_ref, b_ref, o_ref):
    chunk = pl.program_id(0) // (D_MODEL // ADALN_TILE)
    c = c_ref[...]
    s = (c * _sigmoid(c)).astype(BF16)
    y = jnp.dot(s, w_ref[...].astype(BF16), preferred_element_type=F32) + b_ref[...]
    y = y + jnp.where(chunk % 3 == 1, 1.0, 0.0)
    o_ref[...] = y * jnp.where((chunk == 2) | (chunk == 8), 0.5, 1.0)


def _adaln(cond, w, b):
    n = w.shape[1]
    tn = ADALN_TILE
    return pl.pallas_call(
        _adaln_kernel,
        grid=(n // tn,),
        in_specs=[
            pl.BlockSpec((8, D_MODEL), lambda j: (0, 0)),
            pl.BlockSpec((D_MODEL, tn), lambda j: (0, j)),
            pl.BlockSpec((1, tn), lambda j: (0, j)),
        ],
        out_specs=pl.BlockSpec((8, tn), lambda j: (0, j)),
        out_shape=jax.ShapeDtypeStruct((8, n), F32),
        compiler_params=_params("arbitrary"),
        name="adaln",
    )(cond, w, b)


def _ffn_kernel(*refs, mod_base, emit_next, final):
    h_ref, mod_ref, wa_ref, wb_ref, wo_ref = refs[:5]
    refs = refs[5:]
    if final:
        fn_ref, refs = refs[0], refs[1:]
    out_ref, refs = refs[0], refs[1:]
    if emit_next:
        nxt_ref, refs = refs[0], refs[1:]
    xn_sc, acc_sc, inv_sc = refs
    j = pl.program_id(1)
    n_row_blocks = h_ref.shape[0] // ROW_BLOCK

    lane_tiles = [slice(k, k + LANES) for k in range(0, D_MODEL, LANES)]

    def row_block(r):
        return pl.ds(pl.multiple_of(r * ROW_BLOCK, ROW_BLOCK), ROW_BLOCK)

    def inv_rms(x):
        inv = lax.rsqrt(jnp.mean(x * x, axis=-1, keepdims=True) + NORM_EPS)
        return jnp.broadcast_to(inv, (x.shape[0], LANES))

    def for_row_blocks(body, unroll):
        def step(r, carry):
            body(row_block(r))
            return carry

        lax.fori_loop(0, n_row_blocks, step, 0, unroll=unroll)

    @pl.when(j == 0)
    def _():
        def stats(rows):
            inv_sc[rows, :] = inv_rms(h_ref[rows, :])

        def prenorm(rows):
            inv = inv_sc[rows, :]
            for sl in lane_tiles:
                n = h_ref[rows, sl] * inv * mod_ref[0, mod_base + 1, :, sl] + mod_ref[0, mod_base, :, sl]
                xn_sc[rows, sl] = n.astype(BF16)

        for_row_blocks(stats, True)
        for_row_blocks(prenorm, ROW_UNROLL)
        acc_sc[...] = jnp.zeros_like(acc_sc)

    xn = xn_sc[...]
    a = jnp.dot(xn, wa_ref[...], preferred_element_type=F32)
    b = jnp.dot(xn, wb_ref[...], preferred_element_type=F32)
    act = (a * _sigmoid(a) * b).astype(BF16)
    acc_sc[...] += jnp.dot(act, wo_ref[...], preferred_element_type=F32)

    @pl.when(j == pl.num_programs(1) - 1)
    def _():
        def residual(rows):
            h = h_ref[rows, :] + mod_ref[0, mod_base + 2] * acc_sc[rows, :]
            out_ref[rows, :] = h
            inv_sc[rows, :] = inv_rms(h)

        def postnorm(rows):
            inv = inv_sc[rows, :]
            for sl in lane_tiles:
                n = out_ref[rows, sl] * inv
                if final:
                    out_ref[rows, sl] = n * fn_ref[:, sl]
                else:
                    n = n * mod_ref[0, mod_base + 4, :, sl] + mod_ref[0, mod_base + 3, :, sl]
                    nxt_ref[rows, sl] = n.astype(BF16)

        for_row_blocks(residual, True)
        if final or emit_next:
            for_row_blocks(postnorm, ROW_UNROLL)


def _ffn(h, mod, w_in, w_out, *, mod_base, rows_per_mod, row_offset, tm, tf, emit_next=False, final_norm=None):
    t = h.shape[0]
    nf = D_FF // tf
    final = final_norm is not None
    tiles_per_row = rows_per_mod // tm
    mod_map = lambda i, j: (i // tiles_per_row + row_offset, 0, 0, 0)
    in_specs = [
        pl.BlockSpec((tm, D_MODEL), lambda i, j: (i, 0)),
        pl.BlockSpec((1, N_MOD, ROW_BLOCK, D_MODEL), mod_map),
        pl.BlockSpec((D_MODEL, tf), lambda i, j: (0, j)),
        pl.BlockSpec((D_MODEL, tf), lambda i, j: (0, j + nf)),
        pl.BlockSpec((tf, D_MODEL), lambda i, j: (j, 0)),
    ]
    args = [h, mod, w_in, w_in, w_out]
    if final:
        in_specs.append(pl.BlockSpec((ROW_BLOCK, D_MODEL), lambda i, j: (0, 0)))
        args.append(final_norm)
    out_specs = [pl.BlockSpec((tm, D_MODEL), lambda i, j: (i, 0))]
    out_shape = [jax.ShapeDtypeStruct((t, D_MODEL), F32)]
    if emit_next:
        out_specs.append(pl.BlockSpec((tm, D_MODEL), lambda i, j: (i, 0)))
        out_shape.append(jax.ShapeDtypeStruct((t, D_MODEL), BF16))
    return pl.pallas_call(
        functools.partial(_ffn_kernel, mod_base=mod_base, emit_next=emit_next, final=final),
        grid=(t // tm, nf),
        in_specs=in_specs,
        out_specs=out_specs,
        out_shape=out_shape,
        scratch_shapes=[pltpu.VMEM((tm, D_MODEL), BF16), pltpu.VMEM((tm, D_MODEL), F32),
                        pltpu.VMEM((tm, LANES), F32)],
        compiler_params=_params("parallel", "arbitrary"),
        name="ffn_final" if final else "ffn",
    )(*args)


def _qkv_kernel(n_ref, w_ref, cos_ref, sin_ref, gq_ref, gk_ref, o_ref):
    y = jnp.dot(n_ref[...], w_ref[...], preferred_element_type=F32)
    cos = cos_ref[...]
    sin = sin_ref[...]
    gq = gq_ref[...] * Q_SCALE
    gk = gk_ref[...]
    for hh in range(ATTN_Q_HEADS + ATTN_KV_HEADS):
        sl = slice(hh * HEAD_DIM, (hh + 1) * HEAD_DIM)
        t = _rms(y[:, sl]) * (gq if hh < ATTN_Q_HEADS else gk)
        o_ref[:, sl] = (t * cos + pltpu.roll(t, HEAD_DIM // 2, 1) * sin).astype(BF16)
    o_ref[:, ATTN_WIDTH + KV_WIDTH:] = y[:, ATTN_WIDTH + KV_WIDTH:].astype(BF16)


def _qkv_proj(n, w, cos, sin, gq, gk, *, pos_tiles):
    t = n.shape[0]
    tm = TOKEN_TILE
    return pl.pallas_call(
        _qkv_kernel,
        grid=(t // tm,),
        in_specs=[
            pl.BlockSpec((tm, D_MODEL), lambda i: (i, 0)),
            pl.BlockSpec((D_MODEL, QKV_WIDTH), lambda i: (0, 0)),
            pl.BlockSpec((tm, HEAD_DIM), lambda i: (i % pos_tiles, 0)),
            pl.BlockSpec((tm, HEAD_DIM), lambda i: (i % pos_tiles, 0)),
            pl.BlockSpec((1, HEAD_DIM), lambda i: (0, 0)),
            pl.BlockSpec((1, HEAD_DIM), lambda i: (0, 0)),
        ],
        out_specs=pl.BlockSpec((tm, QKV_WIDTH), lambda i: (i, 0)),
        out_shape=jax.ShapeDtypeStruct((t, QKV_WIDTH), BF16),
        compiler_params=_params("parallel"),
        name="qkv_proj",
    )(n, w, cos, sin, gq, gk)


def _scaled_mm_kernel(n_ref, w_ref, s_ref, o_ref):
    y = jnp.dot(n_ref[...], w_ref[...], preferred_element_type=F32)
    o_ref[...] = (y * s_ref[...]).astype(o_ref.dtype)


def _scaled_mm(n, w, col_scale, out_dtype, name):
    t = n.shape[0]
    nn = w.shape[1]
    tm = TOKEN_TILE
    return pl.pallas_call(
        _scaled_mm_kernel,
        grid=(t // tm,),
        in_specs=[
            pl.BlockSpec((tm, D_MODEL), lambda i: (i, 0)),
            pl.BlockSpec((D_MODEL, nn), lambda i: (0, 0)),
            pl.BlockSpec((1, nn), lambda i: (0, 0)),
        ],
        out_specs=pl.BlockSpec((tm, nn), lambda i: (i, 0)),
        out_shape=jax.ShapeDtypeStruct((t, nn), out_dtype),
        compiler_params=_params("parallel"),
        name=name,
    )(n, w, col_scale)


def _attn_kernel(flag_ref, q_ref, kc_ref, kx_ref, vc_ref, vx_ref, o_ref,
                 k_sc, v_sc, qs_sc, acc_sc, m_sc):
    tq = q_ref.shape[1]
    tk = ATTN_KV_TILE
    n_ctx = kc_ref.shape[1]
    lk = k_sc.shape[0]
    nk = lk // tk

    @pl.when(pl.program_id(2) == 0)
    def _():
        k_sc[:n_ctx, :] = kc_ref[0]
        k_sc[n_ctx:, :] = kx_ref[0]
        v_sc[:n_ctx, :HEAD_DIM] = vc_ref[0]
        v_sc[n_ctx:, :HEAD_DIM] = vx_ref[0]
        lane = lax.broadcasted_iota(jnp.int32, (n_ctx, HEAD_DIM), 1)
        ones_col = jnp.where(lane == 0, 1.0, 0.0).astype(BF16)
        for r in range(lk // n_ctx):
            v_sc[r * n_ctx:(r + 1) * n_ctx, HEAD_DIM:] = ones_col

    for g in range(ATTN_GROUPS):
        qs_sc[g * tq:(g + 1) * tq, :] = q_ref[0, :, g * HEAD_DIM:(g + 1) * HEAD_DIM]

    def scores(c):
        start = pl.multiple_of(c * tk, tk)
        k = k_sc[pl.ds(start, tk), :]
        s = lax.dot_general(qs_sc[...], k, (((1,), (1,)), ((), ())), preferred_element_type=F32)
        return s, v_sc[pl.ds(start, tk), :]

    def finish():
        acc = acc_sc[...]
        out = acc[:, :HEAD_DIM] / acc[:, HEAD_DIM:HEAD_DIM + 1]
        for g in range(ATTN_GROUPS):
            o_ref[0, :, g * HEAD_DIM:(g + 1) * HEAD_DIM] = out[g * tq:(g + 1) * tq, :].astype(BF16)

    @pl.when(flag_ref[0] != 0)
    def _():
        acc_sc[...] = jnp.zeros_like(acc_sc)

        def body(c, carry):
            s, v = scores(c)
            acc_sc[...] += jnp.dot(jnp.exp2(s).astype(BF16), v, preferred_element_type=F32)
            return carry

        lax.fori_loop(0, nk, body, 0, unroll=True)
        finish()

    @pl.when(flag_ref[0] == 0)
    def _():
        acc_sc[...] = jnp.zeros_like(acc_sc)
        m_sc[...] = jnp.full_like(m_sc, -jnp.inf)

        def body(c, carry):
            s, v = scores(c)
            m_prev = m_sc[...]
            m_new = jnp.maximum(m_prev, jnp.max(s, axis=-1, keepdims=True))
            p = jnp.exp2(s - m_new).astype(BF16)
            acc_sc[...] = jnp.exp2(m_prev - m_new) * acc_sc[...] + jnp.dot(p, v, preferred_element_type=F32)
            m_sc[...] = m_new
            return carry

        lax.fori_loop(0, nk, body, 0)
        finish()


def _attention(bounded_flag, qkv_x, qkv_c):
    b, l, _ = qkv_x.shape
    lc = qkv_c.shape[1]
    lk = lc + l
    tq = ATTN_Q_TILE
    gw = ATTN_GROUPS * HEAD_DIM
    rows = ATTN_GROUPS * tq
    k_col = ATTN_WIDTH // HEAD_DIM
    v_col = k_col + ATTN_KV_HEADS
    lat = lambda col: pl.BlockSpec((1, l, HEAD_DIM), lambda bi, hi, qi: (bi, 0, hi + col))
    ctx = lambda col: pl.BlockSpec((1, lc, HEAD_DIM), lambda bi, hi, qi: (bi, 0, hi + col))
    return pl.pallas_call(
        _attn_kernel,
        grid=(b, ATTN_KV_HEADS, l // tq),
        in_specs=[
            pl.BlockSpec(memory_space=pltpu.SMEM),
            pl.BlockSpec((1, tq, gw), lambda bi, hi, qi: (bi, qi, hi)),
            ctx(k_col), lat(k_col), ctx(v_col), lat(v_col),
        ],
        out_specs=pl.BlockSpec((1, tq, gw), lambda bi, hi, qi: (bi, qi, hi)),
        out_shape=jax.ShapeDtypeStruct((b, l, ATTN_WIDTH), BF16),
        scratch_shapes=[
            pltpu.VMEM((lk, HEAD_DIM), BF16),
            pltpu.VMEM((lk, 2 * HEAD_DIM), BF16),
            pltpu.VMEM((rows, HEAD_DIM), BF16),
            pltpu.VMEM((rows, 2 * HEAD_DIM), F32),
            pltpu.VMEM((rows, 1), F32),
        ],
        compiler_params=_params("parallel", "parallel", "arbitrary"),
        name="attention",
    )(bounded_flag, qkv_x, qkv_c, qkv_x, qkv_c, qkv_x)


def _ret_kernel(lg_ref, q_ref, k_ref, v_ref, g_ref, kc_ref, vc_ref, o_ref, uf_sc, sb_sc):
    c_len = RET_CHUNK
    n_chunks = q_ref.shape[1] // c_len
    n_ctx = kc_ref.shape[1]
    head = pl.program_id(1)
    lgf = lg_ref[0, head]
    lgb = lg_ref[1, head]

    row = lax.broadcasted_iota(jnp.int32, (c_len, 1), 0).astype(F32)
    vdec_f = jnp.exp((c_len - 1.0 - row) * lgf)
    vdec_b = jnp.exp(row * lgb)
    qdec_f = jnp.exp((row + 1.0) * lgf)
    qdec_b = jnp.exp((c_len - row) * lgb)
    chunk_f = jnp.exp(jnp.full((1, RET_DIM), c_len, F32) * lgf)
    chunk_b = jnp.exp(jnp.full((1, RET_DIM), c_len, F32) * lgb)
    diff = (lax.broadcasted_iota(jnp.int32, (c_len, c_len), 0)
            - lax.broadcasted_iota(jnp.int32, (c_len, c_len), 1)).astype(F32)
    decay = (jnp.where(diff >= 0, jnp.exp(jnp.maximum(diff, 0.0) * lgf), 0.0)
             + jnp.where(diff <= 0, jnp.exp(jnp.maximum(-diff, 0.0) * lgb), 0.0))

    def kv_outer(k, v, dec_f, dec_b):
        vf = v.astype(F32)
        v2 = jnp.concatenate([(vf * dec_f).astype(BF16), (vf * dec_b).astype(BF16)], axis=1)
        return lax.dot_general(k, v2, (((0,), (0,)), ((), ())), preferred_element_type=F32)

    crow = lax.broadcasted_iota(jnp.int32, (n_ctx, 1), 0).astype(F32)
    s0 = kv_outer(kc_ref[0], vc_ref[0], jnp.exp((n_ctx - 1.0 - crow) * lgf), jnp.exp(crow * lgb))

    def back_body(t, sb):
        c = n_chunks - 1 - t
        start = pl.multiple_of(c * c_len, c_len)
        sb_sc[c] = sb
        u = kv_outer(k_ref[0, pl.ds(start, c_len), :], v_ref[0, pl.ds(start, c_len), :], vdec_f, vdec_b)
        uf_sc[c] = u[:, :RET_DIM]
        return sb * chunk_b + u[:, RET_DIM:]

    lax.fori_loop(0, n_chunks, back_body, s0[:, RET_DIM:], unroll=RET_UNROLL)

    def fwd_body(c, sf):
        start = pl.multiple_of(c * c_len, c_len)
        q = q_ref[0, pl.ds(start, c_len), :]
        k = k_ref[0, pl.ds(start, c_len), :]
        v = v_ref[0, pl.ds(start, c_len), :]
        inner = lax.dot_general(q, k, (((1,), (1,)), ((), ())), preferred_element_type=F32)
        y = jnp.dot((inner * decay).astype(BF16), v, preferred_element_type=F32)
        states = jnp.concatenate([sf.astype(BF16), sb_sc[c].astype(BF16)], axis=1)
        cross = jnp.dot(q, states, preferred_element_type=F32)
        y = y + cross[:, :RET_DIM] * qdec_f + cross[:, RET_DIM:] * qdec_b
        gate = g_ref[0, pl.ds(start, c_len), :]
        o_ref[0, pl.ds(start, c_len), :] = (gate * _sigmoid(gate) * _rms(y)).astype(BF16)
        return sf * chunk_f + uf_sc[c]

    lax.fori_loop(0, n_chunks, fwd_body, s0[:, :RET_DIM], unroll=RET_UNROLL)


def _retention(log_gamma, ret_x, gate_x, ret_c):
    b, l, _ = ret_x.shape
    lc = ret_c.shape[1]
    n_chunks = l // RET_CHUNK
    seq = lambda off: pl.BlockSpec((1, l, RET_DIM), lambda bi, hi: (bi, 0, hi + off))
    ctx = lambda off: pl.BlockSpec((1, lc, RET_DIM), lambda bi, hi: (bi, 0, hi + off))
    return pl.pallas_call(
        _ret_kernel,
        grid=(b, RET_HEADS),
        in_specs=[
            pl.BlockSpec(memory_space=pltpu.SMEM),
            seq(0), seq(RET_HEADS), seq(2 * RET_HEADS), seq(0),
            ctx(RET_HEADS), ctx(2 * RET_HEADS),
        ],
        out_specs=seq(0),
        out_shape=jax.ShapeDtypeStruct((b, l, RET_WIDTH), BF16),
        scratch_shapes=[
            pltpu.VMEM((n_chunks, RET_DIM, RET_DIM), F32),
            pltpu.VMEM((n_chunks, RET_DIM, RET_DIM), F32),
        ],
        compiler_params=_params("parallel", "arbitrary"),
        name="retention",
    )(log_gamma, ret_x, ret_x, ret_x, gate_x, ret_c, ret_c)


def _merge_kernel(h_ref, n_ref, ya_ref, yr_ref, mod_ref, wga_ref, wgb_ref, wpa_ref, wpr_ref, wo_ref,
                  out_ref, acc_sc):
    j = pl.program_id(1)

    @pl.when(j == 0)
    def _():
        acc_sc[...] = jnp.zeros_like(acc_sc)

    n = n_ref[...]
    ga = jnp.dot(n, wga_ref[...], preferred_element_type=F32)
    gb = jnp.dot(n, wgb_ref[...], preferred_element_type=F32)
    pa = jnp.dot(ya_ref[...], wpa_ref[...], preferred_element_type=F32)
    pr = jnp.dot(yr_ref[...], wpr_ref[...], preferred_element_type=F32)
    z = (_sigmoid(ga) * pa + _sigmoid(gb) * pr).astype(BF16)
    acc_sc[...] += jnp.dot(z, wo_ref[...], preferred_element_type=F32)

    @pl.when(j == pl.num_programs(1) - 1)
    def _():
        out_ref[...] = h_ref[...] + mod_ref[0, 5, 0:1, :] * acc_sc[...]


def _merge(h, n, ya, yr, mod, w_ga, w_gb, w_pa, w_pr, w_out, *, tiles_per_row):
    t = h.shape[0]
    tm, tc = TOKEN_TILE, MERGE_TILE
    return pl.pallas_call(
        _merge_kernel,
        grid=(t // tm, D_MODEL // tc),
        in_specs=[
            pl.BlockSpec((tm, D_MODEL), lambda i, j: (i, 0)),
            pl.BlockSpec((tm, D_MODEL), lambda i, j: (i, 0)),
            pl.BlockSpec((tm, ATTN_WIDTH), lambda i, j: (i, 0)),
            pl.BlockSpec((tm, RET_WIDTH), lambda i, j: (i, 0)),
            pl.BlockSpec((1, N_MOD, ROW_BLOCK, D_MODEL), lambda i, j: (i // tiles_per_row, 0, 0, 0)),
            pl.BlockSpec((D_MODEL, tc), lambda i, j: (0, j)),
            pl.BlockSpec((D_MODEL, tc), lambda i, j: (0, j)),
            pl.BlockSpec((ATTN_WIDTH, tc), lambda i, j: (0, j)),
            pl.BlockSpec((RET_WIDTH, tc), lambda i, j: (0, j)),
            pl.BlockSpec((tc, D_MODEL), lambda i, j: (j, 0)),
        ],
        out_specs=pl.BlockSpec((tm, D_MODEL), lambda i, j: (i, 0)),
        out_shape=jax.ShapeDtypeStruct((t, D_MODEL), F32),
        scratch_shapes=[pltpu.VMEM((tm, D_MODEL), F32)],
        compiler_params=_params("parallel", "arbitrary"),
        name="merge",
    )(h, n, ya, yr, mod, w_ga, w_gb, w_pa, w_pr, w_out)


def _rope_tables(seq_len):
    rows = seq_len // GRID_W
    row = jnp.repeat(jnp.arange(rows, dtype=F32), GRID_W)
    col = jnp.tile(jnp.arange(GRID_W, dtype=F32), rows)
    half = HEAD_DIM // 2
    inv_freq = ROPE_THETA ** (-jnp.arange(0, half, 2, dtype=F32) / half)
    ang = jnp.concatenate([row[:, None] * inv_freq, col[:, None] * inv_freq], axis=-1)
    cos, sin = jnp.cos(ang), jnp.sin(ang)
    return jnp.concatenate([cos, cos], axis=-1), jnp.concatenate([-sin, sin], axis=-1)


def _deinterleave(t):
    lead = t.shape[:-1]
    t = t.reshape(lead + (-1, HEAD_DIM // 2, 2))
    return jnp.swapaxes(t, -1, -2).reshape(lead + (-1,))


def kernel(x, c, ctx, c_ctx, w_ada, b_ada, ffn1_w_in, ffn1_w_out, mix_w_in, attn_q_gain, attn_k_gain,
           ret_decay_logit, w_proj_attn, w_proj_ret, mix_w_out, ffn2_w_in, ffn2_w_out, final_norm):
    batch, seq_len, _ = x.shape
    assert w_ada.shape[0] == 1, "single-layer block"
    assert seq_len % FFN2_TOKEN_TILE == 0 and (batch * CTX_LEN) % TOKEN_TILE == 0
    tiles_per_row = seq_len // TOKEN_TILE

    w1_in, w1_out = ffn1_w_in[0].astype(BF16), ffn1_w_out[0].astype(BF16)
    w2_in, w2_out = ffn2_w_in[0].astype(BF16), ffn2_w_out[0].astype(BF16)
    w_mix = mix_w_in[0]
    w_qkv = jnp.concatenate([_deinterleave(w_mix[:, :ATTN_WIDTH + KV_WIDTH]),
                             w_mix[:, ATTN_WIDTH + KV_WIDTH:QKV_WIDTH]], axis=1).astype(BF16)
    w_ret = w_mix[:, RET_OFF:GR_OFF].astype(BF16)
    w_gr = w_mix[:, GR_OFF:GA_OFF].astype(BF16)
    w_ga = w_mix[:, GA_OFF:GB_OFF].astype(BF16)
    w_gb = w_mix[:, GB_OFF:].astype(BF16)
    w_pa, w_pr = w_proj_attn[0].astype(BF16), w_proj_ret[0].astype(BF16)
    w_mo = mix_w_out[0].astype(BF16)

    cond = jnp.zeros((8, D_MODEL), F32).at[:batch].set(c).at[batch].set(c_ctx)
    mod = _adaln(cond, w_ada[0], b_ada).reshape(8, N_MOD, 1, D_MODEL)
    mod = jnp.broadcast_to(mod, (8, N_MOD, ROW_BLOCK, D_MODEL))

    x2 = x.reshape(batch * seq_len, D_MODEL)
    c2 = ctx.reshape(batch * CTX_LEN, D_MODEL)
    h1, n2 = _ffn(x2, mod, w1_in, w1_out, mod_base=0, rows_per_mod=seq_len, row_offset=0,
                  tm=TOKEN_TILE, tf=FF_TILE, emit_next=True)
    _, n2c = _ffn(c2, mod, w1_in, w1_out, mod_base=0, rows_per_mod=batch * CTX_LEN, row_offset=batch,
                  tm=TOKEN_TILE, tf=FF_TILE, emit_next=True)

    cos, sin = _rope_tables(seq_len)
    gq = _deinterleave(attn_q_gain[0].reshape(1, HEAD_DIM))
    gk = _deinterleave(attn_k_gain[0].reshape(1, HEAD_DIM))
    qkv_x = _qkv_proj(n2, w_qkv, cos, sin, gq, gk, pos_tiles=tiles_per_row)
    ones = jnp.ones((TOKEN_TILE, HEAD_DIM), F32)
    qkv_c = _qkv_proj(n2c, w_qkv, ones, jnp.zeros_like(ones), gq, gk, pos_tiles=1)

    ret_scale = jnp.concatenate([jnp.ones((1, RET_WIDTH), F32),
                                 jnp.full((1, RET_WIDTH), RET_DIM ** -0.5, F32),
                                 jnp.ones((1, RET_WIDTH), F32)], axis=1)
    ret_x = _scaled_mm(n2, w_ret, ret_scale, BF16, "ret_proj")
    ret_c = _scaled_mm(n2c, w_ret, ret_scale, BF16, "ret_proj")
    gate_x = _scaled_mm(n2, w_gr, jnp.ones((1, RET_WIDTH), F32), F32, "ret_gate_proj")

    qkv_x = qkv_x.reshape(batch, seq_len, QKV_WIDTH)
    qkv_c = qkv_c.reshape(batch, CTX_LEN, QKV_WIDTH)
    score_bound = HEAD_DIM ** 0.5 * jnp.max(jnp.abs(gq)) * jnp.max(jnp.abs(gk))
    ya = _attention((score_bound <= ATTN_SAFE_SCORE).astype(jnp.int32).reshape(1), qkv_x, qkv_c)

    log_gamma = jax.nn.log_sigmoid(ret_decay_logit[0].astype(F32))
    yr = _retention(log_gamma, ret_x.reshape(batch, seq_len, 3 * RET_WIDTH),
                    gate_x.reshape(batch, seq_len, RET_WIDTH),
                    ret_c.reshape(batch, CTX_LEN, 3 * RET_WIDTH))

    h2 = _merge(h1, n2, ya.reshape(batch * seq_len, ATTN_WIDTH), yr.reshape(batch * seq_len, RET_WIDTH), mod,
                w_ga, w_gb, w_pa, w_pr, w_mo, tiles_per_row=tiles_per_row)
    out = _ffn(h2, mod, w2_in, w2_out, mod_base=6, rows_per_mod=seq_len, row_offset=0,
               tm=FFN2_TOKEN_TILE, tf=FFN2_FF_TILE,
               final_norm=jnp.broadcast_to(final_norm.reshape(1, D_MODEL), (ROW_BLOCK, D_MODEL)))
    return out[0].reshape(batch, seq_len, D_MODEL)
```

```python
import functools

import jax
import jax.numpy as jnp
from jax import lax
from jax.experimental import pallas as pl
from jax.experimental.pallas import tpu as pltpu

D_MODEL = 2048
CTX_LEN = 256
GRID_W = 64
HEAD_DIM = 128
LANES = 128
ATTN_Q_HEADS = 8
ATTN_KV_HEADS = 2
ATTN_GROUPS = ATTN_Q_HEADS // ATTN_KV_HEADS
ATTN_WIDTH = ATTN_Q_HEADS * HEAD_DIM
KV_WIDTH = ATTN_KV_HEADS * HEAD_DIM
RET_HEADS = 8
RET_DIM = 128
RET_WIDTH = RET_HEADS * RET_DIM
D_FF = 5632
ROPE_THETA = 10000.0
NORM_EPS = 1e-6
N_MOD = 9

QKV_WIDTH = ATTN_WIDTH + 2 * KV_WIDTH
RET_OFF = QKV_WIDTH
GR_OFF = RET_OFF + 3 * RET_WIDTH
GA_OFF = GR_OFF + RET_WIDTH
GB_OFF = GA_OFF + D_MODEL

F32 = jnp.float32
BF16 = jnp.bfloat16

VMEM_LIMIT_BYTES = 60 * 1024 * 1024

ADALN_TILE = 1024
TOKEN_TILE = 512
FF_TILE = 512
ROW_BLOCK = 16
ROW_UNROLL = 4
MERGE_TILE = 512
ATTN_Q_TILE = 256
ATTN_KV_TILE = 768
RET_CHUNK = 256
RET_UNROLL = 8

Q_SCALE = HEAD_DIM ** -0.5 * 1.4426950408889634
ATTN_SAFE_SCORE = 32.0


def _params(*sem):
    return pltpu.CompilerParams(dimension_semantics=sem, vmem_limit_bytes=VMEM_LIMIT_BYTES)


def _rms(x):
    return x * lax.rsqrt(jnp.mean(x * x, axis=-1, keepdims=True) + NORM_EPS)


def _sigmoid(x):
    return 1.0 / (1.0 + jnp.exp(-x))


def _adaln_kernel(c_ref, w_ref, b_ref, o_ref):
    chunk = pl.program_id(0) // (D_MODEL // ADALN_TILE)
    c = c_ref[...]
    s = (c * _sigmoid(c)).astype(BF16)
    y = jnp.dot(s, w_ref[...].astype(BF16), preferred_element_type=F32) + b_ref[...]
    y = y + jnp.where(chunk % 3 == 1, 1.0, 0.0)
    o_ref[...] = y * jnp.where((chunk == 2) | (chunk == 8), 0.5, 1.0)


def _adaln(cond, w, b):
    n = w.shape[1]
    tn = ADALN_TILE
    return pl.pallas_call(
        _adaln_kernel,
        grid=(n // tn,),
        in_specs=[
            pl.BlockSpec((8, D_MODEL), lambda j: (0, 0)),
            pl.BlockSpec((D_MODEL, tn), lambda j: (0, j)),
            pl.BlockSpec((1, tn), lambda j: (0, j)),
        ],
        out_specs=pl.BlockSpec((8, tn), lambda j: (0, j)),
        out_shape=jax.ShapeDtypeStruct((8, n), F32),
        compiler_params=_params("arbitrary"),
        name="adaln",
    )(cond, w, b)


def _ffn_kernel(*refs, mod_base, emit_h, emit_next, final):
    h_ref, mod_ref, wa_ref, wb_ref, wo_ref = refs[:5]
    refs = refs[5:]
    if final:
        fn_ref, refs = refs[0], refs[1:]
    if emit_h:
        out_ref, refs = refs[0], refs[1:]
    if emit_next:
        nxt_ref, refs = refs[0], refs[1:]
    xn_sc, acc_sc, inv_sc = refs
    h_dst = out_ref if emit_h else acc_sc
    j = pl.program_id(1)
    n_row_blocks = h_ref.shape[0] // ROW_BLOCK

    lane_tiles = [slice(k, k + LANES) for k in range(0, D_MODEL, LANES)]

    def row_block(r):
        return pl.ds(pl.multiple_of(r * ROW_BLOCK, ROW_BLOCK), ROW_BLOCK)

    def inv_rms(x):
        inv = lax.rsqrt(jnp.mean(x * x, axis=-1, keepdims=True) + NORM_EPS)
        return jnp.broadcast_to(inv, (x.shape[0], LANES))

    def for_row_blocks(body, unroll):
        def step(r, carry):
            body(row_block(r))
            return carry

        lax.fori_loop(0, n_row_blocks, step, 0, unroll=unroll)

    @pl.when(j == 0)
    def _():
        def stats(rows):
            inv_sc[rows, :] = inv_rms(h_ref[rows, :])

        def prenorm(rows):
            inv = inv_sc[rows, :]
            for sl in lane_tiles:
                n = h_ref[rows, sl] * inv * mod_ref[0, mod_base + 1, :, sl] + mod_ref[0, mod_base, :, sl]
                xn_sc[rows, sl] = n.astype(BF16)

        for_row_blocks(stats, True)
        for_row_blocks(prenorm, ROW_UNROLL)
        acc_sc[...] = jnp.zeros_like(acc_sc)

    xn = xn_sc[...]
    a = jnp.dot(xn, wa_ref[...], preferred_element_type=F32)
    b = jnp.dot(xn, wb_ref[...], preferred_element_type=F32)
    act = (a * _sigmoid(a) * b).astype(BF16)
    acc_sc[...] += jnp.dot(act, wo_ref[...], preferred_element_type=F32)

    @pl.when(j == pl.num_programs(1) - 1)
    def _():
        def residual(rows):
            h = h_ref[rows, :] + mod_ref[0, mod_base + 2] * acc_sc[rows, :]
            h_dst[rows, :] = h
            inv_sc[rows, :] = inv_rms(h)

        def postnorm(rows):
            inv = inv_sc[rows, :]
            for sl in lane_tiles:
                n = h_dst[rows, sl] * inv
                if final:
                    out_ref[rows, sl] = n * fn_ref[:, sl]
                else:
                    n = n * mod_ref[0, mod_base + 4, :, sl] + mod_ref[0, mod_base + 3, :, sl]
                    nxt_ref[rows, sl] = n.astype(BF16)

        for_row_blocks(residual, True)
        if final or emit_next:
            for_row_blocks(postnorm, ROW_UNROLL)


def _ffn(h, mod, w_in, w_out, *, mod_base, rows_per_mod, row_offset, tm, tf, emit_h=True, emit_next=False,
         final_norm=None):
    t = h.shape[0]
    nf = D_FF // tf
    final = final_norm is not None
    tiles_per_row = rows_per_mod // tm
    mod_map = lambda i, j: (i // tiles_per_row + row_offset, 0, 0, 0)
    in_specs = [
        pl.BlockSpec((tm, D_MODEL), lambda i, j: (i, 0)),
        pl.BlockSpec((1, N_MOD, ROW_BLOCK, D_MODEL), mod_map),
        pl.BlockSpec((D_MODEL, tf), lambda i, j: (0, j)),
        pl.BlockSpec((D_MODEL, tf), lambda i, j: (0, j + nf)),
        pl.BlockSpec((tf, D_MODEL), lambda i, j: (j, 0)),
    ]
    args = [h, mod, w_in, w_in, w_out]
    if final:
        in_specs.append(pl.BlockSpec((ROW_BLOCK, D_MODEL), lambda i, j: (0, 0)))
        args.append(final_norm)
    out_specs, out_shape = [], []
    if emit_h:
        out_specs.append(pl.BlockSpec((tm, D_MODEL), lambda i, j: (i, 0)))
        out_shape.append(jax.ShapeDtypeStruct((t, D_MODEL), F32))
    if emit_next:
        out_specs.append(pl.BlockSpec((tm, D_MODEL), lambda i, j: (i, 0)))
        out_shape.append(jax.ShapeDtypeStruct((t, D_MODEL), BF16))
    return pl.pallas_call(
        functools.partial(_ffn_kernel, mod_base=mod_base, emit_h=emit_h, emit_next=emit_next, final=final),
        grid=(t // tm, nf),
        in_specs=in_specs,
        out_specs=out_specs,
        out_shape=out_shape,
        scratch_shapes=[pltpu.VMEM((tm, D_MODEL), BF16), pltpu.VMEM((tm, D_MODEL), F32),
                        pltpu.VMEM((tm, LANES), F32)],
        compiler_params=_params("parallel", "arbitrary"),
        name="ffn_final" if final else "ffn",
    )(*args)


def _qkv_kernel(n_ref, w_ref, cos_ref, sin_ref, gq_ref, gk_ref, o_ref, *y_bufs, n_tiles):
    i = pl.program_id(0)

    def finish(y_ref):
        cos = cos_ref[...]
        sin = sin_ref[...]
        gq = gq_ref[...] * Q_SCALE
        gk = gk_ref[...]
        for hh in range(ATTN_Q_HEADS + ATTN_KV_HEADS):
            sl = slice(hh * HEAD_DIM, (hh + 1) * HEAD_DIM)
            t = _rms(y_ref[:, sl]) * (gq if hh < ATTN_Q_HEADS else gk)
            o_ref[:, sl] = (t * cos + pltpu.roll(t, HEAD_DIM // 2, 1) * sin).astype(BF16)
        o_ref[:, ATTN_WIDTH + KV_WIDTH:] = y_ref[:, ATTN_WIDTH + KV_WIDTH:].astype(BF16)

    @pl.when(i == 0)
    def _():
        y_bufs[1][...] = jnp.zeros_like(y_bufs[1])

    for parity in range(2):
        @pl.when((i < n_tiles) & (i % 2 == parity))
        def _(y_cur=y_bufs[parity], y_prev=y_bufs[1 - parity]):
            finish(y_prev)
            y_cur[...] = jnp.dot(n_ref[...], w_ref[...], preferred_element_type=F32)

    @pl.when(i == n_tiles)
    def _():
        finish(y_bufs[1 - n_tiles % 2])


def _qkv_proj(n, w, cos, sin, gq, gk, *, pos_tiles):
    t = n.shape[0]
    tm = TOKEN_TILE
    n_tiles = t // tm
    prev_tile = lambda i: jnp.maximum(i - 1, 0)
    return pl.pallas_call(
        functools.partial(_qkv_kernel, n_tiles=n_tiles),
        grid=(n_tiles + 1,),
        in_specs=[
            pl.BlockSpec((tm, D_MODEL), lambda i: (jnp.minimum(i, n_tiles - 1), 0)),
            pl.BlockSpec((D_MODEL, QKV_WIDTH), lambda i: (0, 0)),
            pl.BlockSpec((tm, HEAD_DIM), lambda i: (prev_tile(i) % pos_tiles, 0)),
            pl.BlockSpec((tm, HEAD_DIM), lambda i: (prev_tile(i) % pos_tiles, 0)),
            pl.BlockSpec((1, HEAD_DIM), lambda i: (0, 0)),
            pl.BlockSpec((1, HEAD_DIM), lambda i: (0, 0)),
        ],
        out_specs=pl.BlockSpec((tm, QKV_WIDTH), lambda i: (prev_tile(i), 0)),
        out_shape=jax.ShapeDtypeStruct((t, QKV_WIDTH), BF16),
        scratch_shapes=[pltpu.VMEM((tm, QKV_WIDTH), F32), pltpu.VMEM((tm, QKV_WIDTH), F32)],
        compiler_params=_params("arbitrary"),
        name="qkv_proj",
    )(n, w, cos, sin, gq, gk)


def _scaled_mm_kernel(n_ref, w_ref, s_ref, o_ref):
    y = jnp.dot(n_ref[...], w_ref[...], preferred_element_type=F32)
    o_ref[...] = (y * s_ref[...]).astype(o_ref.dtype)


def _scaled_mm(n, w, col_scale, out_dtype, name):
    t = n.shape[0]
    nn = w.shape[1]
    tm = TOKEN_TILE
    return pl.pallas_call(
        _scaled_mm_kernel,
        grid=(t // tm,),
        in_specs=[
            pl.BlockSpec((tm, D_MODEL), lambda i: (i, 0)),
            pl.BlockSpec((D_MODEL, nn), lambda i: (0, 0)),
            pl.BlockSpec((1, nn), lambda i: (0, 0)),
        ],
        out_specs=pl.BlockSpec((tm, nn), lambda i: (i, 0)),
        out_shape=jax.ShapeDtypeStruct((t, nn), out_dtype),
        compiler_params=_params("parallel"),
        name=name,
    )(n, w, col_scale)


def _attn_kernel(flag_ref, q_ref, kc_ref, kx_ref, vc_ref, vx_ref, o_ref,
                 k_sc, v_sc, qs_sc, acc_sc, m_sc):
    tq = q_ref.shape[1]
    tk = ATTN_KV_TILE
    n_ctx = kc_ref.shape[1]
    lk = k_sc.shape[0]
    nk = lk // tk

    @pl.when(pl.program_id(2) == 0)
    def _():
        k_sc[:n_ctx, :] = kc_ref[0]
        k_sc[n_ctx:, :] = kx_ref[0]
        v_sc[:n_ctx, :HEAD_DIM] = vc_ref[0]
        v_sc[n_ctx:, :HEAD_DIM] = vx_ref[0]
        lane = lax.broadcasted_iota(jnp.int32, (n_ctx, HEAD_DIM), 1)
        ones_col = jnp.where(lane == 0, 1.0, 0.0).astype(BF16)
        for r in range(lk // n_ctx):
            v_sc[r * n_ctx:(r + 1) * n_ctx, HEAD_DIM:] = ones_col

    for g in range(ATTN_GROUPS):
        qs_sc[g * tq:(g + 1) * tq, :] = q_ref[0, :, g * HEAD_DIM:(g + 1) * HEAD_DIM]

    def scores(c):
        start = pl.multiple_of(c * tk, tk)
        k = k_sc[pl.ds(start, tk), :]
        s = lax.dot_general(qs_sc[...], k, (((1,), (1,)), ((), ())), preferred_element_type=F32)
        return s, v_sc[pl.ds(start, tk), :]

    def finish():
        acc = acc_sc[...]
        out = acc[:, :HEAD_DIM] / acc[:, HEAD_DIM:HEAD_DIM + 1]
        for g in range(ATTN_GROUPS):
            o_ref[0, :, g * HEAD_DIM:(g + 1) * HEAD_DIM] = out[g * tq:(g + 1) * tq, :].astype(BF16)

    @pl.when(flag_ref[0] != 0)
    def _():
        acc_sc[...] = jnp.zeros_like(acc_sc)

        def body(c, carry):
            s, v = scores(c)
            acc_sc[...] += jnp.dot(jnp.exp2(s).astype(BF16), v, preferred_element_type=F32)
            return carry

        lax.fori_loop(0, nk, body, 0, unroll=True)
        finish()

    @pl.when(flag_ref[0] == 0)
    def _():
        acc_sc[...] = jnp.zeros_like(acc_sc)
        m_sc[...] = jnp.full_like(m_sc, -jnp.inf)

        def body(c, carry):
            s, v = scores(c)
            m_prev = m_sc[...]
            m_new = jnp.maximum(m_prev, jnp.max(s, axis=-1, keepdims=True))
            p = jnp.exp2(s - m_new).astype(BF16)
            acc_sc[...] = jnp.exp2(m_prev - m_new) * acc_sc[...] + jnp.dot(p, v, preferred_element_type=F32)
            m_sc[...] = m_new
            return carry

        lax.fori_loop(0, nk, body, 0)
        finish()


def _attention(bounded_flag, qkv_x, qkv_c):
    b, l, _ = qkv_x.shape
    lc = qkv_c.shape[1]
    lk = lc + l
    tq = ATTN_Q_TILE
    gw = ATTN_GROUPS * HEAD_DIM
    rows = ATTN_GROUPS * tq
    k_col = ATTN_WIDTH // HEAD_DIM
    v_col = k_col + ATTN_KV_HEADS
    lat = lambda col: pl.BlockSpec((1, l, HEAD_DIM), lambda bi, hi, qi: (bi, 0, hi + col))
    ctx = lambda col: pl.BlockSpec((1, lc, HEAD_DIM), lambda bi, hi, qi: (bi, 0, hi + col))
    return pl.pallas_call(
        _attn_kernel,
        grid=(b, ATTN_KV_HEADS, l // tq),
        in_specs=[
            pl.BlockSpec(memory_space=pltpu.SMEM),
            pl.BlockSpec((1, tq, gw), lambda bi, hi, qi: (bi, qi, hi)),
            ctx(k_col), lat(k_col), ctx(v_col), lat(v_col),
        ],
        out_specs=pl.BlockSpec((1, tq, gw), lambda bi, hi, qi: (bi, qi, hi)),
        out_shape=jax.ShapeDtypeStruct((b, l, ATTN_WIDTH), BF16),
        scratch_shapes=[
            pltpu.VMEM((lk, HEAD_DIM), BF16),
            pltpu.VMEM((lk, 2 * HEAD_DIM), BF16),
            pltpu.VMEM((rows, HEAD_DIM), BF16),
            pltpu.VMEM((rows, 2 * HEAD_DIM), F32),
            pltpu.VMEM((rows, 1), F32),
        ],
        compiler_params=_params("parallel", "parallel", "arbitrary"),
        name="attention",
    )(bounded_flag, qkv_x, qkv_c, qkv_x, qkv_c, qkv_x)


def _ret_kernel(lg_ref, q_ref, k_ref, v_ref, g_ref, kc_ref, vc_ref, o_ref, uf_sc, sb_sc):
    c_len = RET_CHUNK
    n_chunks = q_ref.shape[1] // c_len
    n_ctx = kc_ref.shape[1]
    head = pl.program_id(1)
    lgf = lg_ref[0, head]
    lgb = lg_ref[1, head]

    row = lax.broadcasted_iota(jnp.int32, (c_len, 1), 0).astype(F32)
    vdec_f = jnp.exp((c_len - 1.0 - row) * lgf)
    vdec_b = jnp.exp(row * lgb)
    qdec_f = jnp.exp((row + 1.0) * lgf)
    qdec_b = jnp.exp((c_len - row) * lgb)
    chunk_f = jnp.exp(jnp.full((1, RET_DIM), c_len, F32) * lgf)
    chunk_b = jnp.exp(jnp.full((1, RET_DIM), c_len, F32) * lgb)
    diff = (lax.broadcasted_iota(jnp.int32, (c_len, c_len), 0)
            - lax.broadcasted_iota(jnp.int32, (c_len, c_len), 1)).astype(F32)
    decay = (jnp.where(diff >= 0, jnp.exp(jnp.maximum(diff, 0.0) * lgf), 0.0)
             + jnp.where(diff <= 0, jnp.exp(jnp.maximum(-diff, 0.0) * lgb), 0.0))

    def kv_outer(k, v, dec_f, dec_b):
        vf = v.astype(F32)
        v2 = jnp.concatenate([(vf * dec_f).astype(BF16), (vf * dec_b).astype(BF16)], axis=1)
        return lax.dot_general(k, v2, (((0,), (0,)), ((), ())), preferred_element_type=F32)

    crow = lax.broadcasted_iota(jnp.int32, (n_ctx, 1), 0).astype(F32)
    s0 = kv_outer(kc_ref[0], vc_ref[0], jnp.exp((n_ctx - 1.0 - crow) * lgf), jnp.exp(crow * lgb))

    def back_body(t, sb):
        c = n_chunks - 1 - t
        start = pl.multiple_of(c * c_len, c_len)
        sb_sc[c] = sb
        u = kv_outer(k_ref[0, pl.ds(start, c_len), :], v_ref[0, pl.ds(start, c_len), :], vdec_f, vdec_b)
        uf_sc[c] = u[:, :RET_DIM]
        return sb * chunk_b + u[:, RET_DIM:]

    lax.fori_loop(0, n_chunks, back_body, s0[:, RET_DIM:], unroll=RET_UNROLL)

    def fwd_body(c, sf):
        start = pl.multiple_of(c * c_len, c_len)
        q = q_ref[0, pl.ds(start, c_len), :]
        k = k_ref[0, pl.ds(start, c_len), :]
        v = v_ref[0, pl.ds(start, c_len), :]
        inner = lax.dot_general(q, k, (((1,), (1,)), ((), ())), preferred_element_type=F32)
        y = jnp.dot((inner * decay).astype(BF16), v, preferred_element_type=F32)
        states = jnp.concatenate([sf.astype(BF16), sb_sc[c].astype(BF16)], axis=1)
        cross = jnp.dot(q, states, preferred_element_type=F32)
        y = y + cross[:, :RET_DIM] * qdec_f + cross[:, RET_DIM:] * qdec_b
        gate = g_ref[0, pl.ds(start, c_len), :]
        o_ref[0, pl.ds(start, c_len), :] = (gate * _sigmoid(gate) * _rms(y)).astype(BF16)
        return sf * chunk_f + uf_sc[c]

    lax.fori_loop(0, n_chunks, fwd_body, s0[:, :RET_DIM], unroll=RET_UNROLL)


def _retention(log_gamma, ret_x, gate_x, ret_c):
    b, l, _ = ret_x.shape
    lc = ret_c.shape[1]
    n_chunks = l // RET_CHUNK
    seq = lambda off: pl.BlockSpec((1, l, RET_DIM), lambda bi, hi: (bi, 0, hi + off))
    ctx = lambda off: pl.BlockSpec((1, lc, RET_DIM), lambda bi, hi: (bi, 0, hi + off))
    return pl.pallas_call(
        _ret_kernel,
        grid=(b, RET_HEADS),
        in_specs=[
            pl.BlockSpec(memory_space=pltpu.SMEM),
            seq(0), seq(RET_HEADS), seq(2 * RET_HEADS), seq(0),
            ctx(RET_HEADS), ctx(2 * RET_HEADS),
        ],
        out_specs=seq(0),
        out_shape=jax.ShapeDtypeStruct((b, l, RET_WIDTH), BF16),
        scratch_shapes=[
            pltpu.VMEM((n_chunks, RET_DIM, RET_DIM), F32),
            pltpu.VMEM((n_chunks, RET_DIM, RET_DIM), F32),
        ],
        compiler_params=_params("parallel", "arbitrary"),
        name="retention",
    )(log_gamma, ret_x, ret_x, ret_x, gate_x, ret_c, ret_c)


def _merge_kernel(h_ref, n_ref, ya_ref, yr_ref, mod_ref, wga_ref, wgb_ref, wpa_ref, wpr_ref, wo_ref,
                  out_ref, acc_sc):
    j = pl.program_id(1)

    @pl.when(j == 0)
    def _():
        acc_sc[...] = jnp.zeros_like(acc_sc)

    n = n_ref[...]
    ga = jnp.dot(n, wga_ref[...], preferred_element_type=F32)
    gb = jnp.dot(n, wgb_ref[...], preferred_element_type=F32)
    pa = jnp.dot(ya_ref[...], wpa_ref[...], preferred_element_type=F32)
    pr = jnp.dot(yr_ref[...], wpr_ref[...], preferred_element_type=F32)
    z = (_sigmoid(ga) * pa + _sigmoid(gb) * pr).astype(BF16)
    acc_sc[...] += jnp.dot(z, wo_ref[...], preferred_element_type=F32)

    @pl.when(j == pl.num_programs(1) - 1)
    def _():
        out_ref[...] = h_ref[...] + mod_ref[0, 5, 0:1, :] * acc_sc[...]


def _merge(h, n, ya, yr, mod, w_ga, w_gb, w_pa, w_pr, w_out, *, tiles_per_row):
    t = h.shape[0]
    tm, tc = TOKEN_TILE, MERGE_TILE
    return pl.pallas_call(
        _merge_kernel,
        grid=(t // tm, D_MODEL // tc),
        in_specs=[
            pl.BlockSpec((tm, D_MODEL), lambda i, j: (i, 0)),
            pl.BlockSpec((tm, D_MODEL), lambda i, j: (i, 0)),
            pl.BlockSpec((tm, ATTN_WIDTH), lambda i, j: (i, 0)),
            pl.BlockSpec((tm, RET_WIDTH), lambda i, j: (i, 0)),
            pl.BlockSpec((1, N_MOD, ROW_BLOCK, D_MODEL), lambda i, j: (i // tiles_per_row, 0, 0, 0)),
            pl.BlockSpec((D_MODEL, tc), lambda i, j: (0, j)),
            pl.BlockSpec((D_MODEL, tc), lambda i, j: (0, j)),
            pl.BlockSpec((ATTN_WIDTH, tc), lambda i, j: (0, j)),
            pl.BlockSpec((RET_WIDTH, tc), lambda i, j: (0, j)),
            pl.BlockSpec((tc, D_MODEL), lambda i, j: (j, 0)),
        ],
        out_specs=pl.BlockSpec((tm, D_MODEL), lambda i, j: (i, 0)),
        out_shape=jax.ShapeDtypeStruct((t, D_MODEL), F32),
        scratch_shapes=[pltpu.VMEM((tm, D_MODEL), F32)],
        compiler_params=_params("parallel", "arbitrary"),
        name="merge",
    )(h, n, ya, yr, mod, w_ga, w_gb, w_pa, w_pr, w_out)


def _rope_tables(seq_len):
    rows = seq_len // GRID_W
    row = jnp.repeat(jnp.arange(rows, dtype=F32), GRID_W)
    col = jnp.tile(jnp.arange(GRID_W, dtype=F32), rows)
    half = HEAD_DIM // 2
    inv_freq = ROPE_THETA ** (-jnp.arange(0, half, 2, dtype=F32) / half)
    ang = jnp.concatenate([row[:, None] * inv_freq, col[:, None] * inv_freq], axis=-1)
    cos, sin = jnp.cos(ang), jnp.sin(ang)
    return jnp.concatenate([cos, cos], axis=-1), jnp.concatenate([-sin, sin], axis=-1)


def _deinterleave(t):
    lead = t.shape[:-1]
    t = t.reshape(lead + (-1, HEAD_DIM // 2, 2))
    return jnp.swapaxes(t, -1, -2).reshape(lead + (-1,))


def kernel(x, c, ctx, c_ctx, w_ada, b_ada, ffn1_w_in, ffn1_w_out, mix_w_in, attn_q_gain, attn_k_gain,
           ret_decay_logit, w_proj_attn, w_proj_ret, mix_w_out, ffn2_w_in, ffn2_w_out, final_norm):
    batch, seq_len, _ = x.shape
    assert w_ada.shape[0] == 1, "single-layer block"
    assert seq_len % TOKEN_TILE == 0 and (batch * CTX_LEN) % TOKEN_TILE == 0
    tiles_per_row = seq_len // TOKEN_TILE

    w1_in, w1_out = ffn1_w_in[0].astype(BF16), ffn1_w_out[0].astype(BF16)
    w2_in, w2_out = ffn2_w_in[0].astype(BF16), ffn2_w_out[0].astype(BF16)
    w_mix = mix_w_in[0]
    w_qkv = jnp.concatenate([_deinterleave(w_mix[:, :ATTN_WIDTH + KV_WIDTH]),
                             w_mix[:, ATTN_WIDTH + KV_WIDTH:QKV_WIDTH]], axis=1).astype(BF16)
    w_ret = w_mix[:, RET_OFF:GR_OFF].astype(BF16)
    w_gr = w_mix[:, GR_OFF:GA_OFF].astype(BF16)
    w_ga = w_mix[:, GA_OFF:GB_OFF].astype(BF16)
    w_gb = w_mix[:, GB_OFF:].astype(BF16)
    w_pa, w_pr = w_proj_attn[0].astype(BF16), w_proj_ret[0].astype(BF16)
    w_mo = mix_w_out[0].astype(BF16)

    cond = jnp.zeros((8, D_MODEL), F32).at[:batch].set(c).at[batch].set(c_ctx)
    mod = _adaln(cond, w_ada[0], b_ada).reshape(8, N_MOD, 1, D_MODEL)
    mod = jnp.broadcast_to(mod, (8, N_MOD, ROW_BLOCK, D_MODEL))

    x2 = x.reshape(batch * seq_len, D_MODEL)
    c2 = ctx.reshape(batch * CTX_LEN, D_MODEL)
    h1, n2 = _ffn(x2, mod, w1_in, w1_out, mod_base=0, rows_per_mod=seq_len, row_offset=0,
                  tm=TOKEN_TILE, tf=FF_TILE, emit_next=True)
    n2c, = _ffn(c2, mod, w1_in, w1_out, mod_base=0, rows_per_mod=batch * CTX_LEN, row_offset=batch,
                tm=batch * CTX_LEN, tf=FF_TILE, emit_h=False, emit_next=True)

    cos, sin = _rope_tables(seq_len)
    gq = _deinterleave(attn_q_gain[0].reshape(1, HEAD_DIM))
    gk = _deinterleave(attn_k_gain[0].reshape(1, HEAD_DIM))
    qkv_x = _qkv_proj(n2, w_qkv, cos, sin, gq, gk, pos_tiles=tiles_per_row)
    ones = jnp.ones((TOKEN_TILE, HEAD_DIM), F32)
    qkv_c = _qkv_proj(n2c, w_qkv, ones, jnp.zeros_like(ones), gq, gk, pos_tiles=1)

    ret_scale = jnp.concatenate([jnp.ones((1, RET_WIDTH), F32),
                                 jnp.full((1, RET_WIDTH), RET_DIM ** -0.5, F32),
                                 jnp.ones((1, RET_WIDTH), F32)], axis=1)
    ret_x = _scaled_mm(n2, w_ret, ret_scale, BF16, "ret_proj")
    ret_c = _scaled_mm(n2c, w_ret, ret_scale, BF16, "ret_proj")
    gate_x = _scaled_mm(n2, w_gr, jnp.ones((1, RET_WIDTH), F32), F32, "ret_gate_proj")

    qkv_x = qkv_x.reshape(batch, seq_len, QKV_WIDTH)
    qkv_c = qkv_c.reshape(batch, CTX_LEN, QKV_WIDTH)
    score_bound = HEAD_DIM ** 0.5 * jnp.max(jnp.abs(gq)) * jnp.max(jnp.abs(gk))
    ya = _attention((score_bound <= ATTN_SAFE_SCORE).astype(jnp.int32).reshape(1), qkv_x, qkv_c)

    log_gamma = jax.nn.log_sigmoid(ret_decay_logit[0].astype(F32))
    yr = _retention(log_gamma, ret_x.reshape(batch, seq_len, 3 * RET_WIDTH),
                    gate_x.reshape(batch, seq_len, RET_WIDTH),
                    ret_c.reshape(batch, CTX_LEN, 3 * RET_WIDTH))

    h2 = _merge(h1, n2, ya.reshape(batch * seq_len, ATTN_WIDTH), yr.reshape(batch * seq_len, RET_WIDTH), mod,
                w_ga, w_gb, w_pa, w_pr, w_mo, tiles_per_row=tiles_per_row)
    out = _ffn(h2, mod, w2_in, w2_out, mod_base=6, rows_per_mod=seq_len, row_offset=0,
               tm=TOKEN_TILE, tf=FF_TILE,
               final_norm=jnp.broadcast_to(final_norm.reshape(1, D_MODEL), (ROW_BLOCK, D_MODEL)))
    return out[0].reshape(batch, seq_len, D_MODEL)
```

```python
import functools

import jax
import jax.numpy as jnp
from jax import lax
from jax.experimental import pallas as pl
from jax.experimental.pallas import tpu as pltpu

D_MODEL = 2048
CTX_LEN = 256
GRID_W = 64
HEAD_DIM = 128
LANES = 128
ATTN_Q_HEADS = 8
ATTN_KV_HEADS = 2
ATTN_GROUPS = ATTN_Q_HEADS // ATTN_KV_HEADS
ATTN_WIDTH = ATTN_Q_HEADS * HEAD_DIM
KV_WIDTH = ATTN_KV_HEADS * HEAD_DIM
RET_HEADS = 8
RET_DIM = 128
RET_WIDTH = RET_HEADS * RET_DIM
D_FF = 5632
ROPE_THETA = 10000.0
NORM_EPS = 1e-6
N_MOD = 9

QKV_WIDTH = ATTN_WIDTH + 2 * KV_WIDTH
RET_OFF = QKV_WIDTH
GR_OFF = RET_OFF + 3 * RET_WIDTH
GA_OFF = GR_OFF + RET_WIDTH
GB_OFF = GA_OFF + D_MODEL

F32 = jnp.float32
BF16 = jnp.bfloat16

VMEM_LIMIT_BYTES = 60 * 1024 * 1024

ADALN_TILE = 1024
TOKEN_TILE = 512
FF_TILE = 512
ROW_BLOCK = 16
ROW_UNROLL = 4
MERGE_TILE = 512
ATTN_Q_TILE = 512
ATTN_KV_TILE = 768
RET_CHUNK = 256
RET_UNROLL = 8

Q_SCALE = HEAD_DIM ** -0.5 * 1.4426950408889634
ATTN_SAFE_SCORE = 32.0


def _params(*sem):
    return pltpu.CompilerParams(dimension_semantics=sem, vmem_limit_bytes=VMEM_LIMIT_BYTES)


def _rms(x):
    return x * lax.rsqrt(jnp.mean(x * x, axis=-1, keepdims=True) + NORM_EPS)


def _sigmoid(x):
    return 1.0 / (1.0 + jnp.exp(-x))


def _adaln_kernel(c_ref, w_ref, b_ref, o_ref):
    chunk = pl.program_id(0) // (D_MODEL // ADALN_TILE)
    c = c_ref[...]
    s = (c * _sigmoid(c)).astype(BF16)
    y = jnp.dot(s, w_ref[...].astype(BF16), preferred_element_type=F32) + b_ref[...]
    y = y + jnp.where(chunk % 3 == 1, 1.0, 0.0)
    o_ref[...] = y * jnp.where((chunk == 2) | (chunk == 8), 0.5, 1.0)


def _adaln(cond, w, b):
    n = w.shape[1]
    tn = ADALN_TILE
    return pl.pallas_call(
        _adaln_kernel,
        grid=(n // tn,),
        in_specs=[
            pl.BlockSpec((8, D_MODEL), lambda j: (0, 0)),
            pl.BlockSpec((D_MODEL, tn), lambda j: (0, j)),
            pl.BlockSpec((1, tn), lambda j: (0, j)),
        ],
        out_specs=pl.BlockSpec((8, tn), lambda j: (0, j)),
        out_shape=jax.ShapeDtypeStruct((8, n), F32),
        compiler_params=_params("arbitrary"),
        name="adaln",
    )(cond, w, b)


def _ffn_kernel(*refs, mod_base, emit_h, emit_next, final):
    h_ref, mod_ref, wa_ref, wb_ref, wo_ref = refs[:5]
    refs = refs[5:]
    if final:
        fn_ref, refs = refs[0], refs[1:]
    if emit_h:
        out_ref, refs = refs[0], refs[1:]
    if emit_next:
        nxt_ref, refs = refs[0], refs[1:]
    xn_sc, acc_sc, inv_sc = refs
    h_dst = out_ref if emit_h else acc_sc
    j = pl.program_id(1)
    n_row_blocks = h_ref.shape[0] // ROW_BLOCK

    lane_tiles = [slice(k, k + LANES) for k in range(0, D_MODEL, LANES)]

    def row_block(r):
        return pl.ds(pl.multiple_of(r * ROW_BLOCK, ROW_BLOCK), ROW_BLOCK)

    def inv_rms(x):
        inv = lax.rsqrt(jnp.mean(x * x, axis=-1, keepdims=True) + NORM_EPS)
        return jnp.broadcast_to(inv, (x.shape[0], LANES))

    def for_row_blocks(body, unroll):
        def step(r, carry):
            body(row_block(r))
            return carry

        lax.fori_loop(0, n_row_blocks, step, 0, unroll=unroll)

    @pl.when(j == 0)
    def _():
        def stats(rows):
            inv_sc[rows, :] = inv_rms(h_ref[rows, :])

        def prenorm(rows):
            inv = inv_sc[rows, :]
            for sl in lane_tiles:
                n = h_ref[rows, sl] * inv * mod_ref[0, mod_base + 1, :, sl] + mod_ref[0, mod_base, :, sl]
                xn_sc[rows, sl] = n.astype(BF16)

        for_row_blocks(stats, True)
        for_row_blocks(prenorm, ROW_UNROLL)
        acc_sc[...] = jnp.zeros_like(acc_sc)

    xn = xn_sc[...]
    half = wa_ref.shape[1] // 2
    partial = None
    for sl in (slice(0, half), slice(half, 2 * half)):
        a = jnp.dot(xn, wa_ref[:, sl], preferred_element_type=F32)
        b = jnp.dot(xn, wb_ref[:, sl], preferred_element_type=F32)
        act = (a * _sigmoid(a) * b).astype(BF16)
        p = jnp.dot(act, wo_ref[sl, :], preferred_element_type=F32)
        partial = p if partial is None else partial + p
    acc_sc[...] += partial

    @pl.when(j == pl.num_programs(1) - 1)
    def _():
        def residual(rows):
            h = h_ref[rows, :] + mod_ref[0, mod_base + 2] * acc_sc[rows, :]
            h_dst[rows, :] = h
            inv_sc[rows, :] = inv_rms(h)

        def postnorm(rows):
            inv = inv_sc[rows, :]
            for sl in lane_tiles:
                n = h_dst[rows, sl] * inv
                if final:
                    out_ref[rows, sl] = n * fn_ref[:, sl]
                else:
                    n = n * mod_ref[0, mod_base + 4, :, sl] + mod_ref[0, mod_base + 3, :, sl]
                    nxt_ref[rows, sl] = n.astype(BF16)

        for_row_blocks(residual, True)
        if final or emit_next:
            for_row_blocks(postnorm, ROW_UNROLL)


def _ffn(h, mod, w_in, w_out, *, mod_base, rows_per_mod, row_offset, tm, tf, emit_h=True, emit_next=False,
         final_norm=None):
    t = h.shape[0]
    nf = D_FF // tf
    final = final_norm is not None
    tiles_per_row = rows_per_mod // tm
    mod_map = lambda i, j: (i // tiles_per_row + row_offset, 0, 0, 0)
    in_specs = [
        pl.BlockSpec((tm, D_MODEL), lambda i, j: (i, 0)),
        pl.BlockSpec((1, N_MOD, ROW_BLOCK, D_MODEL), mod_map),
        pl.BlockSpec((D_MODEL, tf), lambda i, j: (0, j)),
        pl.BlockSpec((D_MODEL, tf), lambda i, j: (0, j + nf)),
        pl.BlockSpec((tf, D_MODEL), lambda i, j: (j, 0)),
    ]
    args = [h, mod, w_in, w_in, w_out]
    if final:
        in_specs.append(pl.BlockSpec((ROW_BLOCK, D_MODEL), lambda i, j: (0, 0)))
        args.append(final_norm)
    out_specs, out_shape = [], []
    if emit_h:
        out_specs.append(pl.BlockSpec((tm, D_MODEL), lambda i, j: (i, 0)))
        out_shape.append(jax.ShapeDtypeStruct((t, D_MODEL), F32))
    if emit_next:
        out_specs.append(pl.BlockSpec((tm, D_MODEL), lambda i, j: (i, 0)))
        out_shape.append(jax.ShapeDtypeStruct((t, D_MODEL), BF16))
    return pl.pallas_call(
        functools.partial(_ffn_kernel, mod_base=mod_base, emit_h=emit_h, emit_next=emit_next, final=final),
        grid=(t // tm, nf),
        in_specs=in_specs,
        out_specs=out_specs,
        out_shape=out_shape,
        scratch_shapes=[pltpu.VMEM((tm, D_MODEL), BF16), pltpu.VMEM((tm, D_MODEL), F32),
                        pltpu.VMEM((tm, LANES), F32)],
        compiler_params=_params("parallel", "arbitrary"),
        name="ffn_final" if final else "ffn",
    )(*args)


def _qkv_kernel(n_ref, w_ref, cos_ref, sin_ref, gq_ref, gk_ref, o_ref, *y_bufs, n_tiles):
    i = pl.program_id(0)

    def finish(y_ref):
        cos = cos_ref[...]
        sin = sin_ref[...]
        gq = gq_ref[...] * Q_SCALE
        gk = gk_ref[...]
        for hh in range(ATTN_Q_HEADS + ATTN_KV_HEADS):
            sl = slice(hh * HEAD_DIM, (hh + 1) * HEAD_DIM)
            t = _rms(y_ref[:, sl]) * (gq if hh < ATTN_Q_HEADS else gk)
            o_ref[:, sl] = (t * cos + pltpu.roll(t, HEAD_DIM // 2, 1) * sin).astype(BF16)
        o_ref[:, ATTN_WIDTH + KV_WIDTH:] = y_ref[:, ATTN_WIDTH + KV_WIDTH:].astype(BF16)

    @pl.when(i == 0)
    def _():
        y_bufs[1][...] = jnp.zeros_like(y_bufs[1])

    for parity in range(2):
        @pl.when((i < n_tiles) & (i % 2 == parity))
        def _(y_cur=y_bufs[parity], y_prev=y_bufs[1 - parity]):
            finish(y_prev)
            y_cur[...] = jnp.dot(n_ref[...], w_ref[...], preferred_element_type=F32)

    @pl.when(i == n_tiles)
    def _():
        finish(y_bufs[1 - n_tiles % 2])


def _qkv_proj(n, w, cos, sin, gq, gk, *, pos_tiles):
    t = n.shape[0]
    tm = TOKEN_TILE
    n_tiles = t // tm
    prev_tile = lambda i: jnp.maximum(i - 1, 0)
    return pl.pallas_call(
        functools.partial(_qkv_kernel, n_tiles=n_tiles),
        grid=(n_tiles + 1,),
        in_specs=[
            pl.BlockSpec((tm, D_MODEL), lambda i: (jnp.minimum(i, n_tiles - 1), 0)),
            pl.BlockSpec((D_MODEL, QKV_WIDTH), lambda i: (0, 0)),
            pl.BlockSpec((tm, HEAD_DIM), lambda i: (prev_tile(i) % pos_tiles, 0)),
            pl.BlockSpec((tm, HEAD_DIM), lambda i: (prev_tile(i) % pos_tiles, 0)),
            pl.BlockSpec((1, HEAD_DIM), lambda i: (0, 0)),
            pl.BlockSpec((1, HEAD_DIM), lambda i: (0, 0)),
        ],
        out_specs=pl.BlockSpec((tm, QKV_WIDTH), lambda i: (prev_tile(i), 0)),
        out_shape=jax.ShapeDtypeStruct((t, QKV_WIDTH), BF16),
        scratch_shapes=[pltpu.VMEM((tm, QKV_WIDTH), F32), pltpu.VMEM((tm, QKV_WIDTH), F32)],
        compiler_params=_params("arbitrary"),
        name="qkv_proj",
    )(n, w, cos, sin, gq, gk)


def _scaled_mm_kernel(n_ref, w_ref, s_ref, o_ref):
    y = jnp.dot(n_ref[...], w_ref[...], preferred_element_type=F32)
    o_ref[...] = (y * s_ref[...]).astype(o_ref.dtype)


def _scaled_mm(n, w, col_scale, out_dtype, name):
    t = n.shape[0]
    nn = w.shape[1]
    tm = TOKEN_TILE
    return pl.pallas_call(
        _scaled_mm_kernel,
        grid=(t // tm,),
        in_specs=[
            pl.BlockSpec((tm, D_MODEL), lambda i: (i, 0)),
            pl.BlockSpec((D_MODEL, nn), lambda i: (0, 0)),
            pl.BlockSpec((1, nn), lambda i: (0, 0)),
        ],
        out_specs=pl.BlockSpec((tm, nn), lambda i: (i, 0)),
        out_shape=jax.ShapeDtypeStruct((t, nn), out_dtype),
        compiler_params=_params("parallel"),
        name=name,
    )(n, w, col_scale)


def _attn_kernel(flag_ref, q_ref, kc_ref, kx_ref, vc_ref, vx_ref, o_ref,
                 k_sc, v_sc, qs_sc, acc_sc, m_sc):
    tq = q_ref.shape[1]
    tk = ATTN_KV_TILE
    n_ctx = kc_ref.shape[1]
    lk = k_sc.shape[0]
    nk = lk // tk

    @pl.when(pl.program_id(2) == 0)
    def _():
        k_sc[:n_ctx, :] = kc_ref[0]
        k_sc[n_ctx:, :] = kx_ref[0]
        v_sc[:n_ctx, :HEAD_DIM] = vc_ref[0]
        v_sc[n_ctx:, :HEAD_DIM] = vx_ref[0]
        lane = lax.broadcasted_iota(jnp.int32, (n_ctx, HEAD_DIM), 1)
        ones_col = jnp.where(lane == 0, 1.0, 0.0).astype(BF16)
        for r in range(lk // n_ctx):
            v_sc[r * n_ctx:(r + 1) * n_ctx, HEAD_DIM:] = ones_col

    for g in range(ATTN_GROUPS):
        qs_sc[g * tq:(g + 1) * tq, :] = q_ref[0, :, g * HEAD_DIM:(g + 1) * HEAD_DIM]

    def scores(c):
        start = pl.multiple_of(c * tk, tk)
        k = k_sc[pl.ds(start, tk), :]
        s = lax.dot_general(qs_sc[...], k, (((1,), (1,)), ((), ())), preferred_element_type=F32)
        return s, v_sc[pl.ds(start, tk), :]

    def finish():
        acc = acc_sc[...]
        out = acc[:, :HEAD_DIM] / acc[:, HEAD_DIM:HEAD_DIM + 1]
        for g in range(ATTN_GROUPS):
            o_ref[0, :, g * HEAD_DIM:(g + 1) * HEAD_DIM] = out[g * tq:(g + 1) * tq, :].astype(BF16)

    @pl.when(flag_ref[0] != 0)
    def _():
        acc_sc[...] = jnp.zeros_like(acc_sc)

        def body(c, carry):
            s, v = scores(c)
            acc_sc[...] += jnp.dot(jnp.exp2(s).astype(BF16), v, preferred_element_type=F32)
            return carry

        lax.fori_loop(0, nk, body, 0, unroll=True)
        finish()

    @pl.when(flag_ref[0] == 0)
    def _():
        acc_sc[...] = jnp.zeros_like(acc_sc)
        m_sc[...] = jnp.full_like(m_sc, -jnp.inf)

        def body(c, carry):
            s, v = scores(c)
            m_prev = m_sc[...]
            m_new = jnp.maximum(m_prev, jnp.max(s, axis=-1, keepdims=True))
            p = jnp.exp2(s - m_new).astype(BF16)
            acc_sc[...] = jnp.exp2(m_prev - m_new) * acc_sc[...] + jnp.dot(p, v, preferred_element_type=F32)
            m_sc[...] = m_new
            return carry

        lax.fori_loop(0, nk, body, 0)
        finish()


def _attention(bounded_flag, qkv_x, qkv_c):
    b, l, _ = qkv_x.shape
    lc = qkv_c.shape[1]
    lk = lc + l
    tq = ATTN_Q_TILE
    gw = ATTN_GROUPS * HEAD_DIM
    rows = ATTN_GROUPS * tq
    k_col = ATTN_WIDTH // HEAD_DIM
    v_col = k_col + ATTN_KV_HEADS
    lat = lambda col: pl.BlockSpec((1, l, HEAD_DIM), lambda bi, hi, qi: (bi, 0, hi + col))
    ctx = lambda col: pl.BlockSpec((1, lc, HEAD_DIM), lambda bi, hi, qi: (bi, 0, hi + col))
    return pl.pallas_call(
        _attn_kernel,
        grid=(b, ATTN_KV_HEADS, l // tq),
        in_specs=[
            pl.BlockSpec(memory_space=pltpu.SMEM),
            pl.BlockSpec((1, tq, gw), lambda bi, hi, qi: (bi, qi, hi)),
            ctx(k_col), lat(k_col), ctx(v_col), lat(v_col),
        ],
        out_specs=pl.BlockSpec((1, tq, gw), lambda bi, hi, qi: (bi, qi, hi)),
        out_shape=jax.ShapeDtypeStruct((b, l, ATTN_WIDTH), BF16),
        scratch_shapes=[
            pltpu.VMEM((lk, HEAD_DIM), BF16),
            pltpu.VMEM((lk, 2 * HEAD_DIM), BF16),
            pltpu.VMEM((rows, HEAD_DIM), BF16),
            pltpu.VMEM((rows, 2 * HEAD_DIM), F32),
            pltpu.VMEM((rows, 1), F32),
        ],
        compiler_params=_params("parallel", "parallel", "arbitrary"),
        name="attention",
    )(bounded_flag, qkv_x, qkv_c, qkv_x, qkv_c, qkv_x)


def _ret_kernel(lg_ref, q_ref, k_ref, v_ref, g_ref, kc_ref, vc_ref, o_ref, uf_sc, sb_sc):
    c_len = RET_CHUNK
    n_chunks = q_ref.shape[1] // c_len
    n_ctx = kc_ref.shape[1]
    head = pl.program_id(1)
    lgf = lg_ref[0, head]
    lgb = lg_ref[1, head]

    row = lax.broadcasted_iota(jnp.int32, (c_len, 1), 0).astype(F32)
    vdec_f = jnp.exp((c_len - 1.0 - row) * lgf)
    vdec_b = jnp.exp(row * lgb)
    qdec_f = jnp.exp((row + 1.0) * lgf)
    qdec_b = jnp.exp((c_len - row) * lgb)
    chunk_f = jnp.exp(jnp.full((1, RET_DIM), c_len, F32) * lgf)
    chunk_b = jnp.exp(jnp.full((1, RET_DIM), c_len, F32) * lgb)
    diff = (lax.broadcasted_iota(jnp.int32, (c_len, c_len), 0)
            - lax.broadcasted_iota(jnp.int32, (c_len, c_len), 1)).astype(F32)
    decay = (jnp.where(diff >= 0, jnp.exp(jnp.maximum(diff, 0.0) * lgf), 0.0)
             + jnp.where(diff <= 0, jnp.exp(jnp.maximum(-diff, 0.0) * lgb), 0.0))

    def kv_outer(k, v, dec_f, dec_b):
        vf = v.astype(F32)
        v2 = jnp.concatenate([(vf * dec_f).astype(BF16), (vf * dec_b).astype(BF16)], axis=1)
        return lax.dot_general(k, v2, (((0,), (0,)), ((), ())), preferred_element_type=F32)

    crow = lax.broadcasted_iota(jnp.int32, (n_ctx, 1), 0).astype(F32)
    s0 = kv_outer(kc_ref[0], vc_ref[0], jnp.exp((n_ctx - 1.0 - crow) * lgf), jnp.exp(crow * lgb))

    def back_body(t, sb):
        c = n_chunks - 1 - t
        start = pl.multiple_of(c * c_len, c_len)
        sb_sc[c] = sb
        u = kv_outer(k_ref[0, pl.ds(start, c_len), :], v_ref[0, pl.ds(start, c_len), :], vdec_f, vdec_b)
        uf_sc[c] = u[:, :RET_DIM]
        return sb * chunk_b + u[:, RET_DIM:]

    lax.fori_loop(0, n_chunks, back_body, s0[:, RET_DIM:], unroll=RET_UNROLL)

    def fwd_body(c, sf):
        start = pl.multiple_of(c * c_len, c_len)
        q = q_ref[0, pl.ds(start, c_len), :]
        k = k_ref[0, pl.ds(start, c_len), :]
        v = v_ref[0, pl.ds(start, c_len), :]
        inner = lax.dot_general(q, k, (((1,), (1,)), ((), ())), preferred_element_type=F32)
        y = jnp.dot((inner * decay).astype(BF16), v, preferred_element_type=F32)
        states = jnp.concatenate([sf.astype(BF16), sb_sc[c].astype(BF16)], axis=1)
        cross = jnp.dot(q, states, preferred_element_type=F32)
        y = y + cross[:, :RET_DIM] * qdec_f + cross[:, RET_DIM:] * qdec_b
        gate = g_ref[0, pl.ds(start, c_len), :]
        o_ref[0, pl.ds(start, c_len), :] = (gate * _sigmoid(gate) * _rms(y)).astype(BF16)
        return sf * chunk_f + uf_sc[c]

    lax.fori_loop(0, n_chunks, fwd_body, s0[:, :RET_DIM], unroll=RET_UNROLL)


def _retention(log_gamma, ret_x, gate_x, ret_c):
    b, l, _ = ret_x.shape
    lc = ret_c.shape[1]
    n_chunks = l // RET_CHUNK
    seq = lambda off: pl.BlockSpec((1, l, RET_DIM), lambda bi, hi: (bi, 0, hi + off))
    ctx = lambda off: pl.BlockSpec((1, lc, RET_DIM), lambda bi, hi: (bi, 0, hi + off))
    return pl.pallas_call(
        _ret_kernel,
        grid=(b, RET_HEADS),
        in_specs=[
            pl.BlockSpec(memory_space=pltpu.SMEM),
            seq(0), seq(RET_HEADS), seq(2 * RET_HEADS), seq(0),
            ctx(RET_HEADS), ctx(2 * RET_HEADS),
        ],
        out_specs=seq(0),
        out_shape=jax.ShapeDtypeStruct((b, l, RET_WIDTH), BF16),
        scratch_shapes=[
            pltpu.VMEM((n_chunks, RET_DIM, RET_DIM), F32),
            pltpu.VMEM((n_chunks, RET_DIM, RET_DIM), F32),
        ],
        compiler_params=_params("parallel", "arbitrary"),
        name="retention",
    )(log_gamma, ret_x, ret_x, ret_x, gate_x, ret_c, ret_c)


def _merge_kernel(h_ref, n_ref, ya_ref, yr_ref, mod_ref, wga_ref, wgb_ref, wpa_ref, wpr_ref, wo_ref,
                  out_ref, acc_sc):
    j = pl.program_id(1)

    @pl.when(j == 0)
    def _():
        acc_sc[...] = jnp.zeros_like(acc_sc)

    n = n_ref[...]
    ga = jnp.dot(n, wga_ref[...], preferred_element_type=F32)
    gb = jnp.dot(n, wgb_ref[...], preferred_element_type=F32)
    pa = jnp.dot(ya_ref[...], wpa_ref[...], preferred_element_type=F32)
    pr = jnp.dot(yr_ref[...], wpr_ref[...], preferred_element_type=F32)
    z = (_sigmoid(ga) * pa + _sigmoid(gb) * pr).astype(BF16)
    acc_sc[...] += jnp.dot(z, wo_ref[...], preferred_element_type=F32)

    @pl.when(j == pl.num_programs(1) - 1)
    def _():
        out_ref[...] = h_ref[...] + mod_ref[0, 5, 0:1, :] * acc_sc[...]


def _merge(h, n, ya, yr, mod, w_ga, w_gb, w_pa, w_pr, w_out, *, tiles_per_row):
    t = h.shape[0]
    tm, tc = TOKEN_TILE, MERGE_TILE
    return pl.pallas_call(
        _merge_kernel,
        grid=(t // tm, D_MODEL // tc),
        in_specs=[
            pl.BlockSpec((tm, D_MODEL), lambda i, j: (i, 0)),
            pl.BlockSpec((tm, D_MODEL), lambda i, j: (i, 0)),
            pl.BlockSpec((tm, ATTN_WIDTH), lambda i, j: (i, 0)),
            pl.BlockSpec((tm, RET_WIDTH), lambda i, j: (i, 0)),
            pl.BlockSpec((1, N_MOD, ROW_BLOCK, D_MODEL), lambda i, j: (i // tiles_per_row, 0, 0, 0)),
            pl.BlockSpec((D_MODEL, tc), lambda i, j: (0, j)),
            pl.BlockSpec((D_MODEL, tc), lambda i, j: (0, j)),
            pl.BlockSpec((ATTN_WIDTH, tc), lambda i, j: (0, j)),
            pl.BlockSpec((RET_WIDTH, tc), lambda i, j: (0, j)),
            pl.BlockSpec((tc, D_MODEL), lambda i, j: (j, 0)),
        ],
        out_specs=pl.BlockSpec((tm, D_MODEL), lambda i, j: (i, 0)),
        out_shape=jax.ShapeDtypeStruct((t, D_MODEL), F32),
        scratch_shapes=[pltpu.VMEM((tm, D_MODEL), F32)],
        compiler_params=_params("parallel", "arbitrary"),
        name="merge",
    )(h, n, ya, yr, mod, w_ga, w_gb, w_pa, w_pr, w_out)


def _rope_tables(seq_len):
    rows = seq_len // GRID_W
    row = jnp.repeat(jnp.arange(rows, dtype=F32), GRID_W)
    col = jnp.tile(jnp.arange(GRID_W, dtype=F32), rows)
    half = HEAD_DIM // 2
    inv_freq = ROPE_THETA ** (-jnp.arange(0, half, 2, dtype=F32) / half)
    ang = jnp.concatenate([row[:, None] * inv_freq, col[:, None] * inv_freq], axis=-1)
    cos, sin = jnp.cos(ang), jnp.sin(ang)
    return jnp.concatenate([cos, cos], axis=-1), jnp.concatenate([-sin, sin], axis=-1)


def _deinterleave(t):
    lead = t.shape[:-1]
    t = t.reshape(lead + (-1, HEAD_DIM // 2, 2))
    return jnp.swapaxes(t, -1, -2).reshape(lead + (-1,))


def kernel(x, c, ctx, c_ctx, w_ada, b_ada, ffn1_w_in, ffn1_w_out, mix_w_in, attn_q_gain, attn_k_gain,
           ret_decay_logit, w_proj_attn, w_proj_ret, mix_w_out, ffn2_w_in, ffn2_w_out, final_norm):
    batch, seq_len, _ = x.shape
    assert w_ada.shape[0] == 1, "single-layer block"
    assert seq_len % TOKEN_TILE == 0 and (batch * CTX_LEN) % TOKEN_TILE == 0
    tiles_per_row = seq_len // TOKEN_TILE

    w1_in, w1_out = ffn1_w_in[0].astype(BF16), ffn1_w_out[0].astype(BF16)
    w2_in, w2_out = ffn2_w_in[0].astype(BF16), ffn2_w_out[0].astype(BF16)
    w_mix = mix_w_in[0]
    w_qkv = jnp.concatenate([_deinterleave(w_mix[:, :ATTN_WIDTH + KV_WIDTH]),
                             w_mix[:, ATTN_WIDTH + KV_WIDTH:QKV_WIDTH]], axis=1).astype(BF16)
    w_ret = w_mix[:, RET_OFF:GR_OFF].astype(BF16)
    w_gr = w_mix[:, GR_OFF:GA_OFF].astype(BF16)
    w_ga = w_mix[:, GA_OFF:GB_OFF].astype(BF16)
    w_gb = w_mix[:, GB_OFF:].astype(BF16)
    w_pa, w_pr = w_proj_attn[0].astype(BF16), w_proj_ret[0].astype(BF16)
    w_mo = mix_w_out[0].astype(BF16)

    cond = jnp.zeros((8, D_MODEL), F32).at[:batch].set(c).at[batch].set(c_ctx)
    mod = _adaln(cond, w_ada[0], b_ada).reshape(8, N_MOD, 1, D_MODEL)
    mod = jnp.broadcast_to(mod, (8, N_MOD, ROW_BLOCK, D_MODEL))

    x2 = x.reshape(batch * seq_len, D_MODEL)
    c2 = ctx.reshape(batch * CTX_LEN, D_MODEL)
    h1, n2 = _ffn(x2, mod, w1_in, w1_out, mod_base=0, rows_per_mod=seq_len, row_offset=0,
                  tm=TOKEN_TILE, tf=FF_TILE, emit_next=True)
    n2c, = _ffn(c2, mod, w1_in, w1_out, mod_base=0, rows_per_mod=batch * CTX_LEN, row_offset=batch,
                tm=batch * CTX_LEN, tf=FF_TILE, emit_h=False, emit_next=True)

    cos, sin = _rope_tables(seq_len)
    gq = _deinterleave(attn_q_gain[0].reshape(1, HEAD_DIM))
    gk = _deinterleave(attn_k_gain[0].reshape(1, HEAD_DIM))
    qkv_x = _qkv_proj(n2, w_qkv, cos, sin, gq, gk, pos_tiles=tiles_per_row)
    ones = jnp.ones((TOKEN_TILE, HEAD_DIM), F32)
    qkv_c = _qkv_proj(n2c, w_qkv, ones, jnp.zeros_like(ones), gq, gk, pos_tiles=1)

    ret_scale = jnp.concatenate([jnp.ones((1, RET_WIDTH), F32),
                                 jnp.full((1, RET_WIDTH), RET_DIM ** -0.5, F32),
                                 jnp.ones((1, RET_WIDTH), F32)], axis=1)
    ret_x = _scaled_mm(n2, w_ret, ret_scale, BF16, "ret_proj")
    ret_c = _scaled_mm(n2c, w_ret, ret_scale, BF16, "ret_proj")
    gate_x = _scaled_mm(n2, w_gr, jnp.ones((1, RET_WIDTH), F32), F32, "ret_gate_proj")

    qkv_x = qkv_x.reshape(batch, seq_len, QKV_WIDTH)
    qkv_c = qkv_c.reshape(batch, CTX_LEN, QKV_WIDTH)
    score_bound = HEAD_DIM ** 0.5 * jnp.max(jnp.abs(gq)) * jnp.max(jnp.abs(gk))
    ya = _attention((score_bound <= ATTN_SAFE_SCORE).astype(jnp.int32).reshape(1), qkv_x, qkv_c)

    log_gamma = jax.nn.log_sigmoid(ret_decay_logit[0].astype(F32))
    yr = _retention(log_gamma, ret_x.reshape(batch, seq_len, 3 * RET_WIDTH),
                    gate_x.reshape(batch, seq_len, RET_WIDTH),
                    ret_c.reshape(batch, CTX_LEN, 3 * RET_WIDTH))

    h2 = _merge(h1, n2, ya.reshape(batch * seq_len, ATTN_WIDTH), yr.reshape(batch * seq_len, RET_WIDTH), mod,
                w_ga, w_gb, w_pa, w_pr, w_mo, tiles_per_row=tiles_per_row)
    out = _ffn(h2, mod, w2_in, w2_out, mod_base=6, rows_per_mod=seq_len, row_offset=0,
               tm=TOKEN_TILE, tf=FF_TILE,
               final_norm=jnp.broadcast_to(final_norm.reshape(1, D_MODEL), (ROW_BLOCK, D_MODEL)))
    return out[0].reshape(batch, seq_len, D_MODEL)
```

```python
import functools

import jax
import jax.numpy as jnp
from jax import lax
from jax.experimental import pallas as pl
from jax.experimental.pallas import tpu as pltpu

D_MODEL = 2048
CTX_LEN = 256
GRID_W = 64
HEAD_DIM = 128
LANES = 128
ATTN_Q_HEADS = 8
ATTN_KV_HEADS = 2
ATTN_GROUPS = ATTN_Q_HEADS // ATTN_KV_HEADS
ATTN_WIDTH = ATTN_Q_HEADS * HEAD_DIM
KV_WIDTH = ATTN_KV_HEADS * HEAD_DIM
RET_HEADS = 8
RET_DIM = 128
RET_WIDTH = RET_HEADS * RET_DIM
D_FF = 5632
ROPE_THETA = 10000.0
NORM_EPS = 1e-6
N_MOD = 9

QKV_WIDTH = ATTN_WIDTH + 2 * KV_WIDTH
RET_OFF = QKV_WIDTH
GR_OFF = RET_OFF + 3 * RET_WIDTH
GA_OFF = GR_OFF + RET_WIDTH
GB_OFF = GA_OFF + D_MODEL

F32 = jnp.float32
BF16 = jnp.bfloat16

VMEM_LIMIT_BYTES = 60 * 1024 * 1024

ADALN_TILE = 1024
TOKEN_TILE = 512
FF_TILE = 512
ROW_BLOCK = 16
ROW_UNROLL = 4
MERGE_TILE = 512
ATTN_Q_TILE = 512
ATTN_KV_TILE = 768
RET_CHUNK = 256
RET_UNROLL = 8

Q_SCALE = HEAD_DIM ** -0.5 * 1.4426950408889634
ATTN_SAFE_SCORE = 32.0


def _params(*sem):
    return pltpu.CompilerParams(dimension_semantics=sem, vmem_limit_bytes=VMEM_LIMIT_BYTES)


def _rms(x):
    return x * lax.rsqrt(jnp.mean(x * x, axis=-1, keepdims=True) + NORM_EPS)


def _sigmoid(x):
    return 1.0 / (1.0 + jnp.exp(-x))


def _adaln_kernel(c_ref, w_ref, b_ref, o_ref):
    chunk = pl.program_id(0) // (D_MODEL // ADALN_TILE)
    c = c_ref[...]
    s = (c * _sigmoid(c)).astype(BF16)
    y = jnp.dot(s, w_ref[...].astype(BF16), preferred_element_type=F32) + b_ref[...]
    y = y + jnp.where(chunk % 3 == 1, 1.0, 0.0)
    o_ref[...] = y * jnp.where((chunk == 2) | (chunk == 8), 0.5, 1.0)


def _adaln(cond, w, b):
    n = w.shape[1]
    tn = ADALN_TILE
    return pl.pallas_call(
        _adaln_kernel,
        grid=(n // tn,),
        in_specs=[
            pl.BlockSpec((8, D_MODEL), lambda j: (0, 0)),
            pl.BlockSpec((D_MODEL, tn), lambda j: (0, j)),
            pl.BlockSpec((1, tn), lambda j: (0, j)),
        ],
        out_specs=pl.BlockSpec((8, tn), lambda j: (0, j)),
        out_shape=jax.ShapeDtypeStruct((8, n), F32),
        compiler_params=_params("arbitrary"),
        name="adaln",
    )(cond, w, b)


def _ffn_kernel(*refs, mod_base, emit_h, emit_next, final):
    h_ref, mod_ref, wa_ref, wb_ref, wo_ref = refs[:5]
    refs = refs[5:]
    if final:
        fn_ref, refs = refs[0], refs[1:]
    if emit_h:
        out_ref, refs = refs[0], refs[1:]
    if emit_next:
        nxt_ref, refs = refs[0], refs[1:]
    xn_sc, acc_sc, inv_sc = refs
    h_dst = out_ref if emit_h else acc_sc
    j = pl.program_id(1)
    n_row_blocks = h_ref.shape[0] // ROW_BLOCK

    lane_tiles = [slice(k, k + LANES) for k in range(0, D_MODEL, LANES)]

    def row_block(r):
        return pl.ds(pl.multiple_of(r * ROW_BLOCK, ROW_BLOCK), ROW_BLOCK)

    def inv_rms(x):
        inv = lax.rsqrt(jnp.mean(x * x, axis=-1, keepdims=True) + NORM_EPS)
        return jnp.broadcast_to(inv, (x.shape[0], LANES))

    def for_row_blocks(body, unroll):
        def step(r, carry):
            body(row_block(r))
            return carry

        lax.fori_loop(0, n_row_blocks, step, 0, unroll=unroll)

    def prologue():
        def stats(rows):
            inv_sc[rows, :] = inv_rms(h_ref[rows, :])

        def prenorm(rows):
            inv = inv_sc[rows, :]
            for sl in lane_tiles:
                n = h_ref[rows, sl] * inv * mod_ref[0, mod_base + 1, :, sl] + mod_ref[0, mod_base, :, sl]
                xn_sc[rows, sl] = n.astype(BF16)

        for_row_blocks(stats, True)
        for_row_blocks(prenorm, ROW_UNROLL)

    def matmuls(first):
        xn = xn_sc[...]
        half = wa_ref.shape[1] // 2
        partial = None
        for sl in (slice(0, half), slice(half, 2 * half)):
            a = jnp.dot(xn, wa_ref[:, sl], preferred_element_type=F32)
            b = jnp.dot(xn, wb_ref[:, sl], preferred_element_type=F32)
            act = (a * _sigmoid(a) * b).astype(BF16)
            p = jnp.dot(act, wo_ref[sl, :], preferred_element_type=F32)
            partial = p if partial is None else partial + p
        if first:
            acc_sc[...] = partial
        else:
            acc_sc[...] += partial

    def epilogue():
        def residual(rows):
            h = h_ref[rows, :] + mod_ref[0, mod_base + 2] * acc_sc[rows, :]
            h_dst[rows, :] = h
            inv_sc[rows, :] = inv_rms(h)

        def postnorm(rows):
            inv = inv_sc[rows, :]
            for sl in lane_tiles:
                n = h_dst[rows, sl] * inv
                if final:
                    out_ref[rows, sl] = n * fn_ref[:, sl]
                else:
                    n = n * mod_ref[0, mod_base + 4, :, sl] + mod_ref[0, mod_base + 3, :, sl]
                    nxt_ref[rows, sl] = n.astype(BF16)

        for_row_blocks(residual, True)
        if final or emit_next:
            for_row_blocks(postnorm, ROW_UNROLL)

    last = pl.num_programs(1) - 1

    @pl.when(j == 0)
    def _():
        prologue()
        matmuls(first=True)

    @pl.when((j > 0) & (j < last))
    def _():
        matmuls(first=False)

    @pl.when(j == last)
    def _():
        matmuls(first=False)
        epilogue()


def _ffn(h, mod, w_in, w_out, *, mod_base, rows_per_mod, row_offset, tm, tf, emit_h=True, emit_next=False,
         final_norm=None):
    t = h.shape[0]
    nf = D_FF // tf
    final = final_norm is not None
    tiles_per_row = rows_per_mod // tm
    mod_map = lambda i, j: (i // tiles_per_row + row_offset, 0, 0, 0)
    in_specs = [
        pl.BlockSpec((tm, D_MODEL), lambda i, j: (i, 0)),
        pl.BlockSpec((1, N_MOD, ROW_BLOCK, D_MODEL), mod_map),
        pl.BlockSpec((D_MODEL, tf), lambda i, j: (0, j)),
        pl.BlockSpec((D_MODEL, tf), lambda i, j: (0, j + nf)),
        pl.BlockSpec((tf, D_MODEL), lambda i, j: (j, 0)),
    ]
    args = [h, mod, w_in, w_in, w_out]
    if final:
        in_specs.append(pl.BlockSpec((ROW_BLOCK, D_MODEL), lambda i, j: (0, 0)))
        args.append(final_norm)
    out_specs, out_shape = [], []
    if emit_h:
        out_specs.append(pl.BlockSpec((tm, D_MODEL), lambda i, j: (i, 0)))
        out_shape.append(jax.ShapeDtypeStruct((t, D_MODEL), F32))
    if emit_next:
        out_specs.append(pl.BlockSpec((tm, D_MODEL), lambda i, j: (i, 0)))
        out_shape.append(jax.ShapeDtypeStruct((t, D_MODEL), BF16))
    return pl.pallas_call(
        functools.partial(_ffn_kernel, mod_base=mod_base, emit_h=emit_h, emit_next=emit_next, final=final),
        grid=(t // tm, nf),
        in_specs=in_specs,
        out_specs=out_specs,
        out_shape=out_shape,
        scratch_shapes=[pltpu.VMEM((tm, D_MODEL), BF16), pltpu.VMEM((tm, D_MODEL), F32),
                        pltpu.VMEM((tm, LANES), F32)],
        compiler_params=_params("parallel", "arbitrary"),
        name="ffn_final" if final else "ffn",
    )(*args)


def _qkv_kernel(n_ref, w_ref, cos_ref, sin_ref, gq_ref, gk_ref, o_ref, *y_bufs, n_tiles):
    i = pl.program_id(0)

    def finish(y_ref):
        cos = cos_ref[...]
        sin = sin_ref[...]
        gq = gq_ref[...] * Q_SCALE
        gk = gk_ref[...]
        for hh in range(ATTN_Q_HEADS + ATTN_KV_HEADS):
            sl = slice(hh * HEAD_DIM, (hh + 1) * HEAD_DIM)
            t = _rms(y_ref[:, sl]) * (gq if hh < ATTN_Q_HEADS else gk)
            o_ref[:, sl] = (t * cos + pltpu.roll(t, HEAD_DIM // 2, 1) * sin).astype(BF16)
        o_ref[:, ATTN_WIDTH + KV_WIDTH:] = y_ref[:, ATTN_WIDTH + KV_WIDTH:].astype(BF16)

    @pl.when(i == 0)
    def _():
        y_bufs[1][...] = jnp.zeros_like(y_bufs[1])

    for parity in range(2):
        @pl.when((i < n_tiles) & (i % 2 == parity))
        def _(y_cur=y_bufs[parity], y_prev=y_bufs[1 - parity]):
            finish(y_prev)
            y_cur[...] = jnp.dot(n_ref[...], w_ref[...], preferred_element_type=F32)

    @pl.when(i == n_tiles)
    def _():
        finish(y_bufs[1 - n_tiles % 2])


def _qkv_proj(n, w, cos, sin, gq, gk, *, pos_tiles):
    t = n.shape[0]
    tm = TOKEN_TILE
    n_tiles = t // tm
    prev_tile = lambda i: jnp.maximum(i - 1, 0)
    return pl.pallas_call(
        functools.partial(_qkv_kernel, n_tiles=n_tiles),
        grid=(n_tiles + 1,),
        in_specs=[
            pl.BlockSpec((tm, D_MODEL), lambda i: (jnp.minimum(i, n_tiles - 1), 0)),
            pl.BlockSpec((D_MODEL, QKV_WIDTH), lambda i: (0, 0)),
            pl.BlockSpec((tm, HEAD_DIM), lambda i: (prev_tile(i) % pos_tiles, 0)),
            pl.BlockSpec((tm, HEAD_DIM), lambda i: (prev_tile(i) % pos_tiles, 0)),
            pl.BlockSpec((1, HEAD_DIM), lambda i: (0, 0)),
            pl.BlockSpec((1, HEAD_DIM), lambda i: (0, 0)),
        ],
        out_specs=pl.BlockSpec((tm, QKV_WIDTH), lambda i: (prev_tile(i), 0)),
        out_shape=jax.ShapeDtypeStruct((t, QKV_WIDTH), BF16),
        scratch_shapes=[pltpu.VMEM((tm, QKV_WIDTH), F32), pltpu.VMEM((tm, QKV_WIDTH), F32)],
        compiler_params=_params("arbitrary"),
        name="qkv_proj",
    )(n, w, cos, sin, gq, gk)


def _scaled_mm_kernel(n_ref, w_ref, s_ref, o_ref):
    y = jnp.dot(n_ref[...], w_ref[...], preferred_element_type=F32)
    o_ref[...] = (y * s_ref[...]).astype(o_ref.dtype)


def _scaled_mm(n, w, col_scale, out_dtype, name):
    t = n.shape[0]
    nn = w.shape[1]
    tm = TOKEN_TILE
    return pl.pallas_call(
        _scaled_mm_kernel,
        grid=(t // tm,),
        in_specs=[
            pl.BlockSpec((tm, D_MODEL), lambda i: (i, 0)),
            pl.BlockSpec((D_MODEL, nn), lambda i: (0, 0)),
            pl.BlockSpec((1, nn), lambda i: (0, 0)),
        ],
        out_specs=pl.BlockSpec((tm, nn), lambda i: (i, 0)),
        out_shape=jax.ShapeDtypeStruct((t, nn), out_dtype),
        compiler_params=_params("parallel"),
        name=name,
    )(n, w, col_scale)


def _attn_kernel(flag_ref, q_ref, kc_ref, kx_ref, vc_ref, vx_ref, o_ref,
                 k_sc, v_sc, qs_sc, acc_sc, m_sc):
    tq = q_ref.shape[1]
    tk = ATTN_KV_TILE
    n_ctx = kc_ref.shape[1]
    lk = k_sc.shape[0]
    nk = lk // tk

    @pl.when(pl.program_id(2) == 0)
    def _():
        k_sc[:n_ctx, :] = kc_ref[0]
        k_sc[n_ctx:, :] = kx_ref[0]
        v_sc[:n_ctx, :HEAD_DIM] = vc_ref[0]
        v_sc[n_ctx:, :HEAD_DIM] = vx_ref[0]
        lane = lax.broadcasted_iota(jnp.int32, (n_ctx, HEAD_DIM), 1)
        ones_col = jnp.where(lane == 0, 1.0, 0.0).astype(BF16)
        for r in range(lk // n_ctx):
            v_sc[r * n_ctx:(r + 1) * n_ctx, HEAD_DIM:] = ones_col

    for g in range(ATTN_GROUPS):
        qs_sc[g * tq:(g + 1) * tq, :] = q_ref[0, :, g * HEAD_DIM:(g + 1) * HEAD_DIM]

    def scores(c):
        start = pl.multiple_of(c * tk, tk)
        k = k_sc[pl.ds(start, tk), :]
        s = lax.dot_general(qs_sc[...], k, (((1,), (1,)), ((), ())), preferred_element_type=F32)
        return s, v_sc[pl.ds(start, tk), :]

    def finish():
        acc = acc_sc[...]
        out = acc[:, :HEAD_DIM] / acc[:, HEAD_DIM:HEAD_DIM + 1]
        for g in range(ATTN_GROUPS):
            o_ref[0, :, g * HEAD_DIM:(g + 1) * HEAD_DIM] = out[g * tq:(g + 1) * tq, :].astype(BF16)

    @pl.when(flag_ref[0] != 0)
    def _():
        acc_sc[...] = jnp.zeros_like(acc_sc)

        def body(c, carry):
            s, v = scores(c)
            acc_sc[...] += jnp.dot(jnp.exp2(s).astype(BF16), v, preferred_element_type=F32)
            return carry

        lax.fori_loop(0, nk, body, 0, unroll=True)
        finish()

    @pl.when(flag_ref[0] == 0)
    def _():
        acc_sc[...] = jnp.zeros_like(acc_sc)
        m_sc[...] = jnp.full_like(m_sc, -jnp.inf)

        def body(c, carry):
            s, v = scores(c)
            m_prev = m_sc[...]
            m_new = jnp.maximum(m_prev, jnp.max(s, axis=-1, keepdims=True))
            p = jnp.exp2(s - m_new).astype(BF16)
            acc_sc[...] = jnp.exp2(m_prev - m_new) * acc_sc[...] + jnp.dot(p, v, preferred_element_type=F32)
            m_sc[...] = m_new
            return carry

        lax.fori_loop(0, nk, body, 0)
        finish()


def _attention(bounded_flag, qkv_x, qkv_c):
    b, l, _ = qkv_x.shape
    lc = qkv_c.shape[1]
    lk = lc + l
    tq = ATTN_Q_TILE
    gw = ATTN_GROUPS * HEAD_DIM
    rows = ATTN_GROUPS * tq
    k_col = ATTN_WIDTH // HEAD_DIM
    v_col = k_col + ATTN_KV_HEADS
    lat = lambda col: pl.BlockSpec((1, l, HEAD_DIM), lambda bi, hi, qi: (bi, 0, hi + col))
    ctx = lambda col: pl.BlockSpec((1, lc, HEAD_DIM), lambda bi, hi, qi: (bi, 0, hi + col))
    return pl.pallas_call(
        _attn_kernel,
        grid=(b, ATTN_KV_HEADS, l // tq),
        in_specs=[
            pl.BlockSpec(memory_space=pltpu.SMEM),
            pl.BlockSpec((1, tq, gw), lambda bi, hi, qi: (bi, qi, hi)),
            ctx(k_col), lat(k_col), ctx(v_col), lat(v_col),
        ],
        out_specs=pl.BlockSpec((1, tq, gw), lambda bi, hi, qi: (bi, qi, hi)),
        out_shape=jax.ShapeDtypeStruct((b, l, ATTN_WIDTH), BF16),
        scratch_shapes=[
            pltpu.VMEM((lk, HEAD_DIM), BF16),
            pltpu.VMEM((lk, 2 * HEAD_DIM), BF16),
            pltpu.VMEM((rows, HEAD_DIM), BF16),
            pltpu.VMEM((rows, 2 * HEAD_DIM), F32),
            pltpu.VMEM((rows, 1), F32),
        ],
        compiler_params=_params("parallel", "parallel", "arbitrary"),
        name="attention",
    )(bounded_flag, qkv_x, qkv_c, qkv_x, qkv_c, qkv_x)


def _ret_kernel(lg_ref, q_ref, k_ref, v_ref, g_ref, kc_ref, vc_ref, o_ref, uf_sc, sb_sc):
    c_len = RET_CHUNK
    n_chunks = q_ref.shape[1] // c_len
    n_ctx = kc_ref.shape[1]
    head = pl.program_id(1)
    lgf = lg_ref[0, head]
    lgb = lg_ref[1, head]

    row = lax.broadcasted_iota(jnp.int32, (c_len, 1), 0).astype(F32)
    vdec_f = jnp.exp((c_len - 1.0 - row) * lgf)
    vdec_b = jnp.exp(row * lgb)
    qdec_f = jnp.exp((row + 1.0) * lgf)
    qdec_b = jnp.exp((c_len - row) * lgb)
    chunk_f = jnp.exp(jnp.full((1, RET_DIM), c_len, F32) * lgf)
    chunk_b = jnp.exp(jnp.full((1, RET_DIM), c_len, F32) * lgb)
    diff = (lax.broadcasted_iota(jnp.int32, (c_len, c_len), 0)
            - lax.broadcasted_iota(jnp.int32, (c_len, c_len), 1)).astype(F32)
    decay = (jnp.where(diff >= 0, jnp.exp(jnp.maximum(diff, 0.0) * lgf), 0.0)
             + jnp.where(diff <= 0, jnp.exp(jnp.maximum(-diff, 0.0) * lgb), 0.0))

    def kv_outer(k, v, dec_f, dec_b):
        vf = v.astype(F32)
        v2 = jnp.concatenate([(vf * dec_f).astype(BF16), (vf * dec_b).astype(BF16)], axis=1)
        return lax.dot_general(k, v2, (((0,), (0,)), ((), ())), preferred_element_type=F32)

    crow = lax.broadcasted_iota(jnp.int32, (n_ctx, 1), 0).astype(F32)
    s0 = kv_outer(kc_ref[0], vc_ref[0], jnp.exp((n_ctx - 1.0 - crow) * lgf), jnp.exp(crow * lgb))

    def back_body(t, sb):
        c = n_chunks - 1 - t
        start = pl.multiple_of(c * c_len, c_len)
        sb_sc[c] = sb
        u = kv_outer(k_ref[0, pl.ds(start, c_len), :], v_ref[0, pl.ds(start, c_len), :], vdec_f, vdec_b)
        uf_sc[c] = u[:, :RET_DIM]
        return sb * chunk_b + u[:, RET_DIM:]

    lax.fori_loop(0, n_chunks, back_body, s0[:, RET_DIM:], unroll=RET_UNROLL)

    def fwd_body(c, sf):
        start = pl.multiple_of(c * c_len, c_len)
        q = q_ref[0, pl.ds(start, c_len), :]
        k = k_ref[0, pl.ds(start, c_len), :]
        v = v_ref[0, pl.ds(start, c_len), :]
        inner = lax.dot_general(q, k, (((1,), (1,)), ((), ())), preferred_element_type=F32)
        y = jnp.dot((inner * decay).astype(BF16), v, preferred_element_type=F32)
        states = jnp.concatenate([sf.astype(BF16), sb_sc[c].astype(BF16)], axis=1)
        cross = jnp.dot(q, states, preferred_element_type=F32)
        y = y + cross[:, :RET_DIM] * qdec_f + cross[:, RET_DIM:] * qdec_b
        gate = g_ref[0, pl.ds(start, c_len), :]
        o_ref[0, pl.ds(start, c_len), :] = (gate * _sigmoid(gate) * _rms(y)).astype(BF16)
        return sf * chunk_f + uf_sc[c]

    lax.fori_loop(0, n_chunks, fwd_body, s0[:, :RET_DIM], unroll=RET_UNROLL)


def _retention(log_gamma, ret_x, gate_x, ret_c):
    b, l, _ = ret_x.shape
    lc = ret_c.shape[1]
    n_chunks = l // RET_CHUNK
    seq = lambda off: pl.BlockSpec((1, l, RET_DIM), lambda bi, hi: (bi, 0, hi + off))
    ctx = lambda off: pl.BlockSpec((1, lc, RET_DIM), lambda bi, hi: (bi, 0, hi + off))
    return pl.pallas_call(
        _ret_kernel,
        grid=(b, RET_HEADS),
        in_specs=[
            pl.BlockSpec(memory_space=pltpu.SMEM),
            seq(0), seq(RET_HEADS), seq(2 * RET_HEADS), seq(0),
            ctx(RET_HEADS), ctx(2 * RET_HEADS),
        ],
        out_specs=seq(0),
        out_shape=jax.ShapeDtypeStruct((b, l, RET_WIDTH), BF16),
        scratch_shapes=[
            pltpu.VMEM((n_chunks, RET_DIM, RET_DIM), F32),
            pltpu.VMEM((n_chunks, RET_DIM, RET_DIM), F32),
        ],
        compiler_params=_params("parallel", "arbitrary"),
        name="retention",
    )(log_gamma, ret_x, ret_x, ret_x, gate_x, ret_c, ret_c)


def _merge_kernel(h_ref, n_ref, ya_ref, yr_ref, mod_ref, wga_ref, wgb_ref, wpa_ref, wpr_ref, wo_ref,
                  out_ref, acc_sc):
    j = pl.program_id(1)
    last = pl.num_programs(1) - 1

    def step(first):
        n = n_ref[...]
        ga = jnp.dot(n, wga_ref[...], preferred_element_type=F32)
        gb = jnp.dot(n, wgb_ref[...], preferred_element_type=F32)
        pa = jnp.dot(ya_ref[...], wpa_ref[...], preferred_element_type=F32)
        pr = jnp.dot(yr_ref[...], wpr_ref[...], preferred_element_type=F32)
        z = (_sigmoid(ga) * pa + _sigmoid(gb) * pr).astype(BF16)
        partial = jnp.dot(z, wo_ref[...], preferred_element_type=F32)
        if first:
            acc_sc[...] = partial
        else:
            acc_sc[...] += partial

    @pl.when(j == 0)
    def _():
        step(first=True)

    @pl.when((j > 0) & (j < last))
    def _():
        step(first=False)

    @pl.when(j == last)
    def _():
        step(first=False)
        out_ref[...] = h_ref[...] + mod_ref[0, 5, 0:1, :] * acc_sc[...]


def _merge(h, n, ya, yr, mod, w_ga, w_gb, w_pa, w_pr, w_out, *, tiles_per_row):
    t = h.shape[0]
    tm, tc = TOKEN_TILE, MERGE_TILE
    return pl.pallas_call(
        _merge_kernel,
        grid=(t // tm, D_MODEL // tc),
        in_specs=[
            pl.BlockSpec((tm, D_MODEL), lambda i, j: (i, 0)),
            pl.BlockSpec((tm, D_MODEL), lambda i, j: (i, 0)),
            pl.BlockSpec((tm, ATTN_WIDTH), lambda i, j: (i, 0)),
            pl.BlockSpec((tm, RET_WIDTH), lambda i, j: (i, 0)),
            pl.BlockSpec((1, N_MOD, ROW_BLOCK, D_MODEL), lambda i, j: (i // tiles_per_row, 0, 0, 0)),
            pl.BlockSpec((D_MODEL, tc), lambda i, j: (0, j)),
            pl.BlockSpec((D_MODEL, tc), lambda i, j: (0, j)),
            pl.BlockSpec((ATTN_WIDTH, tc), lambda i, j: (0, j)),
            pl.BlockSpec((RET_WIDTH, tc), lambda i, j: (0, j)),
            pl.BlockSpec((tc, D_MODEL), lambda i, j: (j, 0)),
        ],
        out_specs=pl.BlockSpec((tm, D_MODEL), lambda i, j: (i, 0)),
        out_shape=jax.ShapeDtypeStruct((t, D_MODEL), F32),
        scratch_shapes=[pltpu.VMEM((tm, D_MODEL), F32)],
        compiler_params=_params("parallel", "arbitrary"),
        name="merge",
    )(h, n, ya, yr, mod, w_ga, w_gb, w_pa, w_pr, w_out)


def _rope_tables(seq_len):
    rows = seq_len // GRID_W
    row = jnp.repeat(jnp.arange(rows, dtype=F32), GRID_W)
    col = jnp.tile(jnp.arange(GRID_W, dtype=F32), rows)
    half = HEAD_DIM // 2
    inv_freq = ROPE_THETA ** (-jnp.arange(0, half, 2, dtype=F32) / half)
    ang = jnp.concatenate([row[:, None] * inv_freq, col[:, None] * inv_freq], axis=-1)
    cos, sin = jnp.cos(ang), jnp.sin(ang)
    return jnp.concatenate([cos, cos], axis=-1), jnp.concatenate([-sin, sin], axis=-1)


def _deinterleave(t):
    lead = t.shape[:-1]
    t = t.reshape(lead + (-1, HEAD_DIM // 2, 2))
    return jnp.swapaxes(t, -1, -2).reshape(lead + (-1,))


def kernel(x, c, ctx, c_ctx, w_ada, b_ada, ffn1_w_in, ffn1_w_out, mix_w_in, attn_q_gain, attn_k_gain,
           ret_decay_logit, w_proj_attn, w_proj_ret, mix_w_out, ffn2_w_in, ffn2_w_out, final_norm):
    batch, seq_len, _ = x.shape
    assert w_ada.shape[0] == 1, "single-layer block"
    assert seq_len % TOKEN_TILE == 0 and (batch * CTX_LEN) % TOKEN_TILE == 0
    tiles_per_row = seq_len // TOKEN_TILE

    w1_in, w1_out = ffn1_w_in[0].astype(BF16), ffn1_w_out[0].astype(BF16)
    w2_in, w2_out = ffn2_w_in[0].astype(BF16), ffn2_w_out[0].astype(BF16)
    w_mix = mix_w_in[0]
    w_qkv = jnp.concatenate([_deinterleave(w_mix[:, :ATTN_WIDTH + KV_WIDTH]),
                             w_mix[:, ATTN_WIDTH + KV_WIDTH:QKV_WIDTH]], axis=1).astype(BF16)
    w_ret = w_mix[:, RET_OFF:GR_OFF].astype(BF16)
    w_gr = w_mix[:, GR_OFF:GA_OFF].astype(BF16)
    w_ga = w_mix[:, GA_OFF:GB_OFF].astype(BF16)
    w_gb = w_mix[:, GB_OFF:].astype(BF16)
    w_pa, w_pr = w_proj_attn[0].astype(BF16), w_proj_ret[0].astype(BF16)
    w_mo = mix_w_out[0].astype(BF16)

    cond = jnp.zeros((8, D_MODEL), F32).at[:batch].set(c).at[batch].set(c_ctx)
    mod = _adaln(cond, w_ada[0], b_ada).reshape(8, N_MOD, 1, D_MODEL)
    mod = jnp.broadcast_to(mod, (8, N_MOD, ROW_BLOCK, D_MODEL))

    x2 = x.reshape(batch * seq_len, D_MODEL)
    c2 = ctx.reshape(batch * CTX_LEN, D_MODEL)
    h1, n2 = _ffn(x2, mod, w1_in, w1_out, mod_base=0, rows_per_mod=seq_len, row_offset=0,
                  tm=TOKEN_TILE, tf=FF_TILE, emit_next=True)
    n2c, = _ffn(c2, mod, w1_in, w1_out, mod_base=0, rows_per_mod=batch * CTX_LEN, row_offset=batch,
                tm=batch * CTX_LEN, tf=FF_TILE, emit_h=False, emit_next=True)

    cos, sin = _rope_tables(seq_len)
    gq = _deinterleave(attn_q_gain[0].reshape(1, HEAD_DIM))
    gk = _deinterleave(attn_k_gain[0].reshape(1, HEAD_DIM))
    qkv_x = _qkv_proj(n2, w_qkv, cos, sin, gq, gk, pos_tiles=tiles_per_row)
    ones = jnp.ones((TOKEN_TILE, HEAD_DIM), F32)
    qkv_c = _qkv_proj(n2c, w_qkv, ones, jnp.zeros_like(ones), gq, gk, pos_tiles=1)

    ret_scale = jnp.concatenate([jnp.ones((1, RET_WIDTH), F32),
                                 jnp.full((1, RET_WIDTH), RET_DIM ** -0.5, F32),
                                 jnp.ones((1, RET_WIDTH), F32)], axis=1)
    ret_x = _scaled_mm(n2, w_ret, ret_scale, BF16, "ret_proj")
    ret_c = _scaled_mm(n2c, w_ret, ret_scale, BF16, "ret_proj")
    gate_x = _scaled_mm(n2, w_gr, jnp.ones((1, RET_WIDTH), F32), F32, "ret_gate_proj")

    qkv_x = qkv_x.reshape(batch, seq_len, QKV_WIDTH)
    qkv_c = qkv_c.reshape(batch, CTX_LEN, QKV_WIDTH)
    score_bound = HEAD_DIM ** 0.5 * jnp.max(jnp.abs(gq)) * jnp.max(jnp.abs(gk))
    ya = _attention((score_bound <= ATTN_SAFE_SCORE).astype(jnp.int32).reshape(1), qkv_x, qkv_c)

    log_gamma = jax.nn.log_sigmoid(ret_decay_logit[0].astype(F32))
    yr = _retention(log_gamma, ret_x.reshape(batch, seq_len, 3 * RET_WIDTH),
                    gate_x.reshape(batch, seq_len, RET_WIDTH),
                    ret_c.reshape(batch, CTX_LEN, 3 * RET_WIDTH))

    h2 = _merge(h1, n2, ya.reshape(batch * seq_len, ATTN_WIDTH), yr.reshape(batch * seq_len, RET_WIDTH), mod,
                w_ga, w_gb, w_pa, w_pr, w_mo, tiles_per_row=tiles_per_row)
    out = _ffn(h2, mod, w2_in, w2_out, mod_base=6, rows_per_mod=seq_len, row_offset=0,
               tm=TOKEN_TILE, tf=FF_TILE,
               final_norm=jnp.broadcast_to(final_norm.reshape(1, D_MODEL), (ROW_BLOCK, D_MODEL)))
    return out[0].reshape(batch, seq_len, D_MODEL)
```

```python
import functools

import jax
import jax.numpy as jnp
from jax import lax
from jax.experimental import pallas as pl
from jax.experimental.pallas import tpu as pltpu

D_MODEL = 2048
CTX_LEN = 256
GRID_W = 64
HEAD_DIM = 128
LANES = 128
ATTN_Q_HEADS = 8
ATTN_KV_HEADS = 2
ATTN_GROUPS = ATTN_Q_HEADS // ATTN_KV_HEADS
ATTN_WIDTH = ATTN_Q_HEADS * HEAD_DIM
KV_WIDTH = ATTN_KV_HEADS * HEAD_DIM
RET_HEADS = 8
RET_DIM = 128
RET_WIDTH = RET_HEADS * RET_DIM
D_FF = 5632
ROPE_THETA = 10000.0
NORM_EPS = 1e-6
N_MOD = 9

QKV_WIDTH = ATTN_WIDTH + 2 * KV_WIDTH
RET_OFF = QKV_WIDTH
GR_OFF = RET_OFF + 3 * RET_WIDTH
GA_OFF = GR_OFF + RET_WIDTH
GB_OFF = GA_OFF + D_MODEL

F32 = jnp.float32
BF16 = jnp.bfloat16

VMEM_LIMIT_BYTES = 60 * 1024 * 1024

ADALN_TILE = 1024
TOKEN_TILE = 512
FF_TILE = 512
ROW_BLOCK = 16
ROW_UNROLL = 4
MERGE_TILE = 512
ATTN_Q_TILE = 512
ATTN_KV_TILE = 768
RET_CHUNK = 256
RET_UNROLL = 8

Q_SCALE = HEAD_DIM ** -0.5 * 1.4426950408889634
ATTN_SAFE_SCORE = 32.0


def _params(*sem):
    return pltpu.CompilerParams(dimension_semantics=sem, vmem_limit_bytes=VMEM_LIMIT_BYTES)


def _rms(x):
    return x * lax.rsqrt(jnp.mean(x * x, axis=-1, keepdims=True) + NORM_EPS)


def _sigmoid(x):
    return 1.0 / (1.0 + jnp.exp(-x))


def _adaln_kernel(c_ref, w_ref, b_ref, o_ref):
    chunk = pl.program_id(0) // (D_MODEL // ADALN_TILE)
    c = c_ref[...]
    s = (c * _sigmoid(c)).astype(BF16)
    y = jnp.dot(s, w_ref[...].astype(BF16), preferred_element_type=F32) + b_ref[...]
    y = y + jnp.where(chunk % 3 == 1, 1.0, 0.0)
    o_ref[...] = y * jnp.where((chunk == 2) | (chunk == 8), 0.5, 1.0)


def _adaln(cond, w, b):
    n = w.shape[1]
    tn = ADALN_TILE
    return pl.pallas_call(
        _adaln_kernel,
        grid=(n // tn,),
        in_specs=[
            pl.BlockSpec((8, D_MODEL), lambda j: (0, 0)),
            pl.BlockSpec((D_MODEL, tn), lambda j: (0, j)),
            pl.BlockSpec((1, tn), lambda j: (0, j)),
        ],
        out_specs=pl.BlockSpec((8, tn), lambda j: (0, j)),
        out_shape=jax.ShapeDtypeStruct((8, n), F32),
        compiler_params=_params("arbitrary"),
        name="adaln",
    )(cond, w, b)


def _ffn_kernel(*refs, mod_base, emit_h, emit_next, final):
    h_ref, mod_ref, wa_ref, wb_ref, wo_ref = refs[:5]
    refs = refs[5:]
    if final:
        fn_ref, refs = refs[0], refs[1:]
    if emit_h:
        out_ref, refs = refs[0], refs[1:]
    if emit_next:
        nxt_ref, refs = refs[0], refs[1:]
    xn_sc, acc_sc, inv_sc = refs
    h_dst = out_ref if emit_h else acc_sc
    j = pl.program_id(1)
    n_row_blocks = h_ref.shape[0] // ROW_BLOCK

    lane_tiles = [slice(k, k + LANES) for k in range(0, D_MODEL, LANES)]

    def row_block(r):
        return pl.ds(pl.multiple_of(r * ROW_BLOCK, ROW_BLOCK), ROW_BLOCK)

    def inv_rms(x):
        inv = lax.rsqrt(jnp.mean(x * x, axis=-1, keepdims=True) + NORM_EPS)
        return jnp.broadcast_to(inv, (x.shape[0], LANES))

    def for_row_blocks(body, unroll):
        def step(r, carry):
            body(row_block(r))
            return carry

        lax.fori_loop(0, n_row_blocks, step, 0, unroll=unroll)

    def prologue():
        def stats(rows):
            inv_sc[rows, :] = inv_rms(h_ref[rows, :])

        def prenorm(rows):
            inv = inv_sc[rows, :]
            for sl in lane_tiles:
                n = h_ref[rows, sl] * inv * mod_ref[0, mod_base + 1, :, sl] + mod_ref[0, mod_base, :, sl]
                xn_sc[rows, sl] = n.astype(BF16)

        for_row_blocks(stats, True)
        for_row_blocks(prenorm, ROW_UNROLL)

    def matmuls(first):
        xn = xn_sc[...]
        half = wa_ref.shape[1] // 2
        partial = None
        for sl in (slice(0, half), slice(half, 2 * half)):
            a = jnp.dot(xn, wa_ref[:, sl], preferred_element_type=F32)
            b = jnp.dot(xn, wb_ref[:, sl], preferred_element_type=F32)
            act = (a * _sigmoid(a) * b).astype(BF16)
            p = jnp.dot(act, wo_ref[sl, :], preferred_element_type=F32)
            partial = p if partial is None else partial + p
        if first:
            acc_sc[...] = partial
        else:
            acc_sc[...] += partial

    def epilogue():
        def residual(rows):
            h = h_ref[rows, :] + mod_ref[0, mod_base + 2] * acc_sc[rows, :]
            h_dst[rows, :] = h
            inv_sc[rows, :] = inv_rms(h)

        def postnorm(rows):
            inv = inv_sc[rows, :]
            for sl in lane_tiles:
                n = h_dst[rows, sl] * inv
                if final:
                    out_ref[rows, sl] = n * fn_ref[:, sl]
                else:
                    n = n * mod_ref[0, mod_base + 4, :, sl] + mod_ref[0, mod_base + 3, :, sl]
                    nxt_ref[rows, sl] = n.astype(BF16)

        for_row_blocks(residual, True)
        if final or emit_next:
            for_row_blocks(postnorm, ROW_UNROLL)

    last = pl.num_programs(1) - 1

    @pl.when(j == 0)
    def _():
        prologue()
        matmuls(first=True)

    @pl.when((j > 0) & (j < last))
    def _():
        matmuls(first=False)

    @pl.when(j == last)
    def _():
        matmuls(first=False)
        epilogue()


def _ffn(h, mod, w_in, w_out, *, mod_base, rows_per_mod, row_offset, tm, tf, emit_h=True, emit_next=False,
         final_norm=None):
    t = h.shape[0]
    nf = D_FF // tf
    final = final_norm is not None
    tiles_per_row = rows_per_mod // tm
    mod_map = lambda i, j: (i // tiles_per_row + row_offset, 0, 0, 0)
    in_specs = [
        pl.BlockSpec((tm, D_MODEL), lambda i, j: (i, 0)),
        pl.BlockSpec((1, N_MOD, ROW_BLOCK, D_MODEL), mod_map),
        pl.BlockSpec((D_MODEL, tf), lambda i, j: (0, j)),
        pl.BlockSpec((D_MODEL, tf), lambda i, j: (0, j + nf)),
        pl.BlockSpec((tf, D_MODEL), lambda i, j: (j, 0)),
    ]
    args = [h, mod, w_in, w_in, w_out]
    if final:
        in_specs.append(pl.BlockSpec((ROW_BLOCK, D_MODEL), lambda i, j: (0, 0)))
        args.append(final_norm)
    out_specs, out_shape = [], []
    if emit_h:
        out_specs.append(pl.BlockSpec((tm, D_MODEL), lambda i, j: (i, 0)))
        out_shape.append(jax.ShapeDtypeStruct((t, D_MODEL), F32))
    if emit_next:
        out_specs.append(pl.BlockSpec((tm, D_MODEL), lambda i, j: (i, 0)))
        out_shape.append(jax.ShapeDtypeStruct((t, D_MODEL), BF16))
    return pl.pallas_call(
        functools.partial(_ffn_kernel, mod_base=mod_base, emit_h=emit_h, emit_next=emit_next, final=final),
        grid=(t // tm, nf),
        in_specs=in_specs,
        out_specs=out_specs,
        out_shape=out_shape,
        scratch_shapes=[pltpu.VMEM((tm, D_MODEL), BF16), pltpu.VMEM((tm, D_MODEL), F32),
                        pltpu.VMEM((tm, LANES), F32)],
        compiler_params=_params("parallel", "arbitrary"),
        name="ffn_final" if final else "ffn",
    )(*args)


def _qkv_kernel(n_ref, w_ref, cos_ref, sin_ref, gq_ref, gk_ref, o_ref, *y_bufs, n_tiles):
    i = pl.program_id(0)

    def finish(y_ref):
        cos = cos_ref[...]
        sin = sin_ref[...]
        gq = gq_ref[...] * Q_SCALE
        gk = gk_ref[...]
        for hh in range(ATTN_Q_HEADS + ATTN_KV_HEADS):
            sl = slice(hh * HEAD_DIM, (hh + 1) * HEAD_DIM)
            t = _rms(y_ref[:, sl]) * (gq if hh < ATTN_Q_HEADS else gk)
            o_ref[:, sl] = (t * cos + pltpu.roll(t, HEAD_DIM // 2, 1) * sin).astype(BF16)
        o_ref[:, ATTN_WIDTH + KV_WIDTH:] = y_ref[:, ATTN_WIDTH + KV_WIDTH:].astype(BF16)

    @pl.when(i == 0)
    def _():
        y_bufs[1][...] = jnp.zeros_like(y_bufs[1])

    for parity in range(2):
        @pl.when((i < n_tiles) & (i % 2 == parity))
        def _(y_cur=y_bufs[parity], y_prev=y_bufs[1 - parity]):
            finish(y_prev)
            y_cur[...] = jnp.dot(n_ref[...], w_ref[...], preferred_element_type=F32)

    @pl.when(i == n_tiles)
    def _():
        finish(y_bufs[1 - n_tiles % 2])


def _qkv_proj(n, w, cos, sin, gq, gk, *, pos_tiles):
    t = n.shape[0]
    tm = TOKEN_TILE
    n_tiles = t // tm
    prev_tile = lambda i: jnp.maximum(i - 1, 0)
    return pl.pallas_call(
        functools.partial(_qkv_kernel, n_tiles=n_tiles),
        grid=(n_tiles + 1,),
        in_specs=[
            pl.BlockSpec((tm, D_MODEL), lambda i: (jnp.minimum(i, n_tiles - 1), 0)),
            pl.BlockSpec((D_MODEL, QKV_WIDTH), lambda i: (0, 0)),
            pl.BlockSpec((tm, HEAD_DIM), lambda i: (prev_tile(i) % pos_tiles, 0)),
            pl.BlockSpec((tm, HEAD_DIM), lambda i: (prev_tile(i) % pos_tiles, 0)),
            pl.BlockSpec((1, HEAD_DIM), lambda i: (0, 0)),
            pl.BlockSpec((1, HEAD_DIM), lambda i: (0, 0)),
        ],
        out_specs=pl.BlockSpec((tm, QKV_WIDTH), lambda i: (prev_tile(i), 0)),
        out_shape=jax.ShapeDtypeStruct((t, QKV_WIDTH), BF16),
        scratch_shapes=[pltpu.VMEM((tm, QKV_WIDTH), F32), pltpu.VMEM((tm, QKV_WIDTH), F32)],
        compiler_params=_params("arbitrary"),
        name="qkv_proj",
    )(n, w, cos, sin, gq, gk)


def _scaled_mm_kernel(n_ref, w_ref, s_ref, o_ref):
    y = jnp.dot(n_ref[...], w_ref[...], preferred_element_type=F32)
    o_ref[...] = (y * s_ref[...]).astype(o_ref.dtype)


def _scaled_mm(n, w, col_scale, out_dtype, name):
    t = n.shape[0]
    nn = w.shape[1]
    tm = TOKEN_TILE
    return pl.pallas_call(
        _scaled_mm_kernel,
        grid=(t // tm,),
        in_specs=[
            pl.BlockSpec((tm, D_MODEL), lambda i: (i, 0)),
            pl.BlockSpec((D_MODEL, nn), lambda i: (0, 0)),
            pl.BlockSpec((1, nn), lambda i: (0, 0)),
        ],
        out_specs=pl.BlockSpec((tm, nn), lambda i: (i, 0)),
        out_shape=jax.ShapeDtypeStruct((t, nn), out_dtype),
        compiler_params=_params("parallel"),
        name=name,
    )(n, w, col_scale)


def _attn_kernel(flag_ref, q_ref, kc_ref, kx_ref, vc_ref, vx_ref, o_ref,
                 k_sc, v_sc, qs_sc, acc_sc, m_sc):
    tq = q_ref.shape[1]
    tk = ATTN_KV_TILE
    n_ctx = kc_ref.shape[1]
    lk = k_sc.shape[0]
    nk = lk // tk

    @pl.when(pl.program_id(2) == 0)
    def _():
        k_sc[:n_ctx, :] = kc_ref[0]
        k_sc[n_ctx:, :] = kx_ref[0]
        v_sc[:n_ctx, :HEAD_DIM] = vc_ref[0]
        v_sc[n_ctx:, :HEAD_DIM] = vx_ref[0]
        lane = lax.broadcasted_iota(jnp.int32, (n_ctx, HEAD_DIM), 1)
        ones_col = jnp.where(lane == 0, 1.0, 0.0).astype(BF16)
        for r in range(lk // n_ctx):
            v_sc[r * n_ctx:(r + 1) * n_ctx, HEAD_DIM:] = ones_col

    for g in range(ATTN_GROUPS):
        qs_sc[g * tq:(g + 1) * tq, :] = q_ref[0, :, g * HEAD_DIM:(g + 1) * HEAD_DIM]

    def scores(c):
        start = pl.multiple_of(c * tk, tk)
        k = k_sc[pl.ds(start, tk), :]
        s = lax.dot_general(qs_sc[...], k, (((1,), (1,)), ((), ())), preferred_element_type=F32)
        return s, v_sc[pl.ds(start, tk), :]

    def finish():
        acc = acc_sc[...]
        out = acc[:, :HEAD_DIM] / acc[:, HEAD_DIM:HEAD_DIM + 1]
        for g in range(ATTN_GROUPS):
            o_ref[0, :, g * HEAD_DIM:(g + 1) * HEAD_DIM] = out[g * tq:(g + 1) * tq, :].astype(BF16)

    @pl.when(flag_ref[0] != 0)
    def _():
        acc_sc[...] = jnp.zeros_like(acc_sc)

        def body(c, carry):
            s, v = scores(c)
            acc_sc[...] += jnp.dot(jnp.exp2(s).astype(BF16), v, preferred_element_type=F32)
            return carry

        lax.fori_loop(0, nk, body, 0, unroll=True)
        finish()

    @pl.when(flag_ref[0] == 0)
    def _():
        acc_sc[...] = jnp.zeros_like(acc_sc)
        m_sc[...] = jnp.full_like(m_sc, -jnp.inf)

        def body(c, carry):
            s, v = scores(c)
            m_prev = m_sc[...]
            m_new = jnp.maximum(m_prev, jnp.max(s, axis=-1, keepdims=True))
            p = jnp.exp2(s - m_new).astype(BF16)
            acc_sc[...] = jnp.exp2(m_prev - m_new) * acc_sc[...] + jnp.dot(p, v, preferred_element_type=F32)
            m_sc[...] = m_new
            return carry

        lax.fori_loop(0, nk, body, 0)
        finish()


def _attention(bounded_flag, qkv_x, qkv_c):
    b, l, _ = qkv_x.shape
    lc = qkv_c.shape[1]
    lk = lc + l
    tq = ATTN_Q_TILE
    gw = ATTN_GROUPS * HEAD_DIM
    rows = ATTN_GROUPS * tq
    k_col = ATTN_WIDTH // HEAD_DIM
    v_col = k_col + ATTN_KV_HEADS
    lat = lambda col: pl.BlockSpec((1, l, HEAD_DIM), lambda bi, hi, qi: (bi, 0, hi + col))
    ctx = lambda col: pl.BlockSpec((1, lc, HEAD_DIM), lambda bi, hi, qi: (bi, 0, hi + col))
    return pl.pallas_call(
        _attn_kernel,
        grid=(b, ATTN_KV_HEADS, l // tq),
        in_specs=[
            pl.BlockSpec(memory_space=pltpu.SMEM),
            pl.BlockSpec((1, tq, gw), lambda bi, hi, qi: (bi, qi, hi)),
            ctx(k_col), lat(k_col), ctx(v_col), lat(v_col),
        ],
        out_specs=pl.BlockSpec((1, tq, gw), lambda bi, hi, qi: (bi, qi, hi)),
        out_shape=jax.ShapeDtypeStruct((b, l, ATTN_WIDTH), BF16),
        scratch_shapes=[
            pltpu.VMEM((lk, HEAD_DIM), BF16),
            pltpu.VMEM((lk, 2 * HEAD_DIM), BF16),
            pltpu.VMEM((rows, HEAD_DIM), BF16),
            pltpu.VMEM((rows, 2 * HEAD_DIM), F32),
            pltpu.VMEM((rows, 1), F32),
        ],
        compiler_params=_params("parallel", "parallel", "arbitrary"),
        name="attention",
    )(bounded_flag, qkv_x, qkv_c, qkv_x, qkv_c, qkv_x)


def _ret_kernel(lg_ref, q_ref, k_ref, v_ref, g_ref, kc_ref, vc_ref, o_ref, uf_sc, sb_sc):
    c_len = RET_CHUNK
    n_chunks = q_ref.shape[1] // c_len
    n_ctx = kc_ref.shape[1]
    head = pl.program_id(1)
    lgf = lg_ref[0, head]
    lgb = lg_ref[1, head]

    row = lax.broadcasted_iota(jnp.int32, (c_len, 1), 0).astype(F32)
    vdec_f = jnp.exp((c_len - 1.0 - row) * lgf)
    vdec_b = jnp.exp(row * lgb)
    qdec_f = jnp.exp((row + 1.0) * lgf)
    qdec_b = jnp.exp((c_len - row) * lgb)
    chunk_f = jnp.exp(jnp.full((1, RET_DIM), c_len, F32) * lgf)
    chunk_b = jnp.exp(jnp.full((1, RET_DIM), c_len, F32) * lgb)
    diff = (lax.broadcasted_iota(jnp.int32, (c_len, c_len), 0)
            - lax.broadcasted_iota(jnp.int32, (c_len, c_len), 1)).astype(F32)
    decay = (jnp.where(diff >= 0, jnp.exp(jnp.maximum(diff, 0.0) * lgf), 0.0)
             + jnp.where(diff <= 0, jnp.exp(jnp.maximum(-diff, 0.0) * lgb), 0.0))

    def kv_outer(k, v, dec_f, dec_b):
        vf = v.astype(F32)
        v2 = jnp.concatenate([(vf * dec_f).astype(BF16), (vf * dec_b).astype(BF16)], axis=1)
        return lax.dot_general(k, v2, (((0,), (0,)), ((), ())), preferred_element_type=F32)

    crow = lax.broadcasted_iota(jnp.int32, (n_ctx, 1), 0).astype(F32)
    s0 = kv_outer(kc_ref[0], vc_ref[0], jnp.exp((n_ctx - 1.0 - crow) * lgf), jnp.exp(crow * lgb))

    def back_body(t, sb):
        c = n_chunks - 1 - t
        start = pl.multiple_of(c * c_len, c_len)
        sb_sc[c] = sb
        u = kv_outer(k_ref[0, pl.ds(start, c_len), :], v_ref[0, pl.ds(start, c_len), :], vdec_f, vdec_b)
        uf_sc[c] = u[:, :RET_DIM]
        return sb * chunk_b + u[:, RET_DIM:]

    lax.fori_loop(0, n_chunks, back_body, s0[:, RET_DIM:], unroll=RET_UNROLL)

    def fwd_body(c, sf):
        start = pl.multiple_of(c * c_len, c_len)
        q = q_ref[0, pl.ds(start, c_len), :]
        k = k_ref[0, pl.ds(start, c_len), :]
        v = v_ref[0, pl.ds(start, c_len), :]
        inner = lax.dot_general(q, k, (((1,), (1,)), ((), ())), preferred_element_type=F32)
        y = jnp.dot((inner * decay).astype(BF16), v, preferred_element_type=F32)
        states = jnp.concatenate([sf.astype(BF16), sb_sc[c].astype(BF16)], axis=1)
        cross = jnp.dot(q, states, preferred_element_type=F32)
        y = y + cross[:, :RET_DIM] * qdec_f + cross[:, RET_DIM:] * qdec_b
        gate = g_ref[0, pl.ds(start, c_len), :]
        o_ref[0, pl.ds(start, c_len), :] = (gate * _sigmoid(gate) * _rms(y)).astype(BF16)
        return sf * chunk_f + uf_sc[c]

    lax.fori_loop(0, n_chunks, fwd_body, s0[:, :RET_DIM], unroll=RET_UNROLL)


def _retention(log_gamma, ret_x, gate_x, ret_c):
    b, l, _ = ret_x.shape
    lc = ret_c.shape[1]
    n_chunks = l // RET_CHUNK
    seq = lambda off: pl.BlockSpec((1, l, RET_DIM), lambda bi, hi: (bi, 0, hi + off))
    ctx = lambda off: pl.BlockSpec((1, lc, RET_DIM), lambda bi, hi: (bi, 0, hi + off))
    return pl.pallas_call(
        _ret_kernel,
        grid=(b, RET_HEADS),
        in_specs=[
            pl.BlockSpec(memory_space=pltpu.SMEM),
            seq(0), seq(RET_HEADS), seq(2 * RET_HEADS), seq(0),
            ctx(RET_HEADS), ctx(2 * RET_HEADS),
        ],
        out_specs=seq(0),
        out_shape=jax.ShapeDtypeStruct((b, l, RET_WIDTH), BF16),
        scratch_shapes=[
            pltpu.VMEM((n_chunks, RET_DIM, RET_DIM), F32),
            pltpu.VMEM((n_chunks, RET_DIM, RET_DIM), F32),
        ],
        compiler_params=_params("parallel", "arbitrary"),
        name="retention",
    )(log_gamma, ret_x, ret_x, ret_x, gate_x, ret_c, ret_c)


def _merge_kernel(h_ref, n_ref, ya_ref, yr_ref, mod_ref, w_ref, wo_ref, out_ref, acc_sc):
    j = pl.program_id(1)
    last = pl.num_programs(1) - 1
    r1, r2, r3 = D_MODEL, 2 * D_MODEL, 2 * D_MODEL + ATTN_WIDTH

    def step(first):
        n = n_ref[...]
        ga = jnp.dot(n, w_ref[:r1, :], preferred_element_type=F32)
        gb = jnp.dot(n, w_ref[r1:r2, :], preferred_element_type=F32)
        pa = jnp.dot(ya_ref[...], w_ref[r2:r3, :], preferred_element_type=F32)
        pr = jnp.dot(yr_ref[...], w_ref[r3:, :], preferred_element_type=F32)
        z = (_sigmoid(ga) * pa + _sigmoid(gb) * pr).astype(BF16)
        partial = jnp.dot(z, wo_ref[...], preferred_element_type=F32)
        if first:
            acc_sc[...] = partial
        else:
            acc_sc[...] += partial

    @pl.when(j == 0)
    def _():
        step(first=True)

    @pl.when((j > 0) & (j < last))
    def _():
        step(first=False)

    @pl.when(j == last)
    def _():
        step(first=False)
        out_ref[...] = h_ref[...] + mod_ref[0, 5, 0:1, :] * acc_sc[...]


def _merge_weight_blocks(w_ga, w_gb, w_pa, w_pr):
    blocks = [jnp.transpose(w.reshape(w.shape[0], D_MODEL // MERGE_TILE, MERGE_TILE), (1, 0, 2))
              for w in (w_ga, w_gb, w_pa, w_pr)]
    return jnp.concatenate(blocks, axis=1).astype(BF16)


def _merge(h, n, ya, yr, mod, w_in, w_out, *, tiles_per_row):
    t = h.shape[0]
    tm, tc = TOKEN_TILE, MERGE_TILE
    return pl.pallas_call(
        _merge_kernel,
        grid=(t // tm, D_MODEL // tc),
        in_specs=[
            pl.BlockSpec((tm, D_MODEL), lambda i, j: (i, 0)),
            pl.BlockSpec((tm, D_MODEL), lambda i, j: (i, 0)),
            pl.BlockSpec((tm, ATTN_WIDTH), lambda i, j: (i, 0)),
            pl.BlockSpec((tm, RET_WIDTH), lambda i, j: (i, 0)),
            pl.BlockSpec((1, N_MOD, ROW_BLOCK, D_MODEL), lambda i, j: (i // tiles_per_row, 0, 0, 0)),
            pl.BlockSpec((None, w_in.shape[1], tc), lambda i, j: (j, 0, 0)),
            pl.BlockSpec((tc, D_MODEL), lambda i, j: (j, 0)),
        ],
        out_specs=pl.BlockSpec((tm, D_MODEL), lambda i, j: (i, 0)),
        out_shape=jax.ShapeDtypeStruct((t, D_MODEL), F32),
        scratch_shapes=[pltpu.VMEM((tm, D_MODEL), F32)],
        compiler_params=_params("parallel", "arbitrary"),
        name="merge",
    )(h, n, ya, yr, mod, w_in, w_out)


def _rope_tables(seq_len):
    rows = seq_len // GRID_W
    row = jnp.repeat(jnp.arange(rows, dtype=F32), GRID_W)
    col = jnp.tile(jnp.arange(GRID_W, dtype=F32), rows)
    half = HEAD_DIM // 2
    inv_freq = ROPE_THETA ** (-jnp.arange(0, half, 2, dtype=F32) / half)
    ang = jnp.concatenate([row[:, None] * inv_freq, col[:, None] * inv_freq], axis=-1)
    cos, sin = jnp.cos(ang), jnp.sin(ang)
    return jnp.concatenate([cos, cos], axis=-1), jnp.concatenate([-sin, sin], axis=-1)


def _deinterleave(t):
    lead = t.shape[:-1]
    t = t.reshape(lead + (-1, HEAD_DIM // 2, 2))
    return jnp.swapaxes(t, -1, -2).reshape(lead + (-1,))


def kernel(x, c, ctx, c_ctx, w_ada, b_ada, ffn1_w_in, ffn1_w_out, mix_w_in, attn_q_gain, attn_k_gain,
           ret_decay_logit, w_proj_attn, w_proj_ret, mix_w_out, ffn2_w_in, ffn2_w_out, final_norm):
    batch, seq_len, _ = x.shape
    assert w_ada.shape[0] == 1, "single-layer block"
    assert seq_len % TOKEN_TILE == 0 and (batch * CTX_LEN) % TOKEN_TILE == 0
    tiles_per_row = seq_len // TOKEN_TILE

    w1_in, w1_out = ffn1_w_in[0].astype(BF16), ffn1_w_out[0].astype(BF16)
    w2_in, w2_out = ffn2_w_in[0].astype(BF16), ffn2_w_out[0].astype(BF16)
    w_mix = mix_w_in[0]
    w_qkv = jnp.concatenate([_deinterleave(w_mix[:, :ATTN_WIDTH + KV_WIDTH]),
                             w_mix[:, ATTN_WIDTH + KV_WIDTH:QKV_WIDTH]], axis=1).astype(BF16)
    w_ret = w_mix[:, RET_OFF:GR_OFF].astype(BF16)
    w_gr = w_mix[:, GR_OFF:GA_OFF].astype(BF16)
    w_merge = _merge_weight_blocks(w_mix[:, GA_OFF:GB_OFF], w_mix[:, GB_OFF:], w_proj_attn[0], w_proj_ret[0])
    w_mo = mix_w_out[0].astype(BF16)

    cond = jnp.zeros((8, D_MODEL), F32).at[:batch].set(c).at[batch].set(c_ctx)
    mod = _adaln(cond, w_ada[0], b_ada).reshape(8, N_MOD, 1, D_MODEL)
    mod = jnp.broadcast_to(mod, (8, N_MOD, ROW_BLOCK, D_MODEL))

    x2 = x.reshape(batch * seq_len, D_MODEL)
    c2 = ctx.reshape(batch * CTX_LEN, D_MODEL)
    h1, n2 = _ffn(x2, mod, w1_in, w1_out, mod_base=0, rows_per_mod=seq_len, row_offset=0,
                  tm=TOKEN_TILE, tf=FF_TILE, emit_next=True)
    n2c, = _ffn(c2, mod, w1_in, w1_out, mod_base=0, rows_per_mod=batch * CTX_LEN, row_offset=batch,
                tm=batch * CTX_LEN, tf=FF_TILE, emit_h=False, emit_next=True)

    cos, sin = _rope_tables(seq_len)
    gq = _deinterleave(attn_q_gain[0].reshape(1, HEAD_DIM))
    gk = _deinterleave(attn_k_gain[0].reshape(1, HEAD_DIM))
    qkv_x = _qkv_proj(n2, w_qkv, cos, sin, gq, gk, pos_tiles=tiles_per_row)
    ones = jnp.ones((TOKEN_TILE, HEAD_DIM), F32)
    qkv_c = _qkv_proj(n2c, w_qkv, ones, jnp.zeros_like(ones), gq, gk, pos_tiles=1)

    ret_scale = jnp.concatenate([jnp.ones((1, RET_WIDTH), F32),
                                 jnp.full((1, RET_WIDTH), RET_DIM ** -0.5, F32),
                                 jnp.ones((1, RET_WIDTH), F32)], axis=1)
    ret_x = _scaled_mm(n2, w_ret, ret_scale, BF16, "ret_proj")
    ret_c = _scaled_mm(n2c, w_ret, ret_scale, BF16, "ret_proj")
    gate_x = _scaled_mm(n2, w_gr, jnp.ones((1, RET_WIDTH), F32), F32, "ret_gate_proj")

    qkv_x = qkv_x.reshape(batch, seq_len, QKV_WIDTH)
    qkv_c = qkv_c.reshape(batch, CTX_LEN, QKV_WIDTH)
    score_bound = HEAD_DIM ** 0.5 * jnp.max(jnp.abs(gq)) * jnp.max(jnp.abs(gk))
    ya = _attention((score_bound <= ATTN_SAFE_SCORE).astype(jnp.int32).reshape(1), qkv_x, qkv_c)

    log_gamma = jax.nn.log_sigmoid(ret_decay_logit[0].astype(F32))
    yr = _retention(log_gamma, ret_x.reshape(batch, seq_len, 3 * RET_WIDTH),
                    gate_x.reshape(batch, seq_len, RET_WIDTH),
                    ret_c.reshape(batch, CTX_LEN, 3 * RET_WIDTH))

    h2 = _merge(h1, n2, ya.reshape(batch * seq_len, ATTN_WIDTH), yr.reshape(batch * seq_len, RET_WIDTH), mod,
                w_merge, w_mo, tiles_per_row=tiles_per_row)
    out = _ffn(h2, mod, w2_in, w2_out, mod_base=6, rows_per_mod=seq_len, row_offset=0,
               tm=TOKEN_TILE, tf=FF_TILE,
               final_norm=jnp.broadcast_to(final_norm.reshape(1, D_MODEL), (ROW_BLOCK, D_MODEL)))
    return out[0].reshape(batch, seq_len, D_MODEL)
```

```python
import functools

import jax
import jax.numpy as jnp
from jax import lax
from jax.experimental import pallas as pl
from jax.experimental.pallas import tpu as pltpu

D_MODEL = 2048
CTX_LEN = 256
GRID_W = 64
HEAD_DIM = 128
LANES = 128
ATTN_Q_HEADS = 8
ATTN_KV_HEADS = 2
ATTN_GROUPS = ATTN_Q_HEADS // ATTN_KV_HEADS
ATTN_WIDTH = ATTN_Q_HEADS * HEAD_DIM
KV_WIDTH = ATTN_KV_HEADS * HEAD_DIM
RET_HEADS = 8
RET_DIM = 128
RET_WIDTH = RET_HEADS * RET_DIM
D_FF = 5632
ROPE_THETA = 10000.0
NORM_EPS = 1e-6
N_MOD = 9

QKV_WIDTH = ATTN_WIDTH + 2 * KV_WIDTH
RET_OFF = QKV_WIDTH
GR_OFF = RET_OFF + 3 * RET_WIDTH
GA_OFF = GR_OFF + RET_WIDTH
GB_OFF = GA_OFF + D_MODEL

F32 = jnp.float32
BF16 = jnp.bfloat16

VMEM_LIMIT_BYTES = 63 * 1024 * 1024

ADALN_TILE = 1024
TOKEN_TILE = 512
FF_TILE = 512
FFN_TOKEN_TILE = 1024
ROW_BLOCK = 16
ROW_UNROLL = 4
MERGE_TILE = 512
ATTN_Q_TILE = 512
ATTN_KV_TILE = 768
RET_CHUNK = 256
RET_UNROLL = 8

Q_SCALE = HEAD_DIM ** -0.5 * 1.4426950408889634
ATTN_SAFE_SCORE = 32.0


def _params(*sem):
    return pltpu.CompilerParams(dimension_semantics=sem, vmem_limit_bytes=VMEM_LIMIT_BYTES)


def _rms(x):
    return x * lax.rsqrt(jnp.mean(x * x, axis=-1, keepdims=True) + NORM_EPS)


def _sigmoid(x):
    return 1.0 / (1.0 + jnp.exp(-x))


def _adaln_kernel(c_ref, w_ref, b_ref, o_ref):
    chunk = pl.program_id(0) // (D_MODEL // ADALN_TILE)
    c = c_ref[...]
    s = (c * _sigmoid(c)).astype(BF16)
    y = jnp.dot(s, w_ref[...].astype(BF16), preferred_element_type=F32) + b_ref[...]
    y = y + jnp.where(chunk % 3 == 1, 1.0, 0.0)
    o_ref[...] = y * jnp.where((chunk == 2) | (chunk == 8), 0.5, 1.0)


def _adaln(cond, w, b):
    n = w.shape[1]
    tn = ADALN_TILE
    return pl.pallas_call(
        _adaln_kernel,
        grid=(n // tn,),
        in_specs=[
            pl.BlockSpec((8, D_MODEL), lambda j: (0, 0)),
            pl.BlockSpec((D_MODEL, tn), lambda j: (0, j)),
            pl.BlockSpec((1, tn), lambda j: (0, j)),
        ],
        out_specs=pl.BlockSpec((8, tn), lambda j: (0, j)),
        out_shape=jax.ShapeDtypeStruct((8, n), F32),
        compiler_params=_params("arbitrary"),
        name="adaln",
    )(cond, w, b)


def _ffn_kernel(*refs, mod_base, emit_h, emit_next, final):
    h_ref, mod_ref, wa_ref, wb_ref, wo_ref = refs[:5]
    refs = refs[5:]
    if final:
        fn_ref, refs = refs[0], refs[1:]
    if emit_h:
        out_ref, refs = refs[0], refs[1:]
    if emit_next:
        nxt_ref, refs = refs[0], refs[1:]
    xn_sc, inv_sc = refs[:2]
    acc_ref = out_ref if emit_h else refs[2]
    j = pl.program_id(1)
    n_row_blocks = h_ref.shape[0] // ROW_BLOCK

    lane_tiles = [slice(k, k + LANES) for k in range(0, D_MODEL, LANES)]

    def row_block(r):
        return pl.ds(pl.multiple_of(r * ROW_BLOCK, ROW_BLOCK), ROW_BLOCK)

    def inv_rms(x):
        inv = lax.rsqrt(jnp.mean(x * x, axis=-1, keepdims=True) + NORM_EPS)
        return jnp.broadcast_to(inv, (x.shape[0], LANES))

    def for_row_blocks(body, unroll):
        def step(r, carry):
            body(row_block(r))
            return carry

        lax.fori_loop(0, n_row_blocks, step, 0, unroll=unroll)

    def prologue():
        def stats(rows):
            inv_sc[rows, :] = inv_rms(h_ref[rows, :])

        def prenorm(rows):
            inv = inv_sc[rows, :]
            for sl in lane_tiles:
                n = h_ref[rows, sl] * inv * mod_ref[0, mod_base + 1, :, sl] + mod_ref[0, mod_base, :, sl]
                xn_sc[rows, sl] = n.astype(BF16)

        for_row_blocks(stats, True)
        for_row_blocks(prenorm, ROW_UNROLL)

    def matmuls(first):
        xn = xn_sc[...]
        half = wa_ref.shape[1] // 2
        partial = None
        for sl in (slice(0, half), slice(half, 2 * half)):
            a = jnp.dot(xn, wa_ref[:, sl], preferred_element_type=F32)
            b = jnp.dot(xn, wb_ref[:, sl], preferred_element_type=F32)
            act = (a * _sigmoid(a) * b).astype(BF16)
            p = jnp.dot(act, wo_ref[sl, :], preferred_element_type=F32)
            partial = p if partial is None else partial + p
        if first:
            acc_ref[...] = partial
        else:
            acc_ref[...] += partial

    def epilogue():
        def residual(rows):
            h = h_ref[rows, :] + mod_ref[0, mod_base + 2] * acc_ref[rows, :]
            acc_ref[rows, :] = h
            inv_sc[rows, :] = inv_rms(h)

        def postnorm(rows):
            inv = inv_sc[rows, :]
            for sl in lane_tiles:
                n = acc_ref[rows, sl] * inv
                if final:
                    out_ref[rows, sl] = n * fn_ref[:, sl]
                else:
                    n = n * mod_ref[0, mod_base + 4, :, sl] + mod_ref[0, mod_base + 3, :, sl]
                    nxt_ref[rows, sl] = n.astype(BF16)

        for_row_blocks(residual, True)
        if final or emit_next:
            for_row_blocks(postnorm, ROW_UNROLL)

    last = pl.num_programs(1) - 1

    @pl.when(j == 0)
    def _():
        prologue()
        matmuls(first=True)

    @pl.when((j > 0) & (j < last))
    def _():
        matmuls(first=False)

    @pl.when(j == last)
    def _():
        matmuls(first=False)
        epilogue()


def _ffn(h, mod, w_in, w_out, *, mod_base, rows_per_mod, row_offset, tm, tf, emit_h=True, emit_next=False,
         final_norm=None):
    t = h.shape[0]
    nf = D_FF // tf
    final = final_norm is not None
    tiles_per_row = rows_per_mod // tm
    mod_map = lambda i, j: (i // tiles_per_row + row_offset, 0, 0, 0)
    in_specs = [
        pl.BlockSpec((tm, D_MODEL), lambda i, j: (i, 0)),
        pl.BlockSpec((1, N_MOD, ROW_BLOCK, D_MODEL), mod_map),
        pl.BlockSpec((D_MODEL, tf), lambda i, j: (0, j)),
        pl.BlockSpec((D_MODEL, tf), lambda i, j: (0, j + nf)),
        pl.BlockSpec((tf, D_MODEL), lambda i, j: (j, 0)),
    ]
    args = [h, mod, w_in, w_in, w_out]
    if final:
        in_specs.append(pl.BlockSpec((ROW_BLOCK, D_MODEL), lambda i, j: (0, 0)))
        args.append(final_norm)
    out_specs, out_shape = [], []
    if emit_h:
        out_specs.append(pl.BlockSpec((tm, D_MODEL), lambda i, j: (i, 0)))
        out_shape.append(jax.ShapeDtypeStruct((t, D_MODEL), F32))
    if emit_next:
        nxt_mode = pl.Buffered(1) if emit_h else None
        out_specs.append(pl.BlockSpec((tm, D_MODEL), lambda i, j: (i, 0), pipeline_mode=nxt_mode))
        out_shape.append(jax.ShapeDtypeStruct((t, D_MODEL), BF16))
    return pl.pallas_call(
        functools.partial(_ffn_kernel, mod_base=mod_base, emit_h=emit_h, emit_next=emit_next, final=final),
        grid=(t // tm, nf),
        in_specs=in_specs,
        out_specs=out_specs,
        out_shape=out_shape,
        scratch_shapes=[pltpu.VMEM((tm, D_MODEL), BF16), pltpu.VMEM((tm, LANES), F32)]
        + ([] if emit_h else [pltpu.VMEM((tm, D_MODEL), F32)]),
        compiler_params=_params("parallel", "arbitrary"),
        name="ffn_final" if final else "ffn",
    )(*args)


def _qkv_kernel(n_ref, w_ref, cos_ref, sin_ref, gq_ref, gk_ref, o_ref, *y_bufs, n_tiles):
    i = pl.program_id(0)

    def finish(y_ref):
        cos = cos_ref[...]
        sin = sin_ref[...]
        gq = gq_ref[...] * Q_SCALE
        gk = gk_ref[...]
        for hh in range(ATTN_Q_HEADS + ATTN_KV_HEADS):
            sl = slice(hh * HEAD_DIM, (hh + 1) * HEAD_DIM)
            t = _rms(y_ref[:, sl]) * (gq if hh < ATTN_Q_HEADS else gk)
            o_ref[:, sl] = (t * cos + pltpu.roll(t, HEAD_DIM // 2, 1) * sin).astype(BF16)
        o_ref[:, ATTN_WIDTH + KV_WIDTH:] = y_ref[:, ATTN_WIDTH + KV_WIDTH:].astype(BF16)

    @pl.when(i == 0)
    def _():
        y_bufs[1][...] = jnp.zeros_like(y_bufs[1])

    for parity in range(2):
        @pl.when((i < n_tiles) & (i % 2 == parity))
        def _(y_cur=y_bufs[parity], y_prev=y_bufs[1 - parity]):
            finish(y_prev)
            y_cur[...] = jnp.dot(n_ref[...], w_ref[...], preferred_element_type=F32)

    @pl.when(i == n_tiles)
    def _():
        finish(y_bufs[1 - n_tiles % 2])


def _qkv_proj(n, w, cos, sin, gq, gk, *, pos_tiles):
    t = n.shape[0]
    tm = TOKEN_TILE
    n_tiles = t // tm
    prev_tile = lambda i: jnp.maximum(i - 1, 0)
    return pl.pallas_call(
        functools.partial(_qkv_kernel, n_tiles=n_tiles),
        grid=(n_tiles + 1,),
        in_specs=[
            pl.BlockSpec((tm, D_MODEL), lambda i: (jnp.minimum(i, n_tiles - 1), 0)),
            pl.BlockSpec((D_MODEL, QKV_WIDTH), lambda i: (0, 0)),
            pl.BlockSpec((tm, HEAD_DIM), lambda i: (prev_tile(i) % pos_tiles, 0)),
            pl.BlockSpec((tm, HEAD_DIM), lambda i: (prev_tile(i) % pos_tiles, 0)),
            pl.BlockSpec((1, HEAD_DIM), lambda i: (0, 0)),
            pl.BlockSpec((1, HEAD_DIM), lambda i: (0, 0)),
        ],
        out_specs=pl.BlockSpec((tm, QKV_WIDTH), lambda i: (prev_tile(i), 0)),
        out_shape=jax.ShapeDtypeStruct((t, QKV_WIDTH), BF16),
        scratch_shapes=[pltpu.VMEM((tm, QKV_WIDTH), F32), pltpu.VMEM((tm, QKV_WIDTH), F32)],
        compiler_params=_params("arbitrary"),
        name="qkv_proj",
    )(n, w, cos, sin, gq, gk)


def _scaled_mm_kernel(n_ref, w_ref, s_ref, o_ref):
    y = jnp.dot(n_ref[...], w_ref[...], preferred_element_type=F32)
    o_ref[...] = (y * s_ref[...]).astype(o_ref.dtype)


def _scaled_mm(n, w, col_scale, out_dtype, name):
    t = n.shape[0]
    nn = w.shape[1]
    tm = TOKEN_TILE
    return pl.pallas_call(
        _scaled_mm_kernel,
        grid=(t // tm,),
        in_specs=[
            pl.BlockSpec((tm, D_MODEL), lambda i: (i, 0)),
            pl.BlockSpec((D_MODEL, nn), lambda i: (0, 0)),
            pl.BlockSpec((1, nn), lambda i: (0, 0)),
        ],
        out_specs=pl.BlockSpec((tm, nn), lambda i: (i, 0)),
        out_shape=jax.ShapeDtypeStruct((t, nn), out_dtype),
        compiler_params=_params("parallel"),
        name=name,
    )(n, w, col_scale)


def _attn_kernel(flag_ref, q_ref, kc_ref, kx_ref, vc_ref, vx_ref, o_ref,
                 k_sc, v_sc, qs_sc, acc_sc, m_sc):
    tq = q_ref.shape[1]
    tk = ATTN_KV_TILE
    n_ctx = kc_ref.shape[1]
    lk = k_sc.shape[0]
    nk = lk // tk

    @pl.when(pl.program_id(2) == 0)
    def _():
        k_sc[:n_ctx, :] = kc_ref[0]
        k_sc[n_ctx:, :] = kx_ref[0]
        v_sc[:n_ctx, :HEAD_DIM] = vc_ref[0]
        v_sc[n_ctx:, :HEAD_DIM] = vx_ref[0]
        lane = lax.broadcasted_iota(jnp.int32, (n_ctx, HEAD_DIM), 1)
        ones_col = jnp.where(lane == 0, 1.0, 0.0).astype(BF16)
        for r in range(lk // n_ctx):
            v_sc[r * n_ctx:(r + 1) * n_ctx, HEAD_DIM:] = ones_col

    for g in range(ATTN_GROUPS):
        qs_sc[g * tq:(g + 1) * tq, :] = q_ref[0, :, g * HEAD_DIM:(g + 1) * HEAD_DIM]

    def scores(c):
        start = pl.multiple_of(c * tk, tk)
        k = k_sc[pl.ds(start, tk), :]
        s = lax.dot_general(qs_sc[...], k, (((1,), (1,)), ((), ())), preferred_element_type=F32)
        return s, v_sc[pl.ds(start, tk), :]

    def finish():
        acc = acc_sc[...]
        out = acc[:, :HEAD_DIM] / acc[:, HEAD_DIM:HEAD_DIM + 1]
        for g in range(ATTN_GROUPS):
            o_ref[0, :, g * HEAD_DIM:(g + 1) * HEAD_DIM] = out[g * tq:(g + 1) * tq, :].astype(BF16)

    @pl.when(flag_ref[0] != 0)
    def _():
        acc_sc[...] = jnp.zeros_like(acc_sc)

        def body(c, carry):
            s, v = scores(c)
            acc_sc[...] += jnp.dot(jnp.exp2(s).astype(BF16), v, preferred_element_type=F32)
            return carry

        lax.fori_loop(0, nk, body, 0, unroll=True)
        finish()

    @pl.when(flag_ref[0] == 0)
    def _():
        acc_sc[...] = jnp.zeros_like(acc_sc)
        m_sc[...] = jnp.full_like(m_sc, -jnp.inf)

        def body(c, carry):
            s, v = scores(c)
            m_prev = m_sc[...]
            m_new = jnp.maximum(m_prev, jnp.max(s, axis=-1, keepdims=True))
            p = jnp.exp2(s - m_new).astype(BF16)
            acc_sc[...] = jnp.exp2(m_prev - m_new) * acc_sc[...] + jnp.dot(p, v, preferred_element_type=F32)
            m_sc[...] = m_new
            return carry

        lax.fori_loop(0, nk, body, 0)
        finish()


def _attention(bounded_flag, qkv_x, qkv_c):
    b, l, _ = qkv_x.shape
    lc = qkv_c.shape[1]
    lk = lc + l
    tq = ATTN_Q_TILE
    gw = ATTN_GROUPS * HEAD_DIM
    rows = ATTN_GROUPS * tq
    k_col = ATTN_WIDTH // HEAD_DIM
    v_col = k_col + ATTN_KV_HEADS
    lat = lambda col: pl.BlockSpec((1, l, HEAD_DIM), lambda bi, hi, qi: (bi, 0, hi + col))
    ctx = lambda col: pl.BlockSpec((1, lc, HEAD_DIM), lambda bi, hi, qi: (bi, 0, hi + col))
    return pl.pallas_call(
        _attn_kernel,
        grid=(b, ATTN_KV_HEADS, l // tq),
        in_specs=[
            pl.BlockSpec(memory_space=pltpu.SMEM),
            pl.BlockSpec((1, tq, gw), lambda bi, hi, qi: (bi, qi, hi)),
            ctx(k_col), lat(k_col), ctx(v_col), lat(v_col),
        ],
        out_specs=pl.BlockSpec((1, tq, gw), lambda bi, hi, qi: (bi, qi, hi)),
        out_shape=jax.ShapeDtypeStruct((b, l, ATTN_WIDTH), BF16),
        scratch_shapes=[
            pltpu.VMEM((lk, HEAD_DIM), BF16),
            pltpu.VMEM((lk, 2 * HEAD_DIM), BF16),
            pltpu.VMEM((rows, HEAD_DIM), BF16),
            pltpu.VMEM((rows, 2 * HEAD_DIM), F32),
            pltpu.VMEM((rows, 1), F32),
        ],
        compiler_params=_params("parallel", "parallel", "arbitrary"),
        name="attention",
    )(bounded_flag, qkv_x, qkv_c, qkv_x, qkv_c, qkv_x)


def _ret_kernel(lg_ref, q_ref, k_ref, v_ref, g_ref, kc_ref, vc_ref, o_ref, uf_sc, sb_sc):
    c_len = RET_CHUNK
    n_chunks = q_ref.shape[1] // c_len
    n_ctx = kc_ref.shape[1]
    head = pl.program_id(1)
    lgf = lg_ref[0, head]
    lgb = lg_ref[1, head]

    row = lax.broadcasted_iota(jnp.int32, (c_len, 1), 0).astype(F32)
    vdec_f = jnp.exp((c_len - 1.0 - row) * lgf)
    vdec_b = jnp.exp(row * lgb)
    qdec_f = jnp.exp((row + 1.0) * lgf)
    qdec_b = jnp.exp((c_len - row) * lgb)
    chunk_f = jnp.exp(jnp.full((1, RET_DIM), c_len, F32) * lgf)
    chunk_b = jnp.exp(jnp.full((1, RET_DIM), c_len, F32) * lgb)
    diff = (lax.broadcasted_iota(jnp.int32, (c_len, c_len), 0)
            - lax.broadcasted_iota(jnp.int32, (c_len, c_len), 1)).astype(F32)
    decay = (jnp.where(diff >= 0, jnp.exp(jnp.maximum(diff, 0.0) * lgf), 0.0)
             + jnp.where(diff <= 0, jnp.exp(jnp.maximum(-diff, 0.0) * lgb), 0.0))

    def kv_outer(k, v, dec_f, dec_b):
        vf = v.astype(F32)
        v2 = jnp.concatenate([(vf * dec_f).astype(BF16), (vf * dec_b).astype(BF16)], axis=1)
        return lax.dot_general(k, v2, (((0,), (0,)), ((), ())), preferred_element_type=F32)

    crow = lax.broadcasted_iota(jnp.int32, (n_ctx, 1), 0).astype(F32)
    s0 = kv_outer(kc_ref[0], vc_ref[0], jnp.exp((n_ctx - 1.0 - crow) * lgf), jnp.exp(crow * lgb))

    def back_body(t, sb):
        c = n_chunks - 1 - t
        start = pl.multiple_of(c * c_len, c_len)
        sb_sc[c] = sb
        u = kv_outer(k_ref[0, pl.ds(start, c_len), :], v_ref[0, pl.ds(start, c_len), :], vdec_f, vdec_b)
        uf_sc[c] = u[:, :RET_DIM]
        return sb * chunk_b + u[:, RET_DIM:]

    lax.fori_loop(0, n_chunks, back_body, s0[:, RET_DIM:], unroll=RET_UNROLL)

    def fwd_body(c, sf):
        start = pl.multiple_of(c * c_len, c_len)
        q = q_ref[0, pl.ds(start, c_len), :]
        k = k_ref[0, pl.ds(start, c_len), :]
        v = v_ref[0, pl.ds(start, c_len), :]
        inner = lax.dot_general(q, k, (((1,), (1,)), ((), ())), preferred_element_type=F32)
        y = jnp.dot((inner * decay).astype(BF16), v, preferred_element_type=F32)
        states = jnp.concatenate([sf.astype(BF16), sb_sc[c].astype(BF16)], axis=1)
        cross = jnp.dot(q, states, preferred_element_type=F32)
        y = y + cross[:, :RET_DIM] * qdec_f + cross[:, RET_DIM:] * qdec_b
        gate = g_ref[0, pl.ds(start, c_len), :]
        o_ref[0, pl.ds(start, c_len), :] = (gate * _sigmoid(gate) * _rms(y)).astype(BF16)
        return sf * chunk_f + uf_sc[c]

    lax.fori_loop(0, n_chunks, fwd_body, s0[:, :RET_DIM], unroll=RET_UNROLL)


def _retention(log_gamma, ret_x, gate_x, ret_c):
    b, l, _ = ret_x.shape
    lc = ret_c.shape[1]
    n_chunks = l // RET_CHUNK
    seq = lambda off: pl.BlockSpec((1, l, RET_DIM), lambda bi, hi: (bi, 0, hi + off))
    ctx = lambda off: pl.BlockSpec((1, lc, RET_DIM), lambda bi, hi: (bi, 0, hi + off))
    return pl.pallas_call(
        _ret_kernel,
        grid=(b, RET_HEADS),
        in_specs=[
            pl.BlockSpec(memory_space=pltpu.SMEM),
            seq(0), seq(RET_HEADS), seq(2 * RET_HEADS), seq(0),
            ctx(RET_HEADS), ctx(2 * RET_HEADS),
        ],
        out_specs=seq(0),
        out_shape=jax.ShapeDtypeStruct((b, l, RET_WIDTH), BF16),
        scratch_shapes=[
            pltpu.VMEM((n_chunks, RET_DIM, RET_DIM), F32),
            pltpu.VMEM((n_chunks, RET_DIM, RET_DIM), F32),
        ],
        compiler_params=_params("parallel", "arbitrary"),
        name="retention",
    )(log_gamma, ret_x, ret_x, ret_x, gate_x, ret_c, ret_c)


def _merge_kernel(h_ref, n_ref, ya_ref, yr_ref, mod_ref, wga_ref, wgb_ref, wpa_ref, wpr_ref, wo_ref,
                  out_ref, acc_sc):
    j = pl.program_id(1)
    last = pl.num_programs(1) - 1

    def step(first):
        n = n_ref[...]
        ga = jnp.dot(n, wga_ref[...], preferred_element_type=F32)
        gb = jnp.dot(n, wgb_ref[...], preferred_element_type=F32)
        pa = jnp.dot(ya_ref[...], wpa_ref[...], preferred_element_type=F32)
        pr = jnp.dot(yr_ref[...], wpr_ref[...], preferred_element_type=F32)
        z = (_sigmoid(ga) * pa + _sigmoid(gb) * pr).astype(BF16)
        partial = jnp.dot(z, wo_ref[...], preferred_element_type=F32)
        if first:
            acc_sc[...] = partial
        else:
            acc_sc[...] += partial

    @pl.when(j == 0)
    def _():
        step(first=True)

    @pl.when((j > 0) & (j < last))
    def _():
        step(first=False)

    @pl.when(j == last)
    def _():
        step(first=False)
        out_ref[...] = h_ref[...] + mod_ref[0, 5, 0:1, :] * acc_sc[...]


def _merge(h, n, ya, yr, mod, w_ga, w_gb, w_pa, w_pr, w_out, *, tiles_per_row):
    t = h.shape[0]
    tm, tc = TOKEN_TILE, MERGE_TILE
    return pl.pallas_call(
        _merge_kernel,
        grid=(t // tm, D_MODEL // tc),
        in_specs=[
            pl.BlockSpec((tm, D_MODEL), lambda i, j: (i, 0)),
            pl.BlockSpec((tm, D_MODEL), lambda i, j: (i, 0)),
            pl.BlockSpec((tm, ATTN_WIDTH), lambda i, j: (i, 0)),
            pl.BlockSpec((tm, RET_WIDTH), lambda i, j: (i, 0)),
            pl.BlockSpec((1, N_MOD, ROW_BLOCK, D_MODEL), lambda i, j: (i // tiles_per_row, 0, 0, 0)),
            pl.BlockSpec((D_MODEL, tc), lambda i, j: (0, j)),
            pl.BlockSpec((D_MODEL, tc), lambda i, j: (0, j)),
            pl.BlockSpec((ATTN_WIDTH, tc), lambda i, j: (0, j)),
            pl.BlockSpec((RET_WIDTH, tc), lambda i, j: (0, j)),
            pl.BlockSpec((tc, D_MODEL), lambda i, j: (j, 0)),
        ],
        out_specs=pl.BlockSpec((tm, D_MODEL), lambda i, j: (i, 0)),
        out_shape=jax.ShapeDtypeStruct((t, D_MODEL), F32),
        scratch_shapes=[pltpu.VMEM((tm, D_MODEL), F32)],
        compiler_params=_params("parallel", "arbitrary"),
        name="merge",
    )(h, n, ya, yr, mod, w_ga, w_gb, w_pa, w_pr, w_out)


def _rope_tables(seq_len):
    rows = seq_len // GRID_W
    row = jnp.repeat(jnp.arange(rows, dtype=F32), GRID_W)
    col = jnp.tile(jnp.arange(GRID_W, dtype=F32), rows)
    half = HEAD_DIM // 2
    inv_freq = ROPE_THETA ** (-jnp.arange(0, half, 2, dtype=F32) / half)
    ang = jnp.concatenate([row[:, None] * inv_freq, col[:, None] * inv_freq], axis=-1)
    cos, sin = jnp.cos(ang), jnp.sin(ang)
    return jnp.concatenate([cos, cos], axis=-1), jnp.concatenate([-sin, sin], axis=-1)


def _deinterleave(t):
    lead = t.shape[:-1]
    t = t.reshape(lead + (-1, HEAD_DIM // 2, 2))
    return jnp.swapaxes(t, -1, -2).reshape(lead + (-1,))


def kernel(x, c, ctx, c_ctx, w_ada, b_ada, ffn1_w_in, ffn1_w_out, mix_w_in, attn_q_gain, attn_k_gain,
           ret_decay_logit, w_proj_attn, w_proj_ret, mix_w_out, ffn2_w_in, ffn2_w_out, final_norm):
    batch, seq_len, _ = x.shape
    assert w_ada.shape[0] == 1, "single-layer block"
    assert seq_len % FFN_TOKEN_TILE == 0 and (batch * CTX_LEN) % TOKEN_TILE == 0
    tiles_per_row = seq_len // TOKEN_TILE

    w1_in, w1_out = ffn1_w_in[0].astype(BF16), ffn1_w_out[0].astype(BF16)
    w2_in, w2_out = ffn2_w_in[0].astype(BF16), ffn2_w_out[0].astype(BF16)
    w_mix = mix_w_in[0]
    w_qkv = jnp.concatenate([_deinterleave(w_mix[:, :ATTN_WIDTH + KV_WIDTH]),
                             w_mix[:, ATTN_WIDTH + KV_WIDTH:QKV_WIDTH]], axis=1).astype(BF16)
    w_ret = w_mix[:, RET_OFF:GR_OFF].astype(BF16)
    w_gr = w_mix[:, GR_OFF:GA_OFF].astype(BF16)
    w_ga = w_mix[:, GA_OFF:GB_OFF].astype(BF16)
    w_gb = w_mix[:, GB_OFF:].astype(BF16)
    w_pa, w_pr = w_proj_attn[0].astype(BF16), w_proj_ret[0].astype(BF16)
    w_mo = mix_w_out[0].astype(BF16)

    cond = jnp.zeros((8, D_MODEL), F32).at[:batch].set(c).at[batch].set(c_ctx)
    mod = _adaln(cond, w_ada[0], b_ada).reshape(8, N_MOD, 1, D_MODEL)
    mod = jnp.broadcast_to(mod, (8, N_MOD, ROW_BLOCK, D_MODEL))

    x2 = x.reshape(batch * seq_len, D_MODEL)
    c2 = ctx.reshape(batch * CTX_LEN, D_MODEL)
    h1, n2 = _ffn(x2, mod, w1_in, w1_out, mod_base=0, rows_per_mod=seq_len, row_offset=0,
                  tm=FFN_TOKEN_TILE, tf=FF_TILE, emit_next=True)
    n2c, = _ffn(c2, mod, w1_in, w1_out, mod_base=0, rows_per_mod=batch * CTX_LEN, row_offset=batch,
                tm=batch * CTX_LEN, tf=FF_TILE, emit_h=False, emit_next=True)

    cos, sin = _rope_tables(seq_len)
    gq = _deinterleave(attn_q_gain[0].reshape(1, HEAD_DIM))
    gk = _deinterleave(attn_k_gain[0].reshape(1, HEAD_DIM))
    qkv_x = _qkv_proj(n2, w_qkv, cos, sin, gq, gk, pos_tiles=tiles_per_row)
    ones = jnp.ones((TOKEN_TILE, HEAD_DIM), F32)
    qkv_c = _qkv_proj(n2c, w_qkv, ones, jnp.zeros_like(ones), gq, gk, pos_tiles=1)

    ret_scale = jnp.concatenate([jnp.ones((1, RET_WIDTH), F32),
                                 jnp.full((1, RET_WIDTH), RET_DIM ** -0.5, F32),
                                 jnp.ones((1, RET_WIDTH), F32)], axis=1)
    ret_x = _scaled_mm(n2, w_ret, ret_scale, BF16, "ret_proj")
    ret_c = _scaled_mm(n2c, w_ret, ret_scale, BF16, "ret_proj")
    gate_x = _scaled_mm(n2, w_gr, jnp.ones((1, RET_WIDTH), F32), F32, "ret_gate_proj")

    qkv_x = qkv_x.reshape(batch, seq_len, QKV_WIDTH)
    qkv_c = qkv_c.reshape(batch, CTX_LEN, QKV_WIDTH)
    score_bound = HEAD_DIM ** 0.5 * jnp.max(jnp.abs(gq)) * jnp.max(jnp.abs(gk))
    ya = _attention((score_bound <= ATTN_SAFE_SCORE).astype(jnp.int32).reshape(1), qkv_x, qkv_c)

    log_gamma = jax.nn.log_sigmoid(ret_decay_logit[0].astype(F32))
    yr = _retention(log_gamma, ret_x.reshape(batch, seq_len, 3 * RET_WIDTH),
                    gate_x.reshape(batch, seq_len, RET_WIDTH),
                    ret_c.reshape(batch, CTX_LEN, 3 * RET_WIDTH))

    h2 = _merge(h1, n2, ya.reshape(batch * seq_len, ATTN_WIDTH), yr.reshape(batch * seq_len, RET_WIDTH), mod,
                w_ga, w_gb, w_pa, w_pr, w_mo, tiles_per_row=tiles_per_row)
    out = _ffn(h2, mod, w2_in, w2_out, mod_base=6, rows_per_mod=seq_len, row_offset=0,
               tm=FFN_TOKEN_TILE, tf=FF_TILE,
               final_norm=jnp.broadcast_to(final_norm.reshape(1, D_MODEL), (ROW_BLOCK, D_MODEL)))
    return out[0].reshape(batch, seq_len, D_MODEL)
```

```python
import functools

import jax
import jax.numpy as jnp
from jax import lax
from jax.experimental import pallas as pl
from jax.experimental.pallas import tpu as pltpu

D_MODEL = 2048
CTX_LEN = 256
GRID_W = 64
HEAD_DIM = 128
LANES = 128
ATTN_Q_HEADS = 8
ATTN_KV_HEADS = 2
ATTN_GROUPS = ATTN_Q_HEADS // ATTN_KV_HEADS
ATTN_WIDTH = ATTN_Q_HEADS * HEAD_DIM
KV_WIDTH = ATTN_KV_HEADS * HEAD_DIM
RET_HEADS = 8
RET_DIM = 128
RET_WIDTH = RET_HEADS * RET_DIM
D_FF = 5632
ROPE_THETA = 10000.0
NORM_EPS = 1e-6
N_MOD = 9

QKV_WIDTH = ATTN_WIDTH + 2 * KV_WIDTH
RET_OFF = QKV_WIDTH
GR_OFF = RET_OFF + 3 * RET_WIDTH
GA_OFF = GR_OFF + RET_WIDTH
GB_OFF = GA_OFF + D_MODEL

F32 = jnp.float32
BF16 = jnp.bfloat16

VMEM_LIMIT_BYTES = 64 * 1024 * 1024

ADALN_TILE = 1024
TOKEN_TILE = 512
FF_TILE = 512
FFN_TOKEN_TILE = 1024
ROW_BLOCK = 16
ROW_UNROLL = 4
PROJ_TOKEN_TILE = 1024
MERGE_TOKEN_TILE = 256
MERGE_TILE = 512
ATTN_Q_TILE = 512
ATTN_KV_TILE = 768
RET_CHUNK = 256
RET_UNROLL = 8

Q_SCALE = HEAD_DIM ** -0.5 * 1.4426950408889634
ATTN_SAFE_SCORE = 32.0


def _params(*sem):
    return pltpu.CompilerParams(dimension_semantics=sem, vmem_limit_bytes=VMEM_LIMIT_BYTES)


def _rms(x):
    return x * lax.rsqrt(jnp.mean(x * x, axis=-1, keepdims=True) + NORM_EPS)


def _sigmoid(x):
    return 1.0 / (1.0 + jnp.exp(-x))


def _adaln_kernel(c_ref, w_ref, b_ref, o_ref):
    chunk = pl.program_id(0) // (D_MODEL // ADALN_TILE)
    c = c_ref[...]
    s = (c * _sigmoid(c)).astype(BF16)
    y = jnp.dot(s, w_ref[...].astype(BF16), preferred_element_type=F32) + b_ref[...]
    y = y + jnp.where(chunk % 3 == 1, 1.0, 0.0)
    o_ref[...] = y * jnp.where((chunk == 2) | (chunk == 8), 0.5, 1.0)


def _adaln(cond, w, b):
    n = w.shape[1]
    tn = ADALN_TILE
    return pl.pallas_call(
        _adaln_kernel,
        grid=(n // tn,),
        in_specs=[
            pl.BlockSpec((8, D_MODEL), lambda j: (0, 0)),
            pl.BlockSpec((D_MODEL, tn), lambda j: (0, j)),
            pl.BlockSpec((1, tn), lambda j: (0, j)),
        ],
        out_specs=pl.BlockSpec((8, tn), lambda j: (0, j)),
        out_shape=jax.ShapeDtypeStruct((8, n), F32),
        compiler_params=_params("arbitrary"),
        name="adaln",
    )(cond, w, b)


def _ffn_kernel(*refs, mod_base, emit_h, emit_next, final):
    h_ref, mod_ref, wa_ref, wb_ref, wo_ref = refs[:5]
    refs = refs[5:]
    if final:
        fn_ref, refs = refs[0], refs[1:]
    if emit_h:
        out_ref, refs = refs[0], refs[1:]
    if emit_next:
        nxt_ref, refs = refs[0], refs[1:]
    xn_sc, inv_sc = refs[:2]
    acc_ref = out_ref if emit_h else refs[2]
    j = pl.program_id(1)
    n_row_blocks = h_ref.shape[0] // ROW_BLOCK

    lane_tiles = [slice(k, k + LANES) for k in range(0, D_MODEL, LANES)]

    def row_block(r):
        return pl.ds(pl.multiple_of(r * ROW_BLOCK, ROW_BLOCK), ROW_BLOCK)

    def inv_rms(x):
        inv = lax.rsqrt(jnp.mean(x * x, axis=-1, keepdims=True) + NORM_EPS)
        return jnp.broadcast_to(inv, (x.shape[0], LANES))

    def for_row_blocks(body, unroll):
        def step(r, carry):
            body(row_block(r))
            return carry

        lax.fori_loop(0, n_row_blocks, step, 0, unroll=unroll)

    def prologue():
        def stats(rows):
            inv_sc[rows, :] = inv_rms(h_ref[rows, :])

        def prenorm(rows):
            inv = inv_sc[rows, :]
            for sl in lane_tiles:
                n = h_ref[rows, sl] * inv * mod_ref[0, mod_base + 1, :, sl] + mod_ref[0, mod_base, :, sl]
                xn_sc[rows, sl] = n.astype(BF16)

        for_row_blocks(stats, True)
        for_row_blocks(prenorm, ROW_UNROLL)

    def matmuls(first):
        xn = xn_sc[...]
        half = wa_ref.shape[1] // 2
        for k, sl in enumerate((slice(0, half), slice(half, 2 * half))):
            a = jnp.dot(xn, wa_ref[:, sl], preferred_element_type=F32)
            b = jnp.dot(xn, wb_ref[:, sl], preferred_element_type=F32)
            act = (a * _sigmoid(a) * b).astype(BF16)
            p = jnp.dot(act, wo_ref[sl, :], preferred_element_type=F32)
            if first and k == 0:
                acc_ref[...] = p
            else:
                acc_ref[...] += p

    def epilogue():
        def residual(rows):
            h = h_ref[rows, :] + mod_ref[0, mod_base + 2] * acc_ref[rows, :]
            acc_ref[rows, :] = h
            inv_sc[rows, :] = inv_rms(h)

        def postnorm(rows):
            inv = inv_sc[rows, :]
            for sl in lane_tiles:
                n = acc_ref[rows, sl] * inv
                if final:
                    out_ref[rows, sl] = n * fn_ref[:, sl]
                else:
                    n = n * mod_ref[0, mod_base + 4, :, sl] + mod_ref[0, mod_base + 3, :, sl]
                    nxt_ref[rows, sl] = n.astype(BF16)

        for_row_blocks(residual, True)
        if final or emit_next:
            for_row_blocks(postnorm, ROW_UNROLL)

    last = pl.num_programs(1) - 1

    @pl.when(j == 0)
    def _():
        prologue()
        matmuls(first=True)

    @pl.when((j > 0) & (j < last))
    def _():
        matmuls(first=False)

    @pl.when(j == last)
    def _():
        matmuls(first=False)
        epilogue()


def _ffn(h, mod, w_in, w_out, *, mod_base, rows_per_mod, row_offset, tm, tf, emit_h=True, emit_next=False,
         final_norm=None):
    t = h.shape[0]
    nf = D_FF // tf
    final = final_norm is not None
    tiles_per_row = rows_per_mod // tm
    n_mod = 5 if emit_next else 3
    mod = mod[:, mod_base:mod_base + n_mod]
    mod_map = lambda i, j: (i // tiles_per_row + row_offset, 0, 0, 0)
    in_specs = [
        pl.BlockSpec((tm, D_MODEL), lambda i, j: (i, 0)),
        pl.BlockSpec((1, n_mod, ROW_BLOCK, D_MODEL), mod_map),
        pl.BlockSpec((D_MODEL, tf), lambda i, j: (0, j)),
        pl.BlockSpec((D_MODEL, tf), lambda i, j: (0, j + nf)),
        pl.BlockSpec((tf, D_MODEL), lambda i, j: (j, 0)),
    ]
    args = [h, mod, w_in, w_in, w_out]
    if final:
        in_specs.append(pl.BlockSpec((ROW_BLOCK, D_MODEL), lambda i, j: (0, 0)))
        args.append(final_norm)
    out_specs, out_shape = [], []
    if emit_h:
        out_specs.append(pl.BlockSpec((tm, D_MODEL), lambda i, j: (i, 0)))
        out_shape.append(jax.ShapeDtypeStruct((t, D_MODEL), F32))
    if emit_next:
        out_specs.append(pl.BlockSpec((tm, D_MODEL), lambda i, j: (i, 0)))
        out_shape.append(jax.ShapeDtypeStruct((t, D_MODEL), BF16))
    return pl.pallas_call(
        functools.partial(_ffn_kernel, mod_base=0, emit_h=emit_h, emit_next=emit_next, final=final),
        grid=(t // tm, nf),
        in_specs=in_specs,
        out_specs=out_specs,
        out_shape=out_shape,
        scratch_shapes=[pltpu.VMEM((tm, D_MODEL), BF16), pltpu.VMEM((tm, LANES), F32)]
        + ([] if emit_h else [pltpu.VMEM((tm, D_MODEL), F32)]),
        compiler_params=_params("parallel", "arbitrary"),
        name="ffn_final" if final else "ffn",
    )(*args)


def _qkv_kernel(n_ref, w_ref, cos_ref, sin_ref, gq_ref, gk_ref, o_ref, *y_bufs, n_tiles):
    i = pl.program_id(0)

    def finish(y_ref):
        cos = cos_ref[...]
        sin = sin_ref[...]
        gq = gq_ref[...] * Q_SCALE
        gk = gk_ref[...]
        for hh in range(ATTN_Q_HEADS + ATTN_KV_HEADS):
            sl = slice(hh * HEAD_DIM, (hh + 1) * HEAD_DIM)
            t = _rms(y_ref[:, sl]) * (gq if hh < ATTN_Q_HEADS else gk)
            o_ref[:, sl] = (t * cos + pltpu.roll(t, HEAD_DIM // 2, 1) * sin).astype(BF16)
        o_ref[:, ATTN_WIDTH + KV_WIDTH:] = y_ref[:, ATTN_WIDTH + KV_WIDTH:].astype(BF16)

    @pl.when(i == 0)
    def _():
        y_bufs[1][...] = jnp.zeros_like(y_bufs[1])

    for parity in range(2):
        @pl.when((i < n_tiles) & (i % 2 == parity))
        def _(y_cur=y_bufs[parity], y_prev=y_bufs[1 - parity]):
            finish(y_prev)
            y_cur[...] = jnp.dot(n_ref[...], w_ref[...], preferred_element_type=F32)

    @pl.when(i == n_tiles)
    def _():
        finish(y_bufs[1 - n_tiles % 2])


def _qkv_proj(n, w, cos, sin, gq, gk, *, pos_tiles):
    t = n.shape[0]
    tm = TOKEN_TILE
    n_tiles = t // tm
    prev_tile = lambda i: jnp.maximum(i - 1, 0)
    return pl.pallas_call(
        functools.partial(_qkv_kernel, n_tiles=n_tiles),
        grid=(n_tiles + 1,),
        in_specs=[
            pl.BlockSpec((tm, D_MODEL), lambda i: (jnp.minimum(i, n_tiles - 1), 0)),
            pl.BlockSpec((D_MODEL, QKV_WIDTH), lambda i: (0, 0)),
            pl.BlockSpec((tm, HEAD_DIM), lambda i: (prev_tile(i) % pos_tiles, 0)),
            pl.BlockSpec((tm, HEAD_DIM), lambda i: (prev_tile(i) % pos_tiles, 0)),
            pl.BlockSpec((1, HEAD_DIM), lambda i: (0, 0)),
            pl.BlockSpec((1, HEAD_DIM), lambda i: (0, 0)),
        ],
        out_specs=pl.BlockSpec((tm, QKV_WIDTH), lambda i: (prev_tile(i), 0)),
        out_shape=jax.ShapeDtypeStruct((t, QKV_WIDTH), BF16),
        scratch_shapes=[pltpu.VMEM((tm, QKV_WIDTH), F32), pltpu.VMEM((tm, QKV_WIDTH), F32)],
        compiler_params=_params("arbitrary"),
        name="qkv_proj",
    )(n, w, cos, sin, gq, gk)


def _scaled_mm_kernel(n_ref, w_ref, s_ref, o_ref):
    y = jnp.dot(n_ref[...], w_ref[...], preferred_element_type=F32)
    o_ref[...] = (y * s_ref[...]).astype(o_ref.dtype)


def _scaled_mm(n, w, col_scale, out_dtype, name):
    t = n.shape[0]
    nn = w.shape[1]
    tm = min(t, PROJ_TOKEN_TILE)
    return pl.pallas_call(
        _scaled_mm_kernel,
        grid=(t // tm,),
        in_specs=[
            pl.BlockSpec((tm, D_MODEL), lambda i: (i, 0)),
            pl.BlockSpec((D_MODEL, nn), lambda i: (0, 0), pipeline_mode=pl.Buffered(1)),
            pl.BlockSpec((1, nn), lambda i: (0, 0)),
        ],
        out_specs=pl.BlockSpec((tm, nn), lambda i: (i, 0)),
        out_shape=jax.ShapeDtypeStruct((t, nn), out_dtype),
        compiler_params=_params("parallel"),
        name=name,
    )(n, w, col_scale)


def _attn_kernel(flag_ref, q_ref, kc_ref, kx_ref, vc_ref, vx_ref, o_ref,
                 k_sc, v_sc, qs_sc, acc_sc, m_sc):
    tq = q_ref.shape[1]
    tk = ATTN_KV_TILE
    n_ctx = kc_ref.shape[1]
    lk = k_sc.shape[0]
    nk = lk // tk

    @pl.when(pl.program_id(2) == 0)
    def _():
        k_sc[:n_ctx, :] = kc_ref[0]
        k_sc[n_ctx:, :] = kx_ref[0]
        v_sc[:n_ctx, :HEAD_DIM] = vc_ref[0]
        v_sc[n_ctx:, :HEAD_DIM] = vx_ref[0]
        lane = lax.broadcasted_iota(jnp.int32, (n_ctx, HEAD_DIM), 1)
        ones_col = jnp.where(lane == 0, 1.0, 0.0).astype(BF16)
        for r in range(lk // n_ctx):
            v_sc[r * n_ctx:(r + 1) * n_ctx, HEAD_DIM:] = ones_col

    for g in range(ATTN_GROUPS):
        qs_sc[g * tq:(g + 1) * tq, :] = q_ref[0, :, g * HEAD_DIM:(g + 1) * HEAD_DIM]

    def scores(c):
        start = pl.multiple_of(c * tk, tk)
        k = k_sc[pl.ds(start, tk), :]
        s = lax.dot_general(qs_sc[...], k, (((1,), (1,)), ((), ())), preferred_element_type=F32)
        return s, v_sc[pl.ds(start, tk), :]

    def finish():
        acc = acc_sc[...]
        out = acc[:, :HEAD_DIM] / acc[:, HEAD_DIM:HEAD_DIM + 1]
        for g in range(ATTN_GROUPS):
            o_ref[0, :, g * HEAD_DIM:(g + 1) * HEAD_DIM] = out[g * tq:(g + 1) * tq, :].astype(BF16)

    @pl.when(flag_ref[0] != 0)
    def _():
        acc_sc[...] = jnp.zeros_like(acc_sc)

        def body(c, carry):
            s, v = scores(c)
            acc_sc[...] += jnp.dot(jnp.exp2(s).astype(BF16), v, preferred_element_type=F32)
            return carry

        lax.fori_loop(0, nk, body, 0, unroll=True)
        finish()

    @pl.when(flag_ref[0] == 0)
    def _():
        acc_sc[...] = jnp.zeros_like(acc_sc)
        m_sc[...] = jnp.full_like(m_sc, -jnp.inf)

        def body(c, carry):
            s, v = scores(c)
            m_prev = m_sc[...]
            m_new = jnp.maximum(m_prev, jnp.max(s, axis=-1, keepdims=True))
            p = jnp.exp2(s - m_new).astype(BF16)
            acc_sc[...] = jnp.exp2(m_prev - m_new) * acc_sc[...] + jnp.dot(p, v, preferred_element_type=F32)
            m_sc[...] = m_new
            return carry

        lax.fori_loop(0, nk, body, 0)
        finish()


def _attention(bounded_flag, qkv_x, qkv_c):
    b, l, _ = qkv_x.shape
    lc = qkv_c.shape[1]
    lk = lc + l
    tq = ATTN_Q_TILE
    gw = ATTN_GROUPS * HEAD_DIM
    rows = ATTN_GROUPS * tq
    k_col = ATTN_WIDTH // HEAD_DIM
    v_col = k_col + ATTN_KV_HEADS
    lat = lambda col: pl.BlockSpec((1, l, HEAD_DIM), lambda bi, hi, qi: (bi, 0, hi + col))
    ctx = lambda col: pl.BlockSpec((1, lc, HEAD_DIM), lambda bi, hi, qi: (bi, 0, hi + col))
    return pl.pallas_call(
        _attn_kernel,
        grid=(b, ATTN_KV_HEADS, l // tq),
        in_specs=[
            pl.BlockSpec(memory_space=pltpu.SMEM),
            pl.BlockSpec((1, tq, gw), lambda bi, hi, qi: (bi, qi, hi)),
            ctx(k_col), lat(k_col), ctx(v_col), lat(v_col),
        ],
        out_specs=pl.BlockSpec((1, tq, gw), lambda bi, hi, qi: (bi, qi, hi)),
        out_shape=jax.ShapeDtypeStruct((b, l, ATTN_WIDTH), BF16),
        scratch_shapes=[
            pltpu.VMEM((lk, HEAD_DIM), BF16),
            pltpu.VMEM((lk, 2 * HEAD_DIM), BF16),
            pltpu.VMEM((rows, HEAD_DIM), BF16),
            pltpu.VMEM((rows, 2 * HEAD_DIM), F32),
            pltpu.VMEM((rows, 1), F32),
        ],
        compiler_params=_params("parallel", "parallel", "arbitrary"),
        name="attention",
    )(bounded_flag, qkv_x, qkv_c, qkv_x, qkv_c, qkv_x)


def _ret_kernel(lg_ref, q_ref, k_ref, v_ref, g_ref, kc_ref, vc_ref, o_ref, uf_sc, sb_sc):
    c_len = RET_CHUNK
    n_chunks = q_ref.shape[1] // c_len
    n_ctx = kc_ref.shape[1]
    head = pl.program_id(1)
    lgf = lg_ref[0, head]
    lgb = lg_ref[1, head]

    row = lax.broadcasted_iota(jnp.int32, (c_len, 1), 0).astype(F32)
    vdec_f = jnp.exp((c_len - 1.0 - row) * lgf)
    vdec_b = jnp.exp(row * lgb)
    qdec_f = jnp.exp((row + 1.0) * lgf)
    qdec_b = jnp.exp((c_len - row) * lgb)
    chunk_f = jnp.exp(jnp.full((1, RET_DIM), c_len, F32) * lgf)
    chunk_b = jnp.exp(jnp.full((1, RET_DIM), c_len, F32) * lgb)
    diff = (lax.broadcasted_iota(jnp.int32, (c_len, c_len), 0)
            - lax.broadcasted_iota(jnp.int32, (c_len, c_len), 1)).astype(F32)
    decay = (jnp.where(diff >= 0, jnp.exp(jnp.maximum(diff, 0.0) * lgf), 0.0)
             + jnp.where(diff <= 0, jnp.exp(jnp.maximum(-diff, 0.0) * lgb), 0.0))

    def kv_outer(k, v, dec_f, dec_b):
        vf = v.astype(F32)
        v2 = jnp.concatenate([(vf * dec_f).astype(BF16), (vf * dec_b).astype(BF16)], axis=1)
        return lax.dot_general(k, v2, (((0,), (0,)), ((), ())), preferred_element_type=F32)

    crow = lax.broadcasted_iota(jnp.int32, (n_ctx, 1), 0).astype(F32)
    s0 = kv_outer(kc_ref[0], vc_ref[0], jnp.exp((n_ctx - 1.0 - crow) * lgf), jnp.exp(crow * lgb))

    def back_body(t, sb):
        c = n_chunks - 1 - t
        start = pl.multiple_of(c * c_len, c_len)
        sb_sc[c] = sb
        u = kv_outer(k_ref[0, pl.ds(start, c_len), :], v_ref[0, pl.ds(start, c_len), :], vdec_f, vdec_b)
        uf_sc[c] = u[:, :RET_DIM]
        return sb * chunk_b + u[:, RET_DIM:]

    lax.fori_loop(0, n_chunks, back_body, s0[:, RET_DIM:], unroll=RET_UNROLL)

    def fwd_body(c, sf):
        start = pl.multiple_of(c * c_len, c_len)
        q = q_ref[0, pl.ds(start, c_len), :]
        k = k_ref[0, pl.ds(start, c_len), :]
        v = v_ref[0, pl.ds(start, c_len), :]
        inner = lax.dot_general(q, k, (((1,), (1,)), ((), ())), preferred_element_type=F32)
        y = jnp.dot((inner * decay).astype(BF16), v, preferred_element_type=F32)
        states = jnp.concatenate([sf.astype(BF16), sb_sc[c].astype(BF16)], axis=1)
        cross = jnp.dot(q, states, preferred_element_type=F32)
        y = y + cross[:, :RET_DIM] * qdec_f + cross[:, RET_DIM:] * qdec_b
        gate = g_ref[0, pl.ds(start, c_len), :]
        o_ref[0, pl.ds(start, c_len), :] = (gate * _sigmoid(gate) * _rms(y)).astype(BF16)
        return sf * chunk_f + uf_sc[c]

    lax.fori_loop(0, n_chunks, fwd_body, s0[:, :RET_DIM], unroll=RET_UNROLL)


def _retention(log_gamma, ret_x, gate_x, ret_c):
    b, l, _ = ret_x.shape
    lc = ret_c.shape[1]
    n_chunks = l // RET_CHUNK
    seq = lambda off: pl.BlockSpec((1, l, RET_DIM), lambda bi, hi: (bi, 0, hi + off))
    ctx = lambda off: pl.BlockSpec((1, lc, RET_DIM), lambda bi, hi: (bi, 0, hi + off))
    return pl.pallas_call(
        _ret_kernel,
        grid=(b, RET_HEADS),
        in_specs=[
            pl.BlockSpec(memory_space=pltpu.SMEM),
            seq(0), seq(RET_HEADS), seq(2 * RET_HEADS), seq(0),
            ctx(RET_HEADS), ctx(2 * RET_HEADS),
        ],
        out_specs=seq(0),
        out_shape=jax.ShapeDtypeStruct((b, l, RET_WIDTH), BF16),
        scratch_shapes=[
            pltpu.VMEM((n_chunks, RET_DIM, RET_DIM), F32),
            pltpu.VMEM((n_chunks, RET_DIM, RET_DIM), F32),
        ],
        compiler_params=_params("parallel", "arbitrary"),
        name="retention",
    )(log_gamma, ret_x, ret_x, ret_x, gate_x, ret_c, ret_c)


def _merge_kernel(h_ref, n_ref, ya_ref, yr_ref, mod_ref, wga_ref, wgb_ref, wpa_ref, wpr_ref, wo_ref, out_ref):
    n = n_ref[...]
    ya = ya_ref[...]
    yr = yr_ref[...]
    acc = None
    for c in range(0, D_MODEL, MERGE_TILE):
        sl = slice(c, c + MERGE_TILE)
        ga = jnp.dot(n, wga_ref[:, sl], preferred_element_type=F32)
        gb = jnp.dot(n, wgb_ref[:, sl], preferred_element_type=F32)
        pa = jnp.dot(ya, wpa_ref[:, sl], preferred_element_type=F32)
        pr = jnp.dot(yr, wpr_ref[:, sl], preferred_element_type=F32)
        z = (_sigmoid(ga) * pa + _sigmoid(gb) * pr).astype(BF16)
        p = jnp.dot(z, wo_ref[sl, :], preferred_element_type=F32)
        acc = p if acc is None else acc + p
    out_ref[...] = h_ref[...] + mod_ref[0, 5, 0:1, :] * acc


def _merge(h, n, ya, yr, mod, w_ga, w_gb, w_pa, w_pr, w_out, *, rows_per_mod):
    t = h.shape[0]
    tm = MERGE_TOKEN_TILE
    tiles_per_row = rows_per_mod // tm
    resident = lambda w: pl.BlockSpec(w.shape, lambda i: (0, 0), pipeline_mode=pl.Buffered(1))
    return pl.pallas_call(
        _merge_kernel,
        grid=(t // tm,),
        in_specs=[
            pl.BlockSpec((tm, D_MODEL), lambda i: (i, 0)),
            pl.BlockSpec((tm, D_MODEL), lambda i: (i, 0)),
            pl.BlockSpec((tm, ATTN_WIDTH), lambda i: (i, 0)),
            pl.BlockSpec((tm, RET_WIDTH), lambda i: (i, 0)),
            pl.BlockSpec((1, N_MOD, ROW_BLOCK, D_MODEL), lambda i: (i // tiles_per_row, 0, 0, 0)),
            resident(w_ga), resident(w_gb), resident(w_pa), resident(w_pr), resident(w_out),
        ],
        out_specs=pl.BlockSpec((tm, D_MODEL), lambda i: (i, 0)),
        out_shape=jax.ShapeDtypeStruct((t, D_MODEL), F32),
        compiler_params=_params("parallel"),
        name="merge",
    )(h, n, ya, yr, mod, w_ga, w_gb, w_pa, w_pr, w_out)


def _rope_tables(seq_len):
    rows = seq_len // GRID_W
    row = jnp.repeat(jnp.arange(rows, dtype=F32), GRID_W)
    col = jnp.tile(jnp.arange(GRID_W, dtype=F32), rows)
    half = HEAD_DIM // 2
    inv_freq = ROPE_THETA ** (-jnp.arange(0, half, 2, dtype=F32) / half)
    ang = jnp.concatenate([row[:, None] * inv_freq, col[:, None] * inv_freq], axis=-1)
    cos, sin = jnp.cos(ang), jnp.sin(ang)
    return jnp.concatenate([cos, cos], axis=-1), jnp.concatenate([-sin, sin], axis=-1)


def _deinterleave(t):
    lead = t.shape[:-1]
    t = t.reshape(lead + (-1, HEAD_DIM // 2, 2))
    return jnp.swapaxes(t, -1, -2).reshape(lead + (-1,))


def kernel(x, c, ctx, c_ctx, w_ada, b_ada, ffn1_w_in, ffn1_w_out, mix_w_in, attn_q_gain, attn_k_gain,
           ret_decay_logit, w_proj_attn, w_proj_ret, mix_w_out, ffn2_w_in, ffn2_w_out, final_norm):
    batch, seq_len, _ = x.shape
    assert w_ada.shape[0] == 1, "single-layer block"
    assert seq_len % FFN_TOKEN_TILE == 0 and (batch * CTX_LEN) % TOKEN_TILE == 0
    tiles_per_row = seq_len // TOKEN_TILE

    w1_in, w1_out = ffn1_w_in[0].astype(BF16), ffn1_w_out[0].astype(BF16)
    w2_in, w2_out = ffn2_w_in[0].astype(BF16), ffn2_w_out[0].astype(BF16)
    w_mix = mix_w_in[0]
    w_qkv = jnp.concatenate([_deinterleave(w_mix[:, :ATTN_WIDTH + KV_WIDTH]),
                             w_mix[:, ATTN_WIDTH + KV_WIDTH:QKV_WIDTH]], axis=1).astype(BF16)
    w_ret = w_mix[:, RET_OFF:GR_OFF].astype(BF16)
    w_gr = w_mix[:, GR_OFF:GA_OFF].astype(BF16)
    w_ga = w_mix[:, GA_OFF:GB_OFF].astype(BF16)
    w_gb = w_mix[:, GB_OFF:].astype(BF16)
    w_pa, w_pr = w_proj_attn[0].astype(BF16), w_proj_ret[0].astype(BF16)
    w_mo = mix_w_out[0].astype(BF16)

    cond = jnp.zeros((8, D_MODEL), F32).at[:batch].set(c).at[batch].set(c_ctx)
    mod = _adaln(cond, w_ada[0], b_ada).reshape(8, N_MOD, 1, D_MODEL)
    mod = jnp.broadcast_to(mod, (8, N_MOD, ROW_BLOCK, D_MODEL))

    x2 = x.reshape(batch * seq_len, D_MODEL)
    c2 = ctx.reshape(batch * CTX_LEN, D_MODEL)
    h1, n2 = _ffn(x2, mod, w1_in, w1_out, mod_base=0, rows_per_mod=seq_len, row_offset=0,
                  tm=FFN_TOKEN_TILE, tf=FF_TILE, emit_next=True)
    n2c, = _ffn(c2, mod, w1_in, w1_out, mod_base=0, rows_per_mod=batch * CTX_LEN, row_offset=batch,
                tm=batch * CTX_LEN, tf=FF_TILE, emit_h=False, emit_next=True)

    cos, sin = _rope_tables(seq_len)
    gq = _deinterleave(attn_q_gain[0].reshape(1, HEAD_DIM))
    gk = _deinterleave(attn_k_gain[0].reshape(1, HEAD_DIM))
    qkv_x = _qkv_proj(n2, w_qkv, cos, sin, gq, gk, pos_tiles=tiles_per_row)
    ones = jnp.ones((TOKEN_TILE, HEAD_DIM), F32)
    qkv_c = _qkv_proj(n2c, w_qkv, ones, jnp.zeros_like(ones), gq, gk, pos_tiles=1)

    ret_scale = jnp.concatenate([jnp.ones((1, RET_WIDTH), F32),
                                 jnp.full((1, RET_WIDTH), RET_DIM ** -0.5, F32),
                                 jnp.ones((1, RET_WIDTH), F32)], axis=1)
    ret_x = _scaled_mm(n2, w_ret, ret_scale, BF16, "ret_proj")
    ret_c = _scaled_mm(n2c, w_ret, ret_scale, BF16, "ret_proj")
    gate_x = _scaled_mm(n2, w_gr, jnp.ones((1, RET_WIDTH), F32), F32, "ret_gate_proj")

    qkv_x = qkv_x.reshape(batch, seq_len, QKV_WIDTH)
    qkv_c = qkv_c.reshape(batch, CTX_LEN, QKV_WIDTH)
    score_bound = HEAD_DIM ** 0.5 * jnp.max(jnp.abs(gq)) * jnp.max(jnp.abs(gk))
    ya = _attention((score_bound <= ATTN_SAFE_SCORE).astype(jnp.int32).reshape(1), qkv_x, qkv_c)

    log_gamma = jax.nn.log_sigmoid(ret_decay_logit[0].astype(F32))
    yr = _retention(log_gamma, ret_x.reshape(batch, seq_len, 3 * RET_WIDTH),
                    gate_x.reshape(batch, seq_len, RET_WIDTH),
                    ret_c.reshape(batch, CTX_LEN, 3 * RET_WIDTH))

    h2 = _merge(h1, n2, ya.reshape(batch * seq_len, ATTN_WIDTH), yr.reshape(batch * seq_len, RET_WIDTH), mod,
                w_ga, w_gb, w_pa, w_pr, w_mo, rows_per_mod=seq_len)
    out = _ffn(h2, mod, w2_in, w2_out, mod_base=6, rows_per_mod=seq_len, row_offset=0,
               tm=FFN_TOKEN_TILE, tf=FF_TILE,
               final_norm=jnp.broadcast_to(final_norm.reshape(1, D_MODEL), (ROW_BLOCK, D_MODEL)))
    return out[0].reshape(batch, seq_len, D_MODEL)
```

```python
import functools

import jax
import jax.numpy as jnp
from jax import lax
from jax.experimental import pallas as pl
from jax.experimental.pallas import tpu as pltpu

D_MODEL = 2048
CTX_LEN = 256
GRID_W = 64
HEAD_DIM = 128
LANES = 128
ATTN_Q_HEADS = 8
ATTN_KV_HEADS = 2
ATTN_GROUPS = ATTN_Q_HEADS // ATTN_KV_HEADS
ATTN_WIDTH = ATTN_Q_HEADS * HEAD_DIM
KV_WIDTH = ATTN_KV_HEADS * HEAD_DIM
RET_HEADS = 8
RET_DIM = 128
RET_WIDTH = RET_HEADS * RET_DIM
D_FF = 5632
ROPE_THETA = 10000.0
NORM_EPS = 1e-6
N_MOD = 9

QKV_WIDTH = ATTN_WIDTH + 2 * KV_WIDTH
RET_OFF = QKV_WIDTH
GR_OFF = RET_OFF + 3 * RET_WIDTH
GA_OFF = GR_OFF + RET_WIDTH
GB_OFF = GA_OFF + D_MODEL

F32 = jnp.float32
BF16 = jnp.bfloat16

VMEM_LIMIT_BYTES = 64 * 1024 * 1024

ADALN_TILE = 1024
TOKEN_TILE = 512
FF_TILE = 512
FFN_TOKEN_TILE = 1024
ROW_BLOCK = 16
ROW_UNROLL = 4
PROJ_TOKEN_TILE = 1024
MERGE_TOKEN_TILE = 512
MERGE_TILE = 512
ATTN_Q_TILE = 512
ATTN_KV_TILE = 768
RET_CHUNK = 256
RET_UNROLL = 8

Q_SCALE = HEAD_DIM ** -0.5 * 1.4426950408889634
ATTN_SAFE_SCORE = 32.0


def _params(*sem):
    return pltpu.CompilerParams(dimension_semantics=sem, vmem_limit_bytes=VMEM_LIMIT_BYTES)


def _rms(x):
    return x * lax.rsqrt(jnp.mean(x * x, axis=-1, keepdims=True) + NORM_EPS)


def _sigmoid(x):
    return 1.0 / (1.0 + jnp.exp(-x))


def _adaln_kernel(c_ref, w_ref, b_ref, o_ref):
    chunk = pl.program_id(0) // (D_MODEL // ADALN_TILE)
    c = c_ref[...]
    s = (c * _sigmoid(c)).astype(BF16)
    y = jnp.dot(s, w_ref[...].astype(BF16), preferred_element_type=F32) + b_ref[...]
    y = y + jnp.where(chunk % 3 == 1, 1.0, 0.0)
    o_ref[...] = y * jnp.where((chunk == 2) | (chunk == 8), 0.5, 1.0)


def _adaln(cond, w, b):
    n = w.shape[1]
    tn = ADALN_TILE
    return pl.pallas_call(
        _adaln_kernel,
        grid=(n // tn,),
        in_specs=[
            pl.BlockSpec((8, D_MODEL), lambda j: (0, 0)),
            pl.BlockSpec((D_MODEL, tn), lambda j: (0, j)),
            pl.BlockSpec((1, tn), lambda j: (0, j)),
        ],
        out_specs=pl.BlockSpec((8, tn), lambda j: (0, j)),
        out_shape=jax.ShapeDtypeStruct((8, n), F32),
        compiler_params=_params("arbitrary"),
        name="adaln",
    )(cond, w, b)


def _ffn_kernel(*refs, mod_base, emit_h, emit_next, final):
    h_ref, mod_ref, wa_ref, wb_ref, wo_ref = refs[:5]
    refs = refs[5:]
    if final:
        fn_ref, refs = refs[0], refs[1:]
    if emit_h:
        out_ref, refs = refs[0], refs[1:]
    if emit_next:
        nxt_ref, refs = refs[0], refs[1:]
    xn_sc, inv_sc = refs[:2]
    acc_ref = out_ref if emit_h else refs[2]
    j = pl.program_id(1)
    n_row_blocks = h_ref.shape[0] // ROW_BLOCK

    lane_tiles = [slice(k, k + LANES) for k in range(0, D_MODEL, LANES)]

    def row_block(r):
        return pl.ds(pl.multiple_of(r * ROW_BLOCK, ROW_BLOCK), ROW_BLOCK)

    def inv_rms(x):
        inv = lax.rsqrt(jnp.mean(x * x, axis=-1, keepdims=True) + NORM_EPS)
        return jnp.broadcast_to(inv, (x.shape[0], LANES))

    def for_row_blocks(body, unroll):
        def step(r, carry):
            body(row_block(r))
            return carry

        lax.fori_loop(0, n_row_blocks, step, 0, unroll=unroll)

    def prologue():
        def stats(rows):
            inv_sc[rows, :] = inv_rms(h_ref[rows, :])

        def prenorm(rows):
            inv = inv_sc[rows, :]
            for sl in lane_tiles:
                n = h_ref[rows, sl] * inv * mod_ref[0, mod_base + 1, :, sl] + mod_ref[0, mod_base, :, sl]
                xn_sc[rows, sl] = n.astype(BF16)

        for_row_blocks(stats, True)
        for_row_blocks(prenorm, ROW_UNROLL)

    def matmuls(first):
        xn = xn_sc[...]
        half = wa_ref.shape[1] // 2
        for k, sl in enumerate((slice(0, half), slice(half, 2 * half))):
            a = jnp.dot(xn, wa_ref[:, sl], preferred_element_type=F32)
            b = jnp.dot(xn, wb_ref[:, sl], preferred_element_type=F32)
            act = (a * _sigmoid(a) * b).astype(BF16)
            p = jnp.dot(act, wo_ref[sl, :], preferred_element_type=F32)
            if first and k == 0:
                acc_ref[...] = p
            else:
                acc_ref[...] += p

    def epilogue():
        def residual(rows):
            h = h_ref[rows, :] + mod_ref[0, mod_base + 2] * acc_ref[rows, :]
            acc_ref[rows, :] = h
            inv_sc[rows, :] = inv_rms(h)

        def postnorm(rows):
            inv = inv_sc[rows, :]
            for sl in lane_tiles:
                n = acc_ref[rows, sl] * inv
                if final:
                    out_ref[rows, sl] = n * fn_ref[:, sl]
                else:
                    n = n * mod_ref[0, mod_base + 4, :, sl] + mod_ref[0, mod_base + 3, :, sl]
                    nxt_ref[rows, sl] = n.astype(BF16)

        for_row_blocks(residual, True)
        if final or emit_next:
            for_row_blocks(postnorm, ROW_UNROLL)

    last = pl.num_programs(1) - 1

    @pl.when(j == 0)
    def _():
        prologue()
        matmuls(first=True)

    @pl.when((j > 0) & (j < last))
    def _():
        matmuls(first=False)

    @pl.when(j == last)
    def _():
        matmuls(first=False)
        epilogue()


def _ffn(h, mod, w_in, w_out, *, mod_base, rows_per_mod, row_offset, tm, tf, emit_h=True, emit_next=False,
         final_norm=None):
    t = h.shape[0]
    nf = D_FF // tf
    final = final_norm is not None
    tiles_per_row = rows_per_mod // tm
    n_mod = 5 if emit_next else 3
    mod = mod[:, mod_base:mod_base + n_mod]
    mod_map = lambda i, j: (i // tiles_per_row + row_offset, 0, 0, 0)
    in_specs = [
        pl.BlockSpec((tm, D_MODEL), lambda i, j: (i, 0)),
        pl.BlockSpec((1, n_mod, ROW_BLOCK, D_MODEL), mod_map),
        pl.BlockSpec((D_MODEL, tf), lambda i, j: (0, j)),
        pl.BlockSpec((D_MODEL, tf), lambda i, j: (0, j + nf)),
        pl.BlockSpec((tf, D_MODEL), lambda i, j: (j, 0)),
    ]
    args = [h, mod, w_in, w_in, w_out]
    if final:
        in_specs.append(pl.BlockSpec((ROW_BLOCK, D_MODEL), lambda i, j: (0, 0)))
        args.append(final_norm)
    out_specs, out_shape = [], []
    if emit_h:
        out_specs.append(pl.BlockSpec((tm, D_MODEL), lambda i, j: (i, 0)))
        out_shape.append(jax.ShapeDtypeStruct((t, D_MODEL), F32))
    if emit_next:
        out_specs.append(pl.BlockSpec((tm, D_MODEL), lambda i, j: (i, 0)))
        out_shape.append(jax.ShapeDtypeStruct((t, D_MODEL), BF16))
    return pl.pallas_call(
        functools.partial(_ffn_kernel, mod_base=0, emit_h=emit_h, emit_next=emit_next, final=final),
        grid=(t // tm, nf),
        in_specs=in_specs,
        out_specs=out_specs,
        out_shape=out_shape,
        scratch_shapes=[pltpu.VMEM((tm, D_MODEL), BF16), pltpu.VMEM((tm, LANES), F32)]
        + ([] if emit_h else [pltpu.VMEM((tm, D_MODEL), F32)]),
        compiler_params=_params("parallel", "arbitrary"),
        name="ffn_final" if final else "ffn",
    )(*args)


def _qkv_kernel(n_ref, w_ref, cos_ref, sin_ref, gq_ref, gk_ref, o_ref, *y_bufs, n_tiles):
    i = pl.program_id(0)

    def finish(y_ref):
        cos = cos_ref[...]
        sin = sin_ref[...]
        gq = gq_ref[...] * Q_SCALE
        gk = gk_ref[...]
        for hh in range(ATTN_Q_HEADS + ATTN_KV_HEADS):
            sl = slice(hh * HEAD_DIM, (hh + 1) * HEAD_DIM)
            t = _rms(y_ref[:, sl]) * (gq if hh < ATTN_Q_HEADS else gk)
            o_ref[:, sl] = (t * cos + pltpu.roll(t, HEAD_DIM // 2, 1) * sin).astype(BF16)
        o_ref[:, ATTN_WIDTH + KV_WIDTH:] = y_ref[:, ATTN_WIDTH + KV_WIDTH:].astype(BF16)

    @pl.when(i == 0)
    def _():
        y_bufs[1][...] = jnp.zeros_like(y_bufs[1])

    for parity in range(2):
        @pl.when((i < n_tiles) & (i % 2 == parity))
        def _(y_cur=y_bufs[parity], y_prev=y_bufs[1 - parity]):
            finish(y_prev)
            y_cur[...] = jnp.dot(n_ref[...], w_ref[...], preferred_element_type=F32)

    @pl.when(i == n_tiles)
    def _():
        finish(y_bufs[1 - n_tiles % 2])


def _qkv_proj(n, w, cos, sin, gq, gk, *, pos_tiles):
    t = n.shape[0]
    tm = TOKEN_TILE
    n_tiles = t // tm
    prev_tile = lambda i: jnp.maximum(i - 1, 0)
    return pl.pallas_call(
        functools.partial(_qkv_kernel, n_tiles=n_tiles),
        grid=(n_tiles + 1,),
        in_specs=[
            pl.BlockSpec((tm, D_MODEL), lambda i: (jnp.minimum(i, n_tiles - 1), 0)),
            pl.BlockSpec((D_MODEL, QKV_WIDTH), lambda i: (0, 0)),
            pl.BlockSpec((tm, HEAD_DIM), lambda i: (prev_tile(i) % pos_tiles, 0)),
            pl.BlockSpec((tm, HEAD_DIM), lambda i: (prev_tile(i) % pos_tiles, 0)),
            pl.BlockSpec((1, HEAD_DIM), lambda i: (0, 0)),
            pl.BlockSpec((1, HEAD_DIM), lambda i: (0, 0)),
        ],
        out_specs=pl.BlockSpec((tm, QKV_WIDTH), lambda i: (prev_tile(i), 0)),
        out_shape=jax.ShapeDtypeStruct((t, QKV_WIDTH), BF16),
        scratch_shapes=[pltpu.VMEM((tm, QKV_WIDTH), F32), pltpu.VMEM((tm, QKV_WIDTH), F32)],
        compiler_params=_params("arbitrary"),
        name="qkv_proj",
    )(n, w, cos, sin, gq, gk)


def _scaled_mm_kernel(n_ref, w_ref, s_ref, o_ref):
    y = jnp.dot(n_ref[...], w_ref[...], preferred_element_type=F32)
    o_ref[...] = (y * s_ref[...]).astype(o_ref.dtype)


def _scaled_mm(n, w, col_scale, out_dtype, name):
    t = n.shape[0]
    nn = w.shape[1]
    tm = min(t, PROJ_TOKEN_TILE)
    return pl.pallas_call(
        _scaled_mm_kernel,
        grid=(t // tm,),
        in_specs=[
            pl.BlockSpec((tm, D_MODEL), lambda i: (i, 0)),
            pl.BlockSpec((D_MODEL, nn), lambda i: (0, 0), pipeline_mode=pl.Buffered(1)),
            pl.BlockSpec((1, nn), lambda i: (0, 0)),
        ],
        out_specs=pl.BlockSpec((tm, nn), lambda i: (i, 0)),
        out_shape=jax.ShapeDtypeStruct((t, nn), out_dtype),
        compiler_params=_params("parallel"),
        name=name,
    )(n, w, col_scale)


def _attn_kernel(flag_ref, q_ref, kc_ref, kx_ref, vc_ref, vx_ref, o_ref,
                 k_sc, v_sc, qs_sc, acc_sc, m_sc):
    tq = q_ref.shape[1]
    tk = ATTN_KV_TILE
    n_ctx = kc_ref.shape[1]
    lk = k_sc.shape[0]
    nk = lk // tk

    @pl.when(pl.program_id(2) == 0)
    def _():
        k_sc[:n_ctx, :] = kc_ref[0]
        k_sc[n_ctx:, :] = kx_ref[0]
        v_sc[:n_ctx, :HEAD_DIM] = vc_ref[0]
        v_sc[n_ctx:, :HEAD_DIM] = vx_ref[0]
        lane = lax.broadcasted_iota(jnp.int32, (n_ctx, HEAD_DIM), 1)
        ones_col = jnp.where(lane == 0, 1.0, 0.0).astype(BF16)
        for r in range(lk // n_ctx):
            v_sc[r * n_ctx:(r + 1) * n_ctx, HEAD_DIM:] = ones_col

    for g in range(ATTN_GROUPS):
        qs_sc[g * tq:(g + 1) * tq, :] = q_ref[0, :, g * HEAD_DIM:(g + 1) * HEAD_DIM]

    def scores(c):
        start = pl.multiple_of(c * tk, tk)
        k = k_sc[pl.ds(start, tk), :]
        s = lax.dot_general(qs_sc[...], k, (((1,), (1,)), ((), ())), preferred_element_type=F32)
        return s, v_sc[pl.ds(start, tk), :]

    def finish():
        acc = acc_sc[...]
        out = acc[:, :HEAD_DIM] / acc[:, HEAD_DIM:HEAD_DIM + 1]
        for g in range(ATTN_GROUPS):
            o_ref[0, :, g * HEAD_DIM:(g + 1) * HEAD_DIM] = out[g * tq:(g + 1) * tq, :].astype(BF16)

    @pl.when(flag_ref[0] != 0)
    def _():
        acc_sc[...] = jnp.zeros_like(acc_sc)

        def body(c, carry):
            s, v = scores(c)
            acc_sc[...] += jnp.dot(jnp.exp2(s).astype(BF16), v, preferred_element_type=F32)
            return carry

        lax.fori_loop(0, nk, body, 0, unroll=True)
        finish()

    @pl.when(flag_ref[0] == 0)
    def _():
        acc_sc[...] = jnp.zeros_like(acc_sc)
        m_sc[...] = jnp.full_like(m_sc, -jnp.inf)

        def body(c, carry):
            s, v = scores(c)
            m_prev = m_sc[...]
            m_new = jnp.maximum(m_prev, jnp.max(s, axis=-1, keepdims=True))
            p = jnp.exp2(s - m_new).astype(BF16)
            acc_sc[...] = jnp.exp2(m_prev - m_new) * acc_sc[...] + jnp.dot(p, v, preferred_element_type=F32)
            m_sc[...] = m_new
            return carry

        lax.fori_loop(0, nk, body, 0)
        finish()


def _attention(bounded_flag, qkv_x, qkv_c):
    b, l, _ = qkv_x.shape
    lc = qkv_c.shape[1]
    lk = lc + l
    tq = ATTN_Q_TILE
    gw = ATTN_GROUPS * HEAD_DIM
    rows = ATTN_GROUPS * tq
    k_col = ATTN_WIDTH // HEAD_DIM
    v_col = k_col + ATTN_KV_HEADS
    lat = lambda col: pl.BlockSpec((1, l, HEAD_DIM), lambda bi, hi, qi: (bi, 0, hi + col))
    ctx = lambda col: pl.BlockSpec((1, lc, HEAD_DIM), lambda bi, hi, qi: (bi, 0, hi + col))
    return pl.pallas_call(
        _attn_kernel,
        grid=(b, ATTN_KV_HEADS, l // tq),
        in_specs=[
            pl.BlockSpec(memory_space=pltpu.SMEM),
            pl.BlockSpec((1, tq, gw), lambda bi, hi, qi: (bi, qi, hi)),
            ctx(k_col), lat(k_col), ctx(v_col), lat(v_col),
        ],
        out_specs=pl.BlockSpec((1, tq, gw), lambda bi, hi, qi: (bi, qi, hi)),
        out_shape=jax.ShapeDtypeStruct((b, l, ATTN_WIDTH), BF16),
        scratch_shapes=[
            pltpu.VMEM((lk, HEAD_DIM), BF16),
            pltpu.VMEM((lk, 2 * HEAD_DIM), BF16),
            pltpu.VMEM((rows, HEAD_DIM), BF16),
            pltpu.VMEM((rows, 2 * HEAD_DIM), F32),
            pltpu.VMEM((rows, 1), F32),
        ],
        compiler_params=_params("parallel", "parallel", "arbitrary"),
        name="attention",
    )(bounded_flag, qkv_x, qkv_c, qkv_x, qkv_c, qkv_x)


def _ret_kernel(lg_ref, q_ref, k_ref, v_ref, g_ref, kc_ref, vc_ref, o_ref, uf_sc, sb_sc):
    c_len = RET_CHUNK
    n_chunks = q_ref.shape[1] // c_len
    n_ctx = kc_ref.shape[1]
    head = pl.program_id(1)
    lgf = lg_ref[0, head]
    lgb = lg_ref[1, head]

    row = lax.broadcasted_iota(jnp.int32, (c_len, 1), 0).astype(F32)
    vdec_f = jnp.exp((c_len - 1.0 - row) * lgf)
    vdec_b = jnp.exp(row * lgb)
    qdec_f = jnp.exp((row + 1.0) * lgf)
    qdec_b = jnp.exp((c_len - row) * lgb)
    chunk_f = jnp.exp(jnp.full((1, RET_DIM), c_len, F32) * lgf)
    chunk_b = jnp.exp(jnp.full((1, RET_DIM), c_len, F32) * lgb)
    diff = (lax.broadcasted_iota(jnp.int32, (c_len, c_len), 0)
            - lax.broadcasted_iota(jnp.int32, (c_len, c_len), 1)).astype(F32)
    decay = (jnp.where(diff >= 0, jnp.exp(jnp.maximum(diff, 0.0) * lgf), 0.0)
             + jnp.where(diff <= 0, jnp.exp(jnp.maximum(-diff, 0.0) * lgb), 0.0))

    def kv_outer(k, v, dec_f, dec_b):
        vf = v.astype(F32)
        v2 = jnp.concatenate([(vf * dec_f).astype(BF16), (vf * dec_b).astype(BF16)], axis=1)
        return lax.dot_general(k, v2, (((0,), (0,)), ((), ())), preferred_element_type=F32)

    crow = lax.broadcasted_iota(jnp.int32, (n_ctx, 1), 0).astype(F32)
    s0 = kv_outer(kc_ref[0], vc_ref[0], jnp.exp((n_ctx - 1.0 - crow) * lgf), jnp.exp(crow * lgb))

    def back_body(t, sb):
        c = n_chunks - 1 - t
        start = pl.multiple_of(c * c_len, c_len)
        sb_sc[c] = sb
        u = kv_outer(k_ref[0, pl.ds(start, c_len), :], v_ref[0, pl.ds(start, c_len), :], vdec_f, vdec_b)
        uf_sc[c] = u[:, :RET_DIM]
        return sb * chunk_b + u[:, RET_DIM:]

    lax.fori_loop(0, n_chunks, back_body, s0[:, RET_DIM:], unroll=RET_UNROLL)

    def fwd_body(c, sf):
        start = pl.multiple_of(c * c_len, c_len)
        q = q_ref[0, pl.ds(start, c_len), :]
        k = k_ref[0, pl.ds(start, c_len), :]
        v = v_ref[0, pl.ds(start, c_len), :]
        inner = lax.dot_general(q, k, (((1,), (1,)), ((), ())), preferred_element_type=F32)
        y = jnp.dot((inner * decay).astype(BF16), v, preferred_element_type=F32)
        states = jnp.concatenate([sf.astype(BF16), sb_sc[c].astype(BF16)], axis=1)
        cross = jnp.dot(q, states, preferred_element_type=F32)
        y = y + cross[:, :RET_DIM] * qdec_f + cross[:, RET_DIM:] * qdec_b
        gate = g_ref[0, pl.ds(start, c_len), :]
        o_ref[0, pl.ds(start, c_len), :] = (gate * _sigmoid(gate) * _rms(y)).astype(BF16)
        return sf * chunk_f + uf_sc[c]

    lax.fori_loop(0, n_chunks, fwd_body, s0[:, :RET_DIM], unroll=RET_UNROLL)


def _retention(log_gamma, ret_x, gate_x, ret_c):
    b, l, _ = ret_x.shape
    lc = ret_c.shape[1]
    n_chunks = l // RET_CHUNK
    seq = lambda off: pl.BlockSpec((1, l, RET_DIM), lambda bi, hi: (bi, 0, hi + off))
    ctx = lambda off: pl.BlockSpec((1, lc, RET_DIM), lambda bi, hi: (bi, 0, hi + off))
    return pl.pallas_call(
        _ret_kernel,
        grid=(b, RET_HEADS),
        in_specs=[
            pl.BlockSpec(memory_space=pltpu.SMEM),
            seq(0), seq(RET_HEADS), seq(2 * RET_HEADS), seq(0),
            ctx(RET_HEADS), ctx(2 * RET_HEADS),
        ],
        out_specs=seq(0),
        out_shape=jax.ShapeDtypeStruct((b, l, RET_WIDTH), BF16),
        scratch_shapes=[
            pltpu.VMEM((n_chunks, RET_DIM, RET_DIM), F32),
            pltpu.VMEM((n_chunks, RET_DIM, RET_DIM), F32),
        ],
        compiler_params=_params("parallel", "arbitrary"),
        name="retention",
    )(log_gamma, ret_x, ret_x, ret_x, gate_x, ret_c, ret_c)


def _merge_kernel(h_ref, n_ref, ya_ref, yr_ref, mod_ref, wga_ref, wgb_ref, wpa_ref, wpr_ref, wo_ref, out_ref):
    n = n_ref[...]
    ya = ya_ref[...]
    yr = yr_ref[...]
    for c in range(0, D_MODEL, MERGE_TILE):
        sl = slice(c, c + MERGE_TILE)
        ga = jnp.dot(n, wga_ref[:, sl], preferred_element_type=F32)
        gb = jnp.dot(n, wgb_ref[:, sl], preferred_element_type=F32)
        pa = jnp.dot(ya, wpa_ref[:, sl], preferred_element_type=F32)
        pr = jnp.dot(yr, wpr_ref[:, sl], preferred_element_type=F32)
        z = (_sigmoid(ga) * pa + _sigmoid(gb) * pr).astype(BF16)
        p = jnp.dot(z, wo_ref[sl, :], preferred_element_type=F32)
        if c == 0:
            out_ref[...] = p
        else:
            out_ref[...] += p
    out_ref[...] = h_ref[...] + mod_ref[0, 0, 0:1, :] * out_ref[...]


def _merge(h, n, ya, yr, mod, w_ga, w_gb, w_pa, w_pr, w_out, *, rows_per_mod):
    t = h.shape[0]
    tm = MERGE_TOKEN_TILE
    tiles_per_row = rows_per_mod // tm
    mod = mod[:, 5:6]
    resident = lambda w: pl.BlockSpec(w.shape, lambda i: (0, 0), pipeline_mode=pl.Buffered(1))
    return pl.pallas_call(
        _merge_kernel,
        grid=(t // tm,),
        in_specs=[
            pl.BlockSpec((tm, D_MODEL), lambda i: (i, 0)),
            pl.BlockSpec((tm, D_MODEL), lambda i: (i, 0)),
            pl.BlockSpec((tm, ATTN_WIDTH), lambda i: (i, 0)),
            pl.BlockSpec((tm, RET_WIDTH), lambda i: (i, 0)),
            pl.BlockSpec((1, 1, ROW_BLOCK, D_MODEL), lambda i: (i // tiles_per_row, 0, 0, 0)),
            resident(w_ga), resident(w_gb), resident(w_pa), resident(w_pr), resident(w_out),
        ],
        out_specs=pl.BlockSpec((tm, D_MODEL), lambda i: (i, 0)),
        out_shape=jax.ShapeDtypeStruct((t, D_MODEL), F32),
        compiler_params=_params("parallel"),
        name="merge",
    )(h, n, ya, yr, mod, w_ga, w_gb, w_pa, w_pr, w_out)


def _rope_tables(seq_len):
    rows = seq_len // GRID_W
    row = jnp.repeat(jnp.arange(rows, dtype=F32), GRID_W)
    col = jnp.tile(jnp.arange(GRID_W, dtype=F32), rows)
    half = HEAD_DIM // 2
    inv_freq = ROPE_THETA ** (-jnp.arange(0, half, 2, dtype=F32) / half)
    ang = jnp.concatenate([row[:, None] * inv_freq, col[:, None] * inv_freq], axis=-1)
    cos, sin = jnp.cos(ang), jnp.sin(ang)
    return jnp.concatenate([cos, cos], axis=-1), jnp.concatenate([-sin, sin], axis=-1)


def _deinterleave(t):
    lead = t.shape[:-1]
    t = t.reshape(lead + (-1, HEAD_DIM // 2, 2))
    return jnp.swapaxes(t, -1, -2).reshape(lead + (-1,))


def kernel(x, c, ctx, c_ctx, w_ada, b_ada, ffn1_w_in, ffn1_w_out, mix_w_in, attn_q_gain, attn_k_gain,
           ret_decay_logit, w_proj_attn, w_proj_ret, mix_w_out, ffn2_w_in, ffn2_w_out, final_norm):
    batch, seq_len, _ = x.shape
    assert w_ada.shape[0] == 1, "single-layer block"
    assert seq_len % FFN_TOKEN_TILE == 0 and (batch * CTX_LEN) % TOKEN_TILE == 0
    tiles_per_row = seq_len // TOKEN_TILE

    w1_in, w1_out = ffn1_w_in[0].astype(BF16), ffn1_w_out[0].astype(BF16)
    w2_in, w2_out = ffn2_w_in[0].astype(BF16), ffn2_w_out[0].astype(BF16)
    w_mix = mix_w_in[0]
    w_qkv = jnp.concatenate([_deinterleave(w_mix[:, :ATTN_WIDTH + KV_WIDTH]),
                             w_mix[:, ATTN_WIDTH + KV_WIDTH:QKV_WIDTH]], axis=1).astype(BF16)
    w_ret = w_mix[:, RET_OFF:GR_OFF].astype(BF16)
    w_gr = w_mix[:, GR_OFF:GA_OFF].astype(BF16)
    w_ga = w_mix[:, GA_OFF:GB_OFF].astype(BF16)
    w_gb = w_mix[:, GB_OFF:].astype(BF16)
    w_pa, w_pr = w_proj_attn[0].astype(BF16), w_proj_ret[0].astype(BF16)
    w_mo = mix_w_out[0].astype(BF16)

    cond = jnp.zeros((8, D_MODEL), F32).at[:batch].set(c).at[batch].set(c_ctx)
    mod = _adaln(cond, w_ada[0], b_ada).reshape(8, N_MOD, 1, D_MODEL)
    mod = jnp.broadcast_to(mod, (8, N_MOD, ROW_BLOCK, D_MODEL))

    x2 = x.reshape(batch * seq_len, D_MODEL)
    c2 = ctx.reshape(batch * CTX_LEN, D_MODEL)
    h1, n2 = _ffn(x2, mod, w1_in, w1_out, mod_base=0, rows_per_mod=seq_len, row_offset=0,
                  tm=FFN_TOKEN_TILE, tf=FF_TILE, emit_next=True)
    n2c, = _ffn(c2, mod, w1_in, w1_out, mod_base=0, rows_per_mod=batch * CTX_LEN, row_offset=batch,
                tm=batch * CTX_LEN, tf=FF_TILE, emit_h=False, emit_next=True)

    cos, sin = _rope_tables(seq_len)
    gq = _deinterleave(attn_q_gain[0].reshape(1, HEAD_DIM))
    gk = _deinterleave(attn_k_gain[0].reshape(1, HEAD_DIM))
    qkv_x = _qkv_proj(n2, w_qkv, cos, sin, gq, gk, pos_tiles=tiles_per_row)
    ones = jnp.ones((TOKEN_TILE, HEAD_DIM), F32)
    qkv_c = _qkv_proj(n2c, w_qkv, ones, jnp.zeros_like(ones), gq, gk, pos_tiles=1)

    ret_scale = jnp.concatenate([jnp.ones((1, RET_WIDTH), F32),
                                 jnp.full((1, RET_WIDTH), RET_DIM ** -0.5, F32),
                                 jnp.ones((1, RET_WIDTH), F32)], axis=1)
    ret_x = _scaled_mm(n2, w_ret, ret_scale, BF16, "ret_proj")
    ret_c = _scaled_mm(n2c, w_ret, ret_scale, BF16, "ret_proj")
    gate_x = _scaled_mm(n2, w_gr, jnp.ones((1, RET_WIDTH), F32), F32, "ret_gate_proj")

    qkv_x = qkv_x.reshape(batch, seq_len, QKV_WIDTH)
    qkv_c = qkv_c.reshape(batch, CTX_LEN, QKV_WIDTH)
    score_bound = HEAD_DIM ** 0.5 * jnp.max(jnp.abs(gq)) * jnp.max(jnp.abs(gk))
    ya = _attention((score_bound <= ATTN_SAFE_SCORE).astype(jnp.int32).reshape(1), qkv_x, qkv_c)

    log_gamma = jax.nn.log_sigmoid(ret_decay_logit[0].astype(F32))
    yr = _retention(log_gamma, ret_x.reshape(batch, seq_len, 3 * RET_WIDTH),
                    gate_x.reshape(batch, seq_len, RET_WIDTH),
                    ret_c.reshape(batch, CTX_LEN, 3 * RET_WIDTH))

    h2 = _merge(h1, n2, ya.reshape(batch * seq_len, ATTN_WIDTH), yr.reshape(batch * seq_len, RET_WIDTH), mod,
                w_ga, w_gb, w_pa, w_pr, w_mo, rows_per_mod=seq_len)
    out = _ffn(h2, mod, w2_in, w2_out, mod_base=6, rows_per_mod=seq_len, row_offset=0,
               tm=FFN_TOKEN_TILE, tf=FF_TILE,
               final_norm=jnp.broadcast_to(final_norm.reshape(1, D_MODEL), (ROW_BLOCK, D_MODEL)))
    return out[0].reshape(batch, seq_len, D_MODEL)
```

```python
import functools

import jax
import jax.numpy as jnp
import numpy as np
from jax import lax
from jax.experimental import pallas as pl
from jax.experimental.pallas import tpu as pltpu

D_MODEL = 2048
CTX_LEN = 256
GRID_W = 64
HEAD_DIM = 128
LANES = 128
ATTN_Q_HEADS = 8
ATTN_KV_HEADS = 2
ATTN_GROUPS = ATTN_Q_HEADS // ATTN_KV_HEADS
ATTN_WIDTH = ATTN_Q_HEADS * HEAD_DIM
KV_WIDTH = ATTN_KV_HEADS * HEAD_DIM
RET_HEADS = 8
RET_DIM = 128
RET_WIDTH = RET_HEADS * RET_DIM
D_FF = 5632
ROPE_THETA = 10000.0
NORM_EPS = 1e-6
N_MOD = 9

QKV_WIDTH = ATTN_WIDTH + 2 * KV_WIDTH
RET_OFF = QKV_WIDTH
GR_OFF = RET_OFF + 3 * RET_WIDTH
GA_OFF = GR_OFF + RET_WIDTH
GB_OFF = GA_OFF + D_MODEL

F32 = jnp.float32
BF16 = jnp.bfloat16

VMEM_LIMIT_BYTES = 64 * 1024 * 1024

ADALN_TILE = 1024
FF_TILE = 512
FFN_TOKEN_TILE = 1024
ROW_BLOCK = 16
ROW_UNROLL = 4
PROJ_TOKEN_TILE = 1024
MERGE_TOKEN_TILE = 512
MERGE_TILE = 512
ATTN_Q_TILE = 512
ATTN_KV_TILE = 768
RET_CHUNK = 256
RET_UNROLL = 8

Q_SCALE = HEAD_DIM ** -0.5 * 1.4426950408889634
ATTN_SAFE_SCORE = 32.0


def _params(*sem):
    return pltpu.CompilerParams(dimension_semantics=sem, vmem_limit_bytes=VMEM_LIMIT_BYTES)


def _rms(x):
    return x * lax.rsqrt(jnp.mean(x * x, axis=-1, keepdims=True) + NORM_EPS)


def _sigmoid(x):
    return 1.0 / (1.0 + jnp.exp(-x))


def _adaln_kernel(c_ref, w_ref, b_ref, o_ref):
    chunk = pl.program_id(0) // (D_MODEL // ADALN_TILE)
    c = c_ref[...]
    s = (c * _sigmoid(c)).astype(BF16)
    y = jnp.dot(s, w_ref[...].astype(BF16), preferred_element_type=F32) + b_ref[...]
    y = y + jnp.where(chunk % 3 == 1, 1.0, 0.0)
    o_ref[...] = y * jnp.where((chunk == 2) | (chunk == 8), 0.5, 1.0)


def _adaln(cond, w, b):
    n = w.shape[1]
    tn = ADALN_TILE
    return pl.pallas_call(
        _adaln_kernel,
        grid=(n // tn,),
        in_specs=[
            pl.BlockSpec((8, D_MODEL), lambda j: (0, 0)),
            pl.BlockSpec((D_MODEL, tn), lambda j: (0, j)),
            pl.BlockSpec((1, tn), lambda j: (0, j)),
        ],
        out_specs=pl.BlockSpec((8, tn), lambda j: (0, j)),
        out_shape=jax.ShapeDtypeStruct((8, n), F32),
        compiler_params=_params("arbitrary"),
        name="adaln",
    )(cond, w, b)


def _ffn_kernel(*refs, mod_base, emit_h, emit_next, final):
    h_ref, mod_ref, wa_ref, wb_ref, wo_ref = refs[:5]
    refs = refs[5:]
    if final:
        fn_ref, refs = refs[0], refs[1:]
    if emit_h:
        out_ref, refs = refs[0], refs[1:]
    if emit_next:
        nxt_ref, refs = refs[0], refs[1:]
    xn_sc, inv_sc = refs[:2]
    acc_ref = out_ref if emit_h else refs[2]
    j = pl.program_id(1)
    n_row_blocks = h_ref.shape[0] // ROW_BLOCK

    lane_tiles = [slice(k, k + LANES) for k in range(0, D_MODEL, LANES)]

    def row_block(r):
        return pl.ds(pl.multiple_of(r * ROW_BLOCK, ROW_BLOCK), ROW_BLOCK)

    def inv_rms(x):
        inv = lax.rsqrt(jnp.mean(x * x, axis=-1, keepdims=True) + NORM_EPS)
        return jnp.broadcast_to(inv, (x.shape[0], LANES))

    def for_row_blocks(body, unroll):
        def step(r, carry):
            body(row_block(r))
            return carry

        lax.fori_loop(0, n_row_blocks, step, 0, unroll=unroll)

    def prologue():
        def stats(rows):
            inv_sc[rows, :] = inv_rms(h_ref[rows, :])

        def prenorm(rows):
            inv = inv_sc[rows, :]
            for sl in lane_tiles:
                n = h_ref[rows, sl] * inv * mod_ref[0, mod_base + 1, :, sl] + mod_ref[0, mod_base, :, sl]
                xn_sc[rows, sl] = n.astype(BF16)

        for_row_blocks(stats, True)
        for_row_blocks(prenorm, ROW_UNROLL)

    def matmuls(first):
        xn = xn_sc[...]
        half = wa_ref.shape[1] // 2
        for k, sl in enumerate((slice(0, half), slice(half, 2 * half))):
            a = jnp.dot(xn, wa_ref[:, sl], preferred_element_type=F32)
            b = jnp.dot(xn, wb_ref[:, sl], preferred_element_type=F32)
            act = (a * _sigmoid(a) * b).astype(BF16)
            p = jnp.dot(act, wo_ref[sl, :], preferred_element_type=F32)
            if first and k == 0:
                acc_ref[...] = p
            else:
                acc_ref[...] += p

    def epilogue():
        def residual(rows):
            h = h_ref[rows, :] + mod_ref[0, mod_base + 2] * acc_ref[rows, :]
            acc_ref[rows, :] = h
            inv_sc[rows, :] = inv_rms(h)

        def postnorm(rows):
            inv = inv_sc[rows, :]
            for sl in lane_tiles:
                n = acc_ref[rows, sl] * inv
                if final:
                    out_ref[rows, sl] = n * fn_ref[:, sl]
                else:
                    n = n * mod_ref[0, mod_base + 4, :, sl] + mod_ref[0, mod_base + 3, :, sl]
                    nxt_ref[rows, sl] = n.astype(BF16)

        for_row_blocks(residual, True)
        if final or emit_next:
            for_row_blocks(postnorm, ROW_UNROLL)

    last = pl.num_programs(1) - 1

    @pl.when(j == 0)
    def _():
        prologue()
        matmuls(first=True)

    @pl.when((j > 0) & (j < last))
    def _():
        matmuls(first=False)

    @pl.when(j == last)
    def _():
        matmuls(first=False)
        epilogue()


def _ffn(h, mod, w_in, w_out, *, mod_base, rows_per_mod, row_offset, tm, tf, emit_h=True, emit_next=False,
         final_norm=None):
    t = h.shape[0]
    nf = D_FF // tf
    final = final_norm is not None
    tiles_per_row = rows_per_mod // tm
    n_mod = 5 if emit_next else 3
    mod = mod[:, mod_base:mod_base + n_mod]
    mod_map = lambda i, j: (i // tiles_per_row + row_offset, 0, 0, 0)
    in_specs = [
        pl.BlockSpec((tm, D_MODEL), lambda i, j: (i, 0)),
        pl.BlockSpec((1, n_mod, ROW_BLOCK, D_MODEL), mod_map),
        pl.BlockSpec((D_MODEL, tf), lambda i, j: (0, j)),
        pl.BlockSpec((D_MODEL, tf), lambda i, j: (0, j + nf)),
        pl.BlockSpec((tf, D_MODEL), lambda i, j: (j, 0)),
    ]
    args = [h, mod, w_in, w_in, w_out]
    if final:
        in_specs.append(pl.BlockSpec((ROW_BLOCK, D_MODEL), lambda i, j: (0, 0)))
        args.append(final_norm)
    out_specs, out_shape = [], []
    if emit_h:
        out_specs.append(pl.BlockSpec((tm, D_MODEL), lambda i, j: (i, 0)))
        out_shape.append(jax.ShapeDtypeStruct((t, D_MODEL), F32))
    if emit_next:
        out_specs.append(pl.BlockSpec((tm, D_MODEL), lambda i, j: (i, 0)))
        out_shape.append(jax.ShapeDtypeStruct((t, D_MODEL), BF16))
    return pl.pallas_call(
        functools.partial(_ffn_kernel, mod_base=0, emit_h=emit_h, emit_next=emit_next, final=final),
        grid=(t // tm, nf),
        in_specs=in_specs,
        out_specs=out_specs,
        out_shape=out_shape,
        scratch_shapes=[pltpu.VMEM((tm, D_MODEL), BF16), pltpu.VMEM((tm, LANES), F32)]
        + ([] if emit_h else [pltpu.VMEM((tm, D_MODEL), F32)]),
        compiler_params=_params("parallel", "arbitrary"),
        name="ffn_final" if final else "ffn",
    )(*args)


def _qkv_kernel(n_ref, w_ref, cos_ref, sin_ref, gq_ref, gk_ref, o_ref, *y_bufs, n_tiles):
    i = pl.program_id(0)

    def finish(y_ref):
        cos = cos_ref[...]
        sin = sin_ref[...]
        gq = gq_ref[...] * Q_SCALE
        gk = gk_ref[...]
        for hh in range(ATTN_Q_HEADS + ATTN_KV_HEADS):
            sl = slice(hh * HEAD_DIM, (hh + 1) * HEAD_DIM)
            t = _rms(y_ref[:, sl]) * (gq if hh < ATTN_Q_HEADS else gk)
            o_ref[:, sl] = (t * cos + pltpu.roll(t, HEAD_DIM // 2, 1) * sin).astype(BF16)
        o_ref[:, ATTN_WIDTH + KV_WIDTH:] = y_ref[:, ATTN_WIDTH + KV_WIDTH:].astype(BF16)

    @pl.when(i == 0)
    def _():
        y_bufs[1][...] = jnp.zeros_like(y_bufs[1])

    for parity in range(2):
        @pl.when((i < n_tiles) & (i % 2 == parity))
        def _(y_cur=y_bufs[parity], y_prev=y_bufs[1 - parity]):
            finish(y_prev)
            y_cur[...] = jnp.dot(n_ref[...], w_ref[...], preferred_element_type=F32)

    @pl.when(i == n_tiles)
    def _():
        finish(y_bufs[1 - n_tiles % 2])


def _qkv_proj(n, w, cos, sin, gq, gk, *, pos_tiles):
    t = n.shape[0]
    tm = min(t, PROJ_TOKEN_TILE)
    n_tiles = t // tm
    prev_tile = lambda i: jnp.maximum(i - 1, 0)
    return pl.pallas_call(
        functools.partial(_qkv_kernel, n_tiles=n_tiles),
        grid=(n_tiles + 1,),
        in_specs=[
            pl.BlockSpec((tm, D_MODEL), lambda i: (jnp.minimum(i, n_tiles - 1), 0)),
            pl.BlockSpec((D_MODEL, QKV_WIDTH), lambda i: (0, 0)),
            pl.BlockSpec((tm, HEAD_DIM), lambda i: (prev_tile(i) % pos_tiles, 0)),
            pl.BlockSpec((tm, HEAD_DIM), lambda i: (prev_tile(i) % pos_tiles, 0)),
            pl.BlockSpec((1, HEAD_DIM), lambda i: (0, 0)),
            pl.BlockSpec((1, HEAD_DIM), lambda i: (0, 0)),
        ],
        out_specs=pl.BlockSpec((tm, QKV_WIDTH), lambda i: (prev_tile(i), 0)),
        out_shape=jax.ShapeDtypeStruct((t, QKV_WIDTH), BF16),
        scratch_shapes=[pltpu.VMEM((tm, QKV_WIDTH), F32), pltpu.VMEM((tm, QKV_WIDTH), F32)],
        compiler_params=_params("arbitrary"),
        name="qkv_proj",
    )(n, w, cos, sin, gq, gk)


def _scaled_mm_kernel(n_ref, w_ref, s_ref, o_ref):
    y = jnp.dot(n_ref[...], w_ref[...], preferred_element_type=F32)
    o_ref[...] = (y * s_ref[...]).astype(o_ref.dtype)


def _scaled_mm(n, w, col_scale, out_dtype, name):
    t = n.shape[0]
    nn = w.shape[1]
    tm = min(t, PROJ_TOKEN_TILE)
    return pl.pallas_call(
        _scaled_mm_kernel,
        grid=(t // tm,),
        in_specs=[
            pl.BlockSpec((tm, D_MODEL), lambda i: (i, 0)),
            pl.BlockSpec((D_MODEL, nn), lambda i: (0, 0), pipeline_mode=pl.Buffered(1)),
            pl.BlockSpec((1, nn), lambda i: (0, 0)),
        ],
        out_specs=pl.BlockSpec((tm, nn), lambda i: (i, 0)),
        out_shape=jax.ShapeDtypeStruct((t, nn), out_dtype),
        compiler_params=_params("parallel"),
        name=name,
    )(n, w, col_scale)


def _attn_kernel(flag_ref, q_ref, kc_ref, kx_ref, vc_ref, vx_ref, o_ref,
                 k_sc, v_sc, qs_sc, acc_sc, m_sc):
    tq = q_ref.shape[1]
    tk = ATTN_KV_TILE
    n_ctx = kc_ref.shape[1]
    lk = k_sc.shape[0]
    nk = lk // tk

    @pl.when(pl.program_id(2) == 0)
    def _():
        k_sc[:n_ctx, :] = kc_ref[0]
        k_sc[n_ctx:, :] = kx_ref[0]
        v_sc[:n_ctx, :HEAD_DIM] = vc_ref[0]
        v_sc[n_ctx:, :HEAD_DIM] = vx_ref[0]
        lane = lax.broadcasted_iota(jnp.int32, (n_ctx, HEAD_DIM), 1)
        ones_col = jnp.where(lane == 0, 1.0, 0.0).astype(BF16)
        for r in range(lk // n_ctx):
            v_sc[r * n_ctx:(r + 1) * n_ctx, HEAD_DIM:] = ones_col

    for g in range(ATTN_GROUPS):
        qs_sc[g * tq:(g + 1) * tq, :] = q_ref[0, :, g * HEAD_DIM:(g + 1) * HEAD_DIM]

    def scores(c):
        start = pl.multiple_of(c * tk, tk)
        k = k_sc[pl.ds(start, tk), :]
        s = lax.dot_general(qs_sc[...], k, (((1,), (1,)), ((), ())), preferred_element_type=F32)
        return s, v_sc[pl.ds(start, tk), :]

    def finish():
        acc = acc_sc[...]
        out = acc[:, :HEAD_DIM] / acc[:, HEAD_DIM:HEAD_DIM + 1]
        for g in range(ATTN_GROUPS):
            o_ref[0, :, g * HEAD_DIM:(g + 1) * HEAD_DIM] = out[g * tq:(g + 1) * tq, :].astype(BF16)

    @pl.when(flag_ref[0] != 0)
    def _():
        acc_sc[...] = jnp.zeros_like(acc_sc)

        def body(c, carry):
            s, v = scores(c)
            acc_sc[...] += jnp.dot(jnp.exp2(s).astype(BF16), v, preferred_element_type=F32)
            return carry

        lax.fori_loop(0, nk, body, 0, unroll=True)
        finish()

    @pl.when(flag_ref[0] == 0)
    def _():
        acc_sc[...] = jnp.zeros_like(acc_sc)
        m_sc[...] = jnp.full_like(m_sc, -jnp.inf)

        def body(c, carry):
            s, v = scores(c)
            m_prev = m_sc[...]
            m_new = jnp.maximum(m_prev, jnp.max(s, axis=-1, keepdims=True))
            p = jnp.exp2(s - m_new).astype(BF16)
            acc_sc[...] = jnp.exp2(m_prev - m_new) * acc_sc[...] + jnp.dot(p, v, preferred_element_type=F32)
            m_sc[...] = m_new
            return carry

        lax.fori_loop(0, nk, body, 0)
        finish()


def _attention(bounded_flag, qkv_x, qkv_c):
    b, l, _ = qkv_x.shape
    lc = qkv_c.shape[1]
    lk = lc + l
    tq = ATTN_Q_TILE
    gw = ATTN_GROUPS * HEAD_DIM
    rows = ATTN_GROUPS * tq
    k_col = ATTN_WIDTH // HEAD_DIM
    v_col = k_col + ATTN_KV_HEADS
    lat = lambda col: pl.BlockSpec((1, l, HEAD_DIM), lambda bi, hi, qi: (bi, 0, hi + col))
    ctx = lambda col: pl.BlockSpec((1, lc, HEAD_DIM), lambda bi, hi, qi: (bi, 0, hi + col))
    return pl.pallas_call(
        _attn_kernel,
        grid=(b, ATTN_KV_HEADS, l // tq),
        in_specs=[
            pl.BlockSpec(memory_space=pltpu.SMEM),
            pl.BlockSpec((1, tq, gw), lambda bi, hi, qi: (bi, qi, hi)),
            ctx(k_col), lat(k_col), ctx(v_col), lat(v_col),
        ],
        out_specs=pl.BlockSpec((1, tq, gw), lambda bi, hi, qi: (bi, qi, hi)),
        out_shape=jax.ShapeDtypeStruct((b, l, ATTN_WIDTH), BF16),
        scratch_shapes=[
            pltpu.VMEM((lk, HEAD_DIM), BF16),
            pltpu.VMEM((lk, 2 * HEAD_DIM), BF16),
            pltpu.VMEM((rows, HEAD_DIM), BF16),
            pltpu.VMEM((rows, 2 * HEAD_DIM), F32),
            pltpu.VMEM((rows, 1), F32),
        ],
        compiler_params=_params("parallel", "parallel", "arbitrary"),
        name="attention",
    )(bounded_flag, qkv_x, qkv_c, qkv_x, qkv_c, qkv_x)


def _ret_kernel(lg_ref, q_ref, k_ref, v_ref, g_ref, kc_ref, vc_ref, o_ref, uf_sc, sb_sc):
    c_len = RET_CHUNK
    n_chunks = q_ref.shape[1] // c_len
    n_ctx = kc_ref.shape[1]
    head = pl.program_id(1)
    lgf = lg_ref[0, head]
    lgb = lg_ref[1, head]

    row = lax.broadcasted_iota(jnp.int32, (c_len, 1), 0).astype(F32)
    vdec_f = jnp.exp((c_len - 1.0 - row) * lgf)
    vdec_b = jnp.exp(row * lgb)
    qdec_f = jnp.exp((row + 1.0) * lgf)
    qdec_b = jnp.exp((c_len - row) * lgb)
    chunk_f = jnp.exp(jnp.full((1, RET_DIM), c_len, F32) * lgf)
    chunk_b = jnp.exp(jnp.full((1, RET_DIM), c_len, F32) * lgb)
    diff = (lax.broadcasted_iota(jnp.int32, (c_len, c_len), 0)
            - lax.broadcasted_iota(jnp.int32, (c_len, c_len), 1)).astype(F32)
    decay = (jnp.where(diff >= 0, jnp.exp(jnp.maximum(diff, 0.0) * lgf), 0.0)
             + jnp.where(diff <= 0, jnp.exp(jnp.maximum(-diff, 0.0) * lgb), 0.0))

    def kv_outer(k, v, dec_f, dec_b):
        vf = v.astype(F32)
        v2 = jnp.concatenate([(vf * dec_f).astype(BF16), (vf * dec_b).astype(BF16)], axis=1)
        return lax.dot_general(k, v2, (((0,), (0,)), ((), ())), preferred_element_type=F32)

    crow = lax.broadcasted_iota(jnp.int32, (n_ctx, 1), 0).astype(F32)
    s0 = kv_outer(kc_ref[0], vc_ref[0], jnp.exp((n_ctx - 1.0 - crow) * lgf), jnp.exp(crow * lgb))

    def back_body(t, sb):
        c = n_chunks - 1 - t
        start = pl.multiple_of(c * c_len, c_len)
        sb_sc[c] = sb
        u = kv_outer(k_ref[0, pl.ds(start, c_len), :], v_ref[0, pl.ds(start, c_len), :], vdec_f, vdec_b)
        uf_sc[c] = u[:, :RET_DIM]
        return sb * chunk_b + u[:, RET_DIM:]

    lax.fori_loop(0, n_chunks, back_body, s0[:, RET_DIM:], unroll=RET_UNROLL)

    def fwd_body(c, sf):
        start = pl.multiple_of(c * c_len, c_len)
        q = q_ref[0, pl.ds(start, c_len), :]
        k = k_ref[0, pl.ds(start, c_len), :]
        v = v_ref[0, pl.ds(start, c_len), :]
        inner = lax.dot_general(q, k, (((1,), (1,)), ((), ())), preferred_element_type=F32)
        y = jnp.dot((inner * decay).astype(BF16), v, preferred_element_type=F32)
        states = jnp.concatenate([sf.astype(BF16), sb_sc[c].astype(BF16)], axis=1)
        cross = jnp.dot(q, states, preferred_element_type=F32)
        y = y + cross[:, :RET_DIM] * qdec_f + cross[:, RET_DIM:] * qdec_b
        gate = g_ref[0, pl.ds(start, c_len), :]
        o_ref[0, pl.ds(start, c_len), :] = (gate * _sigmoid(gate) * _rms(y)).astype(BF16)
        return sf * chunk_f + uf_sc[c]

    lax.fori_loop(0, n_chunks, fwd_body, s0[:, :RET_DIM], unroll=RET_UNROLL)


def _retention(log_gamma, ret_x, gate_x, ret_c):
    b, l, _ = ret_x.shape
    lc = ret_c.shape[1]
    n_chunks = l // RET_CHUNK
    seq = lambda off: pl.BlockSpec((1, l, RET_DIM), lambda bi, hi: (bi, 0, hi + off))
    ctx = lambda off: pl.BlockSpec((1, lc, RET_DIM), lambda bi, hi: (bi, 0, hi + off))
    return pl.pallas_call(
        _ret_kernel,
        grid=(b, RET_HEADS),
        in_specs=[
            pl.BlockSpec(memory_space=pltpu.SMEM),
            seq(0), seq(RET_HEADS), seq(2 * RET_HEADS), seq(0),
            ctx(RET_HEADS), ctx(2 * RET_HEADS),
        ],
        out_specs=seq(0),
        out_shape=jax.ShapeDtypeStruct((b, l, RET_WIDTH), BF16),
        scratch_shapes=[
            pltpu.VMEM((n_chunks, RET_DIM, RET_DIM), F32),
            pltpu.VMEM((n_chunks, RET_DIM, RET_DIM), F32),
        ],
        compiler_params=_params("parallel", "arbitrary"),
        name="retention",
    )(log_gamma, ret_x, ret_x, ret_x, gate_x, ret_c, ret_c)


def _merge_kernel(h_ref, n_ref, ya_ref, yr_ref, mod_ref, wga_ref, wgb_ref, wpa_ref, wpr_ref, wo_ref, out_ref):
    n = n_ref[...]
    ya = ya_ref[...]
    yr = yr_ref[...]
    for c in range(0, D_MODEL, MERGE_TILE):
        sl = slice(c, c + MERGE_TILE)
        ga = jnp.dot(n, wga_ref[:, sl], preferred_element_type=F32)
        gb = jnp.dot(n, wgb_ref[:, sl], preferred_element_type=F32)
        pa = jnp.dot(ya, wpa_ref[:, sl], preferred_element_type=F32)
        pr = jnp.dot(yr, wpr_ref[:, sl], preferred_element_type=F32)
        z = (_sigmoid(ga) * pa + _sigmoid(gb) * pr).astype(BF16)
        p = jnp.dot(z, wo_ref[sl, :], preferred_element_type=F32)
        if c == 0:
            out_ref[...] = p
        else:
            out_ref[...] += p
    out_ref[...] = h_ref[...] + mod_ref[0, 0, 0:1, :] * out_ref[...]


def _merge(h, n, ya, yr, mod, w_ga, w_gb, w_pa, w_pr, w_out, *, rows_per_mod):
    t = h.shape[0]
    tm = MERGE_TOKEN_TILE
    tiles_per_row = rows_per_mod // tm
    mod = mod[:, 5:6]
    resident = lambda w: pl.BlockSpec(w.shape, lambda i: (0, 0), pipeline_mode=pl.Buffered(1))
    return pl.pallas_call(
        _merge_kernel,
        grid=(t // tm,),
        in_specs=[
            pl.BlockSpec((tm, D_MODEL), lambda i: (i, 0)),
            pl.BlockSpec((tm, D_MODEL), lambda i: (i, 0)),
            pl.BlockSpec((tm, ATTN_WIDTH), lambda i: (i, 0)),
            pl.BlockSpec((tm, RET_WIDTH), lambda i: (i, 0)),
            pl.BlockSpec((1, 1, ROW_BLOCK, D_MODEL), lambda i: (i // tiles_per_row, 0, 0, 0)),
            resident(w_ga), resident(w_gb), resident(w_pa), resident(w_pr), resident(w_out),
        ],
        out_specs=pl.BlockSpec((tm, D_MODEL), lambda i: (i, 0)),
        out_shape=jax.ShapeDtypeStruct((t, D_MODEL), F32),
        compiler_params=_params("parallel"),
        name="merge",
    )(h, n, ya, yr, mod, w_ga, w_gb, w_pa, w_pr, w_out)


def _rope_tables(seq_len):
    rows = seq_len // GRID_W
    row = np.repeat(np.arange(rows, dtype=np.float32), GRID_W)
    col = np.tile(np.arange(GRID_W, dtype=np.float32), rows)
    half = HEAD_DIM // 2
    inv_freq = np.float32(ROPE_THETA) ** (-np.arange(0, half, 2, dtype=np.float32) / np.float32(half))
    ang = np.concatenate([row[:, None] * inv_freq, col[:, None] * inv_freq], axis=-1)
    cos, sin = np.cos(ang), np.sin(ang)
    return (jnp.asarray(np.concatenate([cos, cos], axis=-1), F32),
            jnp.asarray(np.concatenate([-sin, sin], axis=-1), F32))


def _deinterleave(t):
    lead = t.shape[:-1]
    t = t.reshape(lead + (-1, HEAD_DIM // 2, 2))
    return jnp.swapaxes(t, -1, -2).reshape(lead + (-1,))


def kernel(x, c, ctx, c_ctx, w_ada, b_ada, ffn1_w_in, ffn1_w_out, mix_w_in, attn_q_gain, attn_k_gain,
           ret_decay_logit, w_proj_attn, w_proj_ret, mix_w_out, ffn2_w_in, ffn2_w_out, final_norm):
    batch, seq_len, _ = x.shape
    assert w_ada.shape[0] == 1, "single-layer block"
    assert seq_len % FFN_TOKEN_TILE == 0 and seq_len % PROJ_TOKEN_TILE == 0

    w1_in, w1_out = ffn1_w_in[0].astype(BF16), ffn1_w_out[0].astype(BF16)
    w2_in, w2_out = ffn2_w_in[0].astype(BF16), ffn2_w_out[0].astype(BF16)
    w_mix = mix_w_in[0]
    w_qkv = jnp.concatenate([_deinterleave(w_mix[:, :ATTN_WIDTH + KV_WIDTH]),
                             w_mix[:, ATTN_WIDTH + KV_WIDTH:QKV_WIDTH]], axis=1).astype(BF16)
    w_ret = w_mix[:, RET_OFF:GR_OFF].astype(BF16)
    w_gr = w_mix[:, GR_OFF:GA_OFF].astype(BF16)
    w_ga = w_mix[:, GA_OFF:GB_OFF].astype(BF16)
    w_gb = w_mix[:, GB_OFF:].astype(BF16)
    w_pa, w_pr = w_proj_attn[0].astype(BF16), w_proj_ret[0].astype(BF16)
    w_mo = mix_w_out[0].astype(BF16)

    cond = jnp.zeros((8, D_MODEL), F32).at[:batch].set(c).at[batch].set(c_ctx)
    mod = _adaln(cond, w_ada[0], b_ada).reshape(8, N_MOD, 1, D_MODEL)
    mod = jnp.broadcast_to(mod, (8, N_MOD, ROW_BLOCK, D_MODEL))

    x2 = x.reshape(batch * seq_len, D_MODEL)
    c2 = ctx.reshape(batch * CTX_LEN, D_MODEL)
    h1, n2 = _ffn(x2, mod, w1_in, w1_out, mod_base=0, rows_per_mod=seq_len, row_offset=0,
                  tm=FFN_TOKEN_TILE, tf=FF_TILE, emit_next=True)
    n2c, = _ffn(c2, mod, w1_in, w1_out, mod_base=0, rows_per_mod=batch * CTX_LEN, row_offset=batch,
                tm=batch * CTX_LEN, tf=FF_TILE, emit_h=False, emit_next=True)

    cos, sin = _rope_tables(seq_len)
    gq = _deinterleave(attn_q_gain[0].reshape(1, HEAD_DIM))
    gk = _deinterleave(attn_k_gain[0].reshape(1, HEAD_DIM))
    qkv_x = _qkv_proj(n2, w_qkv, cos, sin, gq, gk, pos_tiles=seq_len // PROJ_TOKEN_TILE)
    ones = jnp.ones((min(batch * CTX_LEN, PROJ_TOKEN_TILE), HEAD_DIM), F32)
    qkv_c = _qkv_proj(n2c, w_qkv, ones, jnp.zeros_like(ones), gq, gk, pos_tiles=1)

    ret_scale = jnp.concatenate([jnp.ones((1, RET_WIDTH), F32),
                                 jnp.full((1, RET_WIDTH), RET_DIM ** -0.5, F32),
                                 jnp.ones((1, RET_WIDTH), F32)], axis=1)
    ret_x = _scaled_mm(n2, w_ret, ret_scale, BF16, "ret_proj")
    ret_c = _scaled_mm(n2c, w_ret, ret_scale, BF16, "ret_proj")
    gate_x = _scaled_mm(n2, w_gr, jnp.ones((1, RET_WIDTH), F32), F32, "ret_gate_proj")

    qkv_x = qkv_x.reshape(batch, seq_len, QKV_WIDTH)
    qkv_c = qkv_c.reshape(batch, CTX_LEN, QKV_WIDTH)
    score_bound = HEAD_DIM ** 0.5 * jnp.max(jnp.abs(gq)) * jnp.max(jnp.abs(gk))
    ya = _attention((score_bound <= ATTN_SAFE_SCORE).astype(jnp.int32).reshape(1), qkv_x, qkv_c)

    log_gamma = jax.nn.log_sigmoid(ret_decay_logit[0].astype(F32))
    yr = _retention(log_gamma, ret_x.reshape(batch, seq_len, 3 * RET_WIDTH),
                    gate_x.reshape(batch, seq_len, RET_WIDTH),
                    ret_c.reshape(batch, CTX_LEN, 3 * RET_WIDTH))

    h2 = _merge(h1, n2, ya.reshape(batch * seq_len, ATTN_WIDTH), yr.reshape(batch * seq_len, RET_WIDTH), mod,
                w_ga, w_gb, w_pa, w_pr, w_mo, rows_per_mod=seq_len)
    out = _ffn(h2, mod, w2_in, w2_out, mod_base=6, rows_per_mod=seq_len, row_offset=0,
               tm=FFN_TOKEN_TILE, tf=FF_TILE,
               final_norm=jnp.broadcast_to(final_norm.reshape(1, D_MODEL), (ROW_BLOCK, D_MODEL)))
    return out[0].reshape(batch, seq_len, D_MODEL)
```
